```python
import math
import jax, jax.numpy as jnp
from jax import lax
import numpy as np

D_MODEL = 1024
BATCH = 4
SEQ = 4096
DEPTH = 4

CHUNK = 64
Q_BLOCK = 128
D_MIX = D_MODEL
HEAD_DIM = 64
D_FOX = D_MIX // 2
FOX_HEADS = D_FOX // HEAD_DIM
D_S5 = D_MIX // 4
S5_GROUP_CH = 16
S5_GROUPS = D_S5 // S5_GROUP_CH
S5_STATE = 64
D_RET = D_MIX // 4
RET_HEADS = D_RET // HEAD_DIM
ROPE_BASE = 10000.0
EPS = 1e-6
SPLIT_SIZES = (D_FOX, D_FOX, D_FOX, FOX_HEADS, D_S5, D_RET, D_RET, D_RET, D_MIX)
D_IN_PROJ = 3 * D_FOX + FOX_HEADS + D_S5 + 3 * D_RET + D_MIX

kernel_name = 'hybrid_fox_s5_retention_block'


def rms_norm(x, g):
    xf = x.astype(jnp.float32)
    y = xf * lax.rsqrt(jnp.mean(xf * xf, axis=-1, keepdims=True) + EPS)
    return (y * g.astype(jnp.float32)).astype(x.dtype)


def split_cols(proj):
    idx = [int(i) for i in np.cumsum(SPLIT_SIZES)[:-1]]
    return jnp.split(proj, idx, axis=-1)


def rotary(x):
    L, d = x.shape[1], x.shape[3]
    half = d // 2
    freqs = ROPE_BASE ** (-jnp.arange(half, dtype=jnp.float32) / half)
    ang = jnp.arange(L, dtype=jnp.float32)[:, None] * freqs[None, :]
    cos = jnp.cos(ang)[None, :, None, :]
    sin = jnp.sin(ang)[None, :, None, :]
    xf = x.astype(jnp.float32)
    x1, x2 = xf[..., :half], xf[..., half:]
    return jnp.concatenate([x1 * cos - x2 * sin, x1 * sin + x2 * cos], axis=-1)


def forgetting_attention(q, k, v, logf):
    B, L, H, dh = q.shape
    nb = L // Q_BLOCK
    scale = 1.0 / math.sqrt(dh)
    qh = q.transpose(0, 2, 1, 3)
    kh = k.transpose(0, 2, 1, 3)
    vh = v.transpose(0, 2, 1, 3)
    c = jnp.cumsum(logf, axis=1).transpose(0, 2, 1)
    qb = qh.reshape(B, H, nb, Q_BLOCK, dh).transpose(2, 0, 1, 3, 4)
    cb = c.reshape(B, H, nb, Q_BLOCK).transpose(2, 0, 1, 3)
    key_pos = jnp.arange(L)

    def block(args):
        qi, ci, bi = args
        s = jnp.einsum('bhqd,bhkd->bhqk', qi, kh).astype(jnp.float32) * scale
        s = s + ci[..., None] - c[:, :, None, :]
        q_pos = bi * Q_BLOCK + jnp.arange(Q_BLOCK)
        s = jnp.where(key_pos[None, :] <= q_pos[:, None], s, -jnp.inf)
        p = jax.nn.softmax(s, axis=-1)
        return jnp.einsum('bhqk,bhkd->bhqd', p.astype(vh.dtype), vh)

    o = lax.map(block, (qb, cb, jnp.arange(nb)))
    return o.transpose(1, 0, 3, 2, 4).reshape(B, L, H * dh)


def complex_affine_combine(e1, e2):
    a1r, a1i, b1r, b1i = e1
    a2r, a2i, b2r, b2i = e2
    ar = a2r * a1r - a2i * a1i
    ai = a2r * a1i + a2i * a1r
    br = a2r * b1r - a2i * b1i + b2r
    bi = a2r * b1i + a2i * b1r + b2i
    return (ar, ai, br, bi)


def s5_layer(u, a_re, a_im, b_re, b_im, c_re, c_im, d, log_dt, w_glu):
    B, L, _ = u.shape
    uf = u.astype(jnp.float32)
    ug = uf.reshape(B, L, S5_GROUPS, S5_GROUP_CH)
    dt = jnp.exp(log_dt.astype(jnp.float32))[:, None]
    ar = a_re.astype(jnp.float32)
    ai = a_im.astype(jnp.float32)
    mag = jnp.exp(ar * dt)
    lr = mag * jnp.cos(ai * dt)
    li = mag * jnp.sin(ai * dt)
    den = ar * ar + ai * ai
    fr = ((lr - 1.0) * ar + li * ai) / den
    fi = (li * ar - (lr - 1.0) * ai) / den
    br = b_re.astype(jnp.float32)
    bi = b_im.astype(jnp.float32)
    bbr = fr[..., None] * br - fi[..., None] * bi
    bbi = fr[..., None] * bi + fi[..., None] * br
    bu_r = jnp.einsum('gph,blgh->blgp', bbr, ug)
    bu_i = jnp.einsum('gph,blgh->blgp', bbi, ug)
    lr_t = jnp.broadcast_to(lr, bu_r.shape)
    li_t = jnp.broadcast_to(li, bu_r.shape)
    _, _, xr, xi = lax.associative_scan(complex_affine_combine, (lr_t, li_t, bu_r, bu_i), axis=1)
    y = (jnp.einsum('ghp,blgp->blgh', c_re.astype(jnp.float32), xr)
         - jnp.einsum('ghp,blgp->blgh', c_im.astype(jnp.float32), xi))
    y = y.reshape(B, L, D_S5) + d.astype(jnp.float32) * uf
    y = jax.nn.gelu(y)
    y = y * jax.nn.sigmoid(y @ w_glu.astype(jnp.float32))
    return y.astype(u.dtype)


def retention(q, k, v, gn_w):
    B, L, H, dk = q.shape
    dv = v.shape[-1]
    nc = L // CHUNK
    scale = 1.0 / math.sqrt(dk)
    log_gamma = jnp.log1p(-(2.0 ** (-5.0 - jnp.arange(H, dtype=jnp.float32))))

    def chunks(t):
        return t.astype(jnp.float32).reshape(B, nc, CHUNK, H, t.shape[-1]).transpose(1, 0, 3, 2, 4)

    qc, kc, vc = chunks(q), chunks(k), chunks(v)
    pos = jnp.arange(CHUNK, dtype=jnp.float32)
    dmat = jnp.exp(log_gamma[:, None, None] * jnp.abs(pos[:, None] - pos[None, :]))
    scores = jnp.einsum('cbhnd,cbhmd->cbhnm', qc, kc) * scale * dmat
    inner = jnp.einsum('cbhnm,cbhme->cbhne', scores, vc)
    wk = jnp.exp(log_gamma[:, None] * (CHUNK - 1.0 - pos)[None, :])
    wq = jnp.exp(log_gamma[:, None] * (pos + 1.0)[None, :])
    upd = jnp.einsum('cbhmd,cbhme->cbhde', kc * wk[None, None, :, :, None], vc)
    g_chunk = jnp.exp(log_gamma * CHUNK)[:, None, None]

    def step(state, u_i):
        return g_chunk * state + u_i, state

    _, prev_states = lax.scan(step, jnp.zeros((B, H, dk, dv), jnp.float32), upd)
    cross = jnp.einsum('cbhnd,cbhde->cbhne', qc * wq[None, None, :, :, None], prev_states) * scale
    o = (inner + cross).transpose(1, 0, 3, 2, 4).reshape(B, L, H, dv)
    mu = jnp.mean(o, axis=-1, keepdims=True)
    var = jnp.mean(jnp.square(o - mu), axis=-1, keepdims=True)
    o = (o - mu) * lax.rsqrt(var + EPS)
    return (o.reshape(B, L, H * dv) * gn_w.astype(jnp.float32)).astype(v.dtype)


def setup_inputs(seed: int = 0) -> dict:
    key = jax.random.key(seed)
    ks = jax.random.split(key, 17)
    f32 = jnp.float32
    x = jax.random.normal(ks[0], (BATCH, SEQ, D_MODEL), f32)
    norm_w = 1.0 + 0.02 * jax.random.normal(ks[1], (DEPTH, D_MODEL), f32)
    w_in = jax.random.normal(ks[2], (DEPTH, D_MODEL, D_IN_PROJ), f32) * D_MODEL ** -0.5
    fox_b_f = 3.0 + 0.5 * jax.random.normal(ks[3], (DEPTH, FOX_HEADS), f32)
    s5_a_re = -0.5 + 0.01 * jax.random.normal(ks[4], (DEPTH, S5_GROUPS, S5_STATE), f32)
    s5_a_im = (math.pi * jnp.arange(S5_STATE, dtype=f32)[None, None, :]
               + 0.01 * jax.random.normal(ks[5], (DEPTH, S5_GROUPS, S5_STATE), f32))
    s5_b_re = jax.random.normal(ks[6], (DEPTH, S5_GROUPS, S5_STATE, S5_GROUP_CH), f32) * (2 * S5_GROUP_CH) ** -0.5
    s5_b_im = jax.random.normal(ks[7], (DEPTH, S5_GROUPS, S5_STATE, S5_GROUP_CH), f32) * (2 * S5_GROUP_CH) ** -0.5
    s5_c_re = jax.random.normal(ks[8], (DEPTH, S5_GROUPS, S5_GROUP_CH, S5_STATE), f32) * S5_STATE ** -0.5
    s5_c_im = jax.random.normal(ks[9], (DEPTH, S5_GROUPS, S5_GROUP_CH, S5_STATE), f32) * S5_STATE ** -0.5
    s5_d = jax.random.normal(ks[10], (DEPTH, D_S5), f32)
    s5_log_dt = jax.random.uniform(ks[11], (DEPTH, S5_GROUPS), f32, math.log(1e-3), math.log(1e-1))
    s5_w_glu = jax.random.normal(ks[12], (DEPTH, D_S5, D_S5), f32) * D_S5 ** -0.5
    ret_gn_w = 1.0 + 0.02 * jax.random.normal(ks[13], (DEPTH, D_RET), f32)
    w_out = jax.random.normal(ks[14], (DEPTH, D_MIX, D_MODEL), f32) * (0.5 * D_MIX ** -0.5)
    final_norm_w = 1.0 + 0.02 * jax.random.normal(ks[15], (D_MODEL,), f32)
    return {'x': x, 'norm_w': norm_w, 'w_in': w_in, 'fox_b_f': fox_b_f,
            's5_a_re': s5_a_re, 's5_a_im': s5_a_im, 's5_b_re': s5_b_re, 's5_b_im': s5_b_im,
            's5_c_re': s5_c_re, 's5_c_im': s5_c_im, 's5_d': s5_d, 's5_log_dt': s5_log_dt,
            's5_w_glu': s5_w_glu, 'ret_gn_w': ret_gn_w, 'w_out': w_out, 'final_norm_w': final_norm_w}


def reference(x, norm_w, w_in, fox_b_f, s5_a_re, s5_a_im, s5_b_re, s5_b_im,
              s5_c_re, s5_c_im, s5_d, s5_log_dt, s5_w_glu, ret_gn_w, w_out, final_norm_w):
    B, L, _ = x.shape
    for l in range(DEPTH):
        h = rms_norm(x, norm_w[l])
        proj = h @ w_in[l]
        fq, fk, fv, flog, su, rq, rk, rv, gate = split_cols(proj)
        logf = jax.nn.log_sigmoid(flog.astype(jnp.float32) + fox_b_f[l].astype(jnp.float32))
        y_fox = forgetting_attention(fq.reshape(B, L, FOX_HEADS, HEAD_DIM),
                                     fk.reshape(B, L, FOX_HEADS, HEAD_DIM),
                                     fv.reshape(B, L, FOX_HEADS, HEAD_DIM), logf)
        y_s5 = s5_layer(su, s5_a_re[l], s5_a_im[l], s5_b_re[l], s5_b_im[l],
                        s5_c_re[l], s5_c_im[l], s5_d[l], s5_log_dt[l], s5_w_glu[l])
        y_ret = retention(rotary(rq.reshape(B, L, RET_HEADS, HEAD_DIM)),
                          rotary(rk.reshape(B, L, RET_HEADS, HEAD_DIM)),
                          rv.reshape(B, L, RET_HEADS, HEAD_DIM), ret_gn_w[l])
        y = jnp.concatenate([y_fox.astype(x.dtype), y_s5.astype(x.dtype), y_ret.astype(x.dtype)], axis=-1)
        y = y * jax.nn.silu(gate)
        x = x + y @ w_out[l]
    return rms_norm(x, final_norm_w)
```

```python
import functools
import math

import jax
import jax.numpy as jnp
from jax import lax
from jax.experimental import pallas as pl
from jax.experimental.pallas import tpu as pltpu

F32 = jnp.float32
BF16 = jnp.bfloat16

D_MODEL = 1024
HEAD_DIM = 64
CHUNK = 64
D_FOX = 512
FOX_HEADS = 8
D_S5 = 256
S5_GROUPS = 16
S5_GROUP_CH = 16
S5_STATE = 64
S5_CH = S5_GROUPS * S5_STATE
D_RET = 256
RET_HEADS = 4
ROPE_BASE = 10000.0
EPS = 1e-6

LANES = 128
SUBLANES = 8
N_QKV = 3 * D_FOX
N_REST = D_S5 + 3 * D_RET + D_MODEL
VMEM_LIMIT = 56 * 1024 * 1024

TM_PROJ = 512
T_GATE = 512
T_FOX = 512
T_RET = 256
S5_SEG = 128
S5_BLK = SUBLANES * S5_SEG
NEG_BIG = -1e30


def _cparams(sem):
    return pltpu.CompilerParams(dimension_semantics=sem, vmem_limit_bytes=VMEM_LIMIT)


def _inproj_kernel(x_ref, g_ref, wa_ref, wb_ref, wc_ref, oa_ref, ob_ref, oc_ref):
    x = x_ref[...]
    ms = jnp.mean(x * x, axis=-1, keepdims=True)
    h = (x * lax.rsqrt(ms + EPS) * g_ref[...]).astype(BF16)
    oa_ref[...] = jnp.dot(h, wa_ref[...], preferred_element_type=F32).astype(oa_ref.dtype)
    ob_ref[...] = jnp.dot(h, wb_ref[...], preferred_element_type=F32)
    oc_ref[...] = jnp.dot(h, wc_ref[...], preferred_element_type=F32)


def _inproj(x2, g, wa, wb, wc):
    m = x2.shape[0]
    row = lambda i: (i, 0)
    fixed = lambda i: (0, 0)
    return pl.pallas_call(
        _inproj_kernel,
        grid=(m // TM_PROJ,),
        in_specs=[
            pl.BlockSpec((TM_PROJ, D_MODEL), row),
            pl.BlockSpec((1, D_MODEL), fixed),
            pl.BlockSpec((D_MODEL, N_QKV), fixed),
            pl.BlockSpec((D_MODEL, N_REST), fixed),
            pl.BlockSpec((D_MODEL, LANES), fixed),
        ],
        out_specs=[
            pl.BlockSpec((TM_PROJ, N_QKV), row),
            pl.BlockSpec((TM_PROJ, N_REST), row),
            pl.BlockSpec((TM_PROJ, LANES), row),
        ],
        out_shape=[
            jax.ShapeDtypeStruct((m, N_QKV), BF16),
            jax.ShapeDtypeStruct((m, N_REST), F32),
            jax.ShapeDtypeStruct((m, LANES), F32),
        ],
        compiler_params=_cparams(("parallel",)),
        name="inproj",
    )(x2, g, wa, wb, wc)


def _gate_kernel(fl_ref, bf_ref, c_ref, carry_ref):
    @pl.when(pl.program_id(1) == 0)
    def _():
        carry_ref[...] = jnp.zeros_like(carry_ref)

    z = fl_ref[0] + bf_ref[...]
    logf = jnp.minimum(z, 0.0) - jnp.log1p(jnp.exp(-jnp.abs(z)))
    r = lax.broadcasted_iota(jnp.int32, (T_GATE, T_GATE), 0)
    s = lax.broadcasted_iota(jnp.int32, (T_GATE, T_GATE), 1)
    tri = jnp.where(s <= r, 1.0, 0.0).astype(F32)
    cs = jnp.dot(tri, logf, precision=lax.Precision.HIGHEST,
                 preferred_element_type=F32) + carry_ref[...]
    carry_ref[...] = cs[T_GATE - 1:T_GATE, :]
    c_ref[0] = cs.T[:FOX_HEADS, :]


def _gate_cumsum(flog, bf_pad, batch, seq):
    return pl.pallas_call(
        _gate_kernel,
        grid=(batch, seq // T_GATE),
        in_specs=[
            pl.BlockSpec((1, T_GATE, LANES), lambda b, j: (b, j, 0)),
            pl.BlockSpec((1, LANES), lambda b, j: (0, 0)),
        ],
        out_specs=pl.BlockSpec((1, FOX_HEADS, T_GATE), lambda b, j: (b, 0, j)),
        out_shape=jax.ShapeDtypeStruct((batch, FOX_HEADS, seq), F32),
        scratch_shapes=[pltpu.VMEM((1, LANES), F32)],
        compiler_params=_cparams(("arbitrary", "arbitrary")),
        name="fox_gate_cumsum",
    )(flog, bf_pad)


def _fox_kernel(q_ref, k_ref, v_ref, c_ref, o_ref):
    t = T_FOX
    hp = pl.program_id(1)
    i = pl.program_id(2)
    q_start = pl.multiple_of(i * t, t)
    lane = lax.broadcasted_iota(jnp.int32, (t, LANES), 1)
    first = lane < HEAD_DIM
    q = q_ref[0] * (1.0 / math.sqrt(HEAD_DIM))
    zero = jnp.zeros_like(q)
    qh = (jnp.where(first, q, zero), jnp.where(first, zero, q))
    cq0 =tuple(c_ref[0, pl.ds(2 * hp + e, 1), pl.ds(q_start, t)][:, 0:1] for e in range(2))

    def tile(j, carry, masked):
        m, l, acc = carry
        k_start = pl.multiple_of(j * t, t)
        kt = k_ref[0, pl.ds(k_start, t), :]
        vt = v_ref[0, pl.ds(k_start, t), :]
        new_m, new_l, alphas, pvs = [], [], [], []
        for e in range(2):
            bias = cq0[e] - c_ref[0, pl.ds(2 * hp + e, 1), pl.ds(k_start, t)]
            s = lax.dot_general(qh[e], kt, (((1,), (1,)), ((), ())),
                                preferred_element_type=F32) + bias
            if masked:
                r = lax.broadcasted_iota(jnp.int32, (t, t), 0)
                cidx = lax.broadcasted_iota(jnp.int32, (t, t), 1)
                s = jnp.where(cidx <= r, s, NEG_BIG)
            m_new = jnp.maximum(m[e], jnp.max(s, axis=-1, keepdims=True))
            alpha = jnp.exp(m[e] - m_new)
            p = jnp.exp(s - m_new)
            new_m.append(m_new)
            new_l.append(alpha * l[e] + jnp.sum(p, axis=-1, keepdims=True))
            alphas.append(alpha)
            pvs.append(jnp.dot(p.astype(BF16), vt, preferred_element_type=F32))
        acc = acc * jnp.where(first, alphas[0], alphas[1]) + jnp.where(first, pvs[0], pvs[1])
        return tuple(new_m), tuple(new_l), acc

    init =((jnp.full((t, 1), NEG_BIG, F32),) * 2, (jnp.zeros((t, 1), F32),) * 2,
            jnp.zeros((t, LANES), F32))
    carry = lax.fori_loop(0, i, lambda j, c: tile(j, c, False), init)
    m, l, acc = tile(i, carry, True)
    o_ref[0] = acc * jnp.where(first, 1.0 / l[0], 1.0 / l[1])


def _fox_attention(qkv, c):
    batch, seq, _ = qkv.shape
    n_pairs = D_FOX // LANES
    return pl.pallas_call(
        _fox_kernel,
        grid=(batch, n_pairs, seq // T_FOX),
        in_specs=[
            pl.BlockSpec((1, T_FOX, LANES), lambda b, h, i: (b, i, h)),
            pl.BlockSpec((1, seq, LANES), lambda b, h, i: (b, 0, n_pairs + h)),
            pl.BlockSpec((1, seq, LANES), lambda b, h, i: (b, 0, 2 * n_pairs + h)),
            pl.BlockSpec((1, FOX_HEADS, seq), lambda b, h, i: (b, 0, 0)),
        ],
        out_specs=pl.BlockSpec((1, T_FOX, LANES), lambda b, h, i: (b, i, h)),
        out_shape=jax.ShapeDtypeStruct((batch, seq, D_FOX), F32),
        compiler_params=_cparams(("parallel", "parallel", "parallel")),
        name="fox_attention",
    )(qkv, qkv, qkv, c)


def _ret_kernel(q_ref, k_ref, v_ref, cos_ref, sin_ref, lg_ref, gn_ref, o_ref, state_ref):
    t = T_RET

    @pl.when(pl.program_id(2) == 0)
    def _():
        state_ref[...] = jnp.zeros_like(state_ref)

    lane = lax.broadcasted_iota(jnp.int32, (t, LANES), 1)
    first = lane < HEAD_DIM
    low_half = (lane % HEAD_DIM) < (HEAD_DIM // 2)
    cos = cos_ref[...]
    sin = sin_ref[...]

    def rotary(x):
        swapped = jnp.where(low_half, pltpu.roll(x, LANES - HEAD_DIM // 2, 1),
                            pltpu.roll(x, HEAD_DIM // 2, 1))
        return x * cos + swapped * sin

    q = rotary(q_ref[0])
    k = rotary(k_ref[0])
    v = v_ref[0]
    vb = v.astype(BF16)
    kb = k.astype(BF16)
    scale = 1.0 / math.sqrt(HEAD_DIM)
    lg_lane = lg_ref[...]
    pos = lax.broadcasted_iota(jnp.int32, (t, 1), 0).astype(F32)
    r = lax.broadcasted_iota(jnp.int32, (t, t), 0)
    s = lax.broadcasted_iota(jnp.int32, (t, t), 1)
    dist = jnp.abs(r - s).astype(F32)
    visible = (s // CHUNK) <= (r // CHUNK)
    zero = jnp.zeros_like(q)
    inner = []
    for e in range(2):
        lg = lg_lane[:, e * HEAD_DIM:e * HEAD_DIM + 1]
        dmat = jnp.where(visible, jnp.exp(lg * dist), 0.0)
        qe = jnp.where(first, q, zero) if e == 0 else jnp.where(first, zero, q)
        sc = lax.dot_general(qe.astype(BF16), kb, (((1,), (1,)), ((), ())),
                             preferred_element_type=F32) * scale * dmat
        inner.append(jnp.dot(sc.astype(BF16), vb, preferred_element_type=F32))
    o = jnp.where(first, inner[0], inner[1])

    rb = lax.broadcasted_iota(jnp.int32, (LANES, LANES), 0)
    cb = lax.broadcasted_iota(jnp.int32, (LANES, LANES), 1)
    same_head = (rb // HEAD_DIM) == (cb // HEAD_DIM)
    state = state_ref[...]
    wq = jnp.exp(lg_lane * (pos + 1.0))
    o = o + jnp.dot((q * wq).astype(BF16), state.astype(BF16),
                    preferred_element_type=F32) * scale
    wk = jnp.exp(lg_lane * (t - 1.0 - pos))
    upd = lax.dot_general((k * wk).astype(BF16), vb, (((0,), (0,)), ((), ())),
                          preferred_element_type=F32)
    state_ref[...] = jnp.where(same_head, state * jnp.exp(lg_lane * float(t)) + upd, 0.0)

    def head_mean(x):
        s0 = jnp.sum(jnp.where(first, x, 0.0), axis=-1, keepdims=True)
        s1 = jnp.sum(jnp.where(first, 0.0, x), axis=-1, keepdims=True)
        return jnp.where(first, s0, s1) * (1.0 / HEAD_DIM)

    d = o - head_mean(o)
    var = head_mean(d * d)
    o_ref[0] = d * lax.rsqrt(var + EPS) * gn_ref[...]


def _retention(rest, cos_t, sin_t, lg, gn_w):
    batch, seq, _ = rest.shape
    n_pairs = D_RET // LANES
    base = D_S5 // LANES
    tok = lambda off: pl.BlockSpec((1, T_RET, LANES), lambda b, h, i, off=off: (b, i, off + h))
    return pl.pallas_call(
        _ret_kernel,
        grid=(batch, n_pairs, seq // T_RET),
        in_specs=[
            tok(base), tok(base + n_pairs), tok(base + 2 * n_pairs),
            pl.BlockSpec((T_RET, LANES), lambda b, h, i: (i, 0)),
            pl.BlockSpec((T_RET, LANES), lambda b, h, i: (i, 0)),
            pl.BlockSpec((1, LANES), lambda b, h, i: (0, h)),
            pl.BlockSpec((1, LANES), lambda b, h, i: (0, h)),
        ],
        out_specs=pl.BlockSpec((1, T_RET, LANES), lambda b, h, i: (b, i, h)),
        out_shape=jax.ShapeDtypeStruct((batch, seq, D_RET), F32),
        scratch_shapes=[pltpu.VMEM((LANES, LANES), F32)],
        compiler_params=_cparams(("arbitrary", "arbitrary", "arbitrary")),
        name="retention",
    )(rest, rest, rest, cos_t, sin_t, lg, gn_w)


def _s5_kernel(ua_ref, ub_ref, are_ref, aim_ref, ldt_ref, br_ref, bi_ref, cr_ref, ci_ref, d_ref,
               wg_ref, oa_ref, ob_ref, wb_ref, wc_ref, lam_ref, pow_ref, up_ref, x_ref, y_ref,
               state_ref):
    n = S5_CH

    @pl.when((pl.program_id(0) == 0) & (pl.program_id(1) == 0))
    def _():
        dt = jnp.exp(ldt_ref[...])
        ar = are_ref[...]
        ai = aim_ref[...]
        mag = jnp.exp(ar * dt)
        lr = mag * jnp.cos(ai * dt)
        li = mag * jnp.sin(ai * dt)
        den = ar * ar + ai * ai
        fr = ((lr - 1.0) * ar + li * ai) / den
        fi = (li * ar - (lr - 1.0) * ai) / den
        b_r = br_ref[...]
        b_i = bi_ref[...]
        wb_ref[:, :n] = (fr * b_r - fi * b_i).astype(BF16)
        wb_ref[:, n:] = (fr * b_i + fi * b_r).astype(BF16)
        wc_ref[:n, :] = cr_ref[...].astype(BF16)
        wc_ref[n:, :] = (-ci_ref[...]).astype(BF16)
        lam_ref[0:1, :] = lr
        lam_ref[1:2, :] = li

        def pw(p, carry):
            pr, pi = carry
            pow_ref[pl.ds(p, 1), :n] = pr
            pow_ref[pl.ds(p, 1), n:] = pi
            return pr * lr - pi * li, pr * li + pi * lr

        lax.fori_loop(0, S5_SEG, pw, (lr, li))

    @pl.when(pl.program_id(1) == 0)
    def _():
        state_ref[...] = jnp.zeros_like(state_ref)

    def permute_in(p, _):
        rows = pl.ds(pl.multiple_of(p * SUBLANES, SUBLANES), SUBLANES)
        up_ref[rows, :LANES] = ua_ref[0, pl.ds(p, SUBLANES, stride=S5_SEG), :]
        up_ref[rows, LANES:] = ub_ref[0, pl.ds(p, SUBLANES, stride=S5_SEG), :]
        return 0

    lax.fori_loop(0, S5_SEG, permute_in, 0)
    u = up_ref[...]
    x_ref[...] = jnp.dot(u.astype(BF16), wb_ref[...], preferred_element_type=F32)

    lr8 = jnp.broadcast_to(lam_ref[0:1, :], (SUBLANES, n))
    li8 = jnp.broadcast_to(lam_ref[1:2, :], (SUBLANES, n))

    def scan(p, carry):
        xr, xi = carry
        rows = pl.ds(pl.multiple_of(p * SUBLANES, SUBLANES), SUBLANES)
        nr = lr8 * xr - li8 * xi + x_ref[rows, :n]
        ni = lr8 * xi + li8 * xr + x_ref[rows, n:]
        x_ref[rows, :n] = nr
        x_ref[rows, n:] = ni
        return nr, ni

    zeros8 = jnp.zeros((SUBLANES, n), F32)
    end_r, end_i = lax.fori_loop(0, S5_SEG, scan, (zeros8, zeros8))

    gr = pow_ref[S5_SEG - 1:S5_SEG, :n]
    gi = pow_ref[S5_SEG - 1:S5_SEG, n:]
    cr = state_ref[0:1, :]
    ci = state_ref[1:2, :]
    ins_r, ins_i = [], []
    for j in range(SUBLANES):
        ins_r.append(cr)
        ins_i.append(ci)
        cr, ci = (gr * cr - gi * ci + end_r[j:j + 1, :], gr * ci + gi * cr + end_i[j:j + 1, :])
    state_ref[0:1, :] = cr
    state_ref[1:2, :] = ci
    in_r = jnp.concatenate(ins_r, axis=0)
    in_i = jnp.concatenate(ins_i, axis=0)

    def fixup(p, _):
        rows = pl.ds(pl.multiple_of(p * SUBLANES, SUBLANES), SUBLANES)
        pr = pow_ref[pl.ds(p, 1), :n]
        pi = pow_ref[pl.ds(p, 1), n:]
        x_ref[rows, :n] = x_ref[rows, :n] + (pr * in_r - pi * in_i)
        x_ref[rows, n:] = x_ref[rows, n:] + (pr * in_i + pi * in_r)
        return 0

    lax.fori_loop(0, S5_SEG, fixup, 0)

    y = jnp.dot(x_ref[...].astype(BF16), wc_ref[...], preferred_element_type=F32)
    y = jax.nn.gelu(y + d_ref[...] * u)
    z = jnp.dot(y.astype(BF16), wg_ref[...], preferred_element_type=F32)
    y_ref[...] = y * jax.nn.sigmoid(z)

    def permute_out(p, _):
        rows = pl.ds(pl.multiple_of(p * SUBLANES, SUBLANES), SUBLANES)
        oa_ref[0, pl.ds(p, SUBLANES, stride=S5_SEG), :] = y_ref[rows, :LANES]
        ob_ref[0, pl.ds(p, SUBLANES, stride=S5_SEG), :] = y_ref[rows, LANES:]
        return 0

    lax.fori_loop(0, S5_SEG, permute_out, 0)


def _s5(rest, prm):
    batch, seq, _ = rest.shape
    fixed = lambda b, j: (0, 0)
    row = pl.BlockSpec((1, S5_CH), fixed)
    half = lambda col: pl.BlockSpec((1, S5_BLK, LANES), lambda b, j, col=col: (b, j, col))
    out_half = jax.ShapeDtypeStruct((batch, seq, LANES), F32)
    return pl.pallas_call(
        _s5_kernel,
        grid=(batch, seq // S5_BLK),
        in_specs=[
            half(0), half(1),
            row, row, row,
            pl.BlockSpec((D_S5, S5_CH), fixed), pl.BlockSpec((D_S5, S5_CH), fixed),
            pl.BlockSpec((S5_CH, D_S5), fixed), pl.BlockSpec((S5_CH, D_S5), fixed),
            pl.BlockSpec((1, D_S5), fixed),
            pl.BlockSpec((D_S5, D_S5), fixed),
        ],
        out_specs=[half(0), half(0)],
        out_shape=[out_half, out_half],
        scratch_shapes=[
            pltpu.VMEM((D_S5, 2 * S5_CH), BF16),
            pltpu.VMEM((2 * S5_CH, D_S5), BF16),
            pltpu.VMEM((2, S5_CH), F32),
            pltpu.VMEM((S5_SEG, 2 * S5_CH), F32),
            pltpu.VMEM((S5_BLK, D_S5), F32),
            pltpu.VMEM((S5_BLK, 2 * S5_CH), F32),
            pltpu.VMEM((S5_BLK, D_S5), F32),
            pltpu.VMEM((2, S5_CH), F32),
        ],
        compiler_params=_cparams(("arbitrary", "arbitrary")),
        name="s5",
    )(rest, rest, prm["a_re"], prm["a_im"], prm["log_dt"], prm["b_re"], prm["b_im"],
      prm["c_re"], prm["c_im"], prm["d"], prm["w_glu"])


def _outproj_kernel(yf_ref, ysa_ref, ysb_ref, yr_ref, gate_ref, x_ref, w_ref, fn_ref, o_ref, *,
                    final):
    g = gate_ref[...]
    g = g * jax.nn.sigmoid(g)
    acc = x_ref[...]
    off = 0
    for y_ref in (yf_ref, ysa_ref, ysb_ref, yr_ref):
        width = y_ref.shape[-1]
        y = (y_ref[...] * g[:, off:off + width]).astype(BF16)
        acc = acc + jnp.dot(y, w_ref[off:off + width, :], preferred_element_type=F32)
        off += width
    if final:
        ms = jnp.mean(acc * acc, axis=-1, keepdims=True)
        acc = acc * lax.rsqrt(ms + EPS) * fn_ref[...]
    o_ref[...] = acc


def _outproj(y_fox, y_s5a, y_s5b, y_ret, rest, x2, w_out, fn_w, final):
    m = x2.shape[0]
    row = lambda i: (i, 0)
    fixed = lambda i: (0, 0)
    return pl.pallas_call(
        functools.partial(_outproj_kernel, final=final),
        grid=(m // TM_PROJ,),
        in_specs=[
            pl.BlockSpec((TM_PROJ, D_FOX), row),
            pl.BlockSpec((TM_PROJ, LANES), row),
            pl.BlockSpec((TM_PROJ, LANES), row),
            pl.BlockSpec((TM_PROJ, D_RET), row),
            pl.BlockSpec((TM_PROJ, D_MODEL), lambda i: (i, 1)),
            pl.BlockSpec((TM_PROJ, D_MODEL), row),
            pl.BlockSpec((D_MODEL, D_MODEL), fixed),
            pl.BlockSpec((1, D_MODEL), fixed),
        ],
        out_specs=pl.BlockSpec((TM_PROJ, D_MODEL), row),
        out_shape=jax.ShapeDtypeStruct((m, D_MODEL), F32),
        compiler_params=_cparams(("parallel",)),
        name="outproj_final" if final else "outproj",
    )(y_fox, y_s5a, y_s5b, y_ret, rest, x2, w_out, fn_w)


def _rotary_tables(seq):
    half = HEAD_DIM // 2
    freqs = ROPE_BASE ** (-jnp.arange(half, dtype=F32) / half)
    ang = jnp.arange(seq, dtype=F32)[:, None] * freqs[None, :]
    cos = jnp.cos(ang)
    sin = jnp.sin(ang)
    cos_t = jnp.tile(cos, (1, LANES // half))
    sin_t = jnp.tile(jnp.concatenate([-sin, sin], axis=-1), (1, LANES // HEAD_DIM))
    return cos_t, sin_t


def _s5_params(a_re, a_im, b_re, b_im, c_re, c_im, d, log_dt, w_glu):
    eye = jnp.eye(S5_GROUPS, dtype=F32)
    bd_b = lambda b: jnp.einsum("gph,gk->ghkp", b, eye).reshape(D_S5, S5_CH)
    bd_c = lambda c: jnp.einsum("ghp,gk->gpkh", c, eye).reshape(S5_CH, D_S5)
    return {
        "a_re": a_re.reshape(1, S5_CH), "a_im": a_im.reshape(1, S5_CH),
        "log_dt": jnp.repeat(log_dt, S5_STATE).reshape(1, S5_CH),
        "b_re": bd_b(b_re), "b_im": bd_b(b_im), "c_re": bd_c(c_re), "c_im": bd_c(c_im),
        "d": d.reshape(1, D_S5), "w_glu": w_glu.astype(BF16),
    }


def kernel(x, norm_w, w_in, fox_b_f, s5_a_re, s5_a_im, s5_b_re, s5_b_im, s5_c_re, s5_c_im,
           s5_d, s5_log_dt, s5_w_glu, ret_gn_w, w_out, final_norm_w):
    batch, seq, _ = x.shape
    depth = w_in.shape[0]
    m = batch * seq
    cos_t, sin_t = _rotary_tables(seq)
    log_gamma = jnp.log1p(-(2.0 ** (-5.0 - jnp.arange(RET_HEADS, dtype=F32))))
    lg = jnp.repeat(log_gamma, HEAD_DIM).reshape(1, D_RET)
    fn_w = final_norm_w.reshape(1, D_MODEL)
    flog_lo = N_QKV
    flog_hi = N_QKV + FOX_HEADS

    x2 = x.reshape(m, D_MODEL)
    for l in range(depth):
        wa = w_in[l, :, :flog_lo].astype(BF16)
        wc = jnp.pad(w_in[l, :, flog_lo:flog_hi], ((0, 0), (0, LANES - FOX_HEADS))).astype(BF16)
        wb = w_in[l, :, flog_hi:].astype(BF16)
        bf_pad = jnp.pad(fox_b_f[l], (0, LANES - FOX_HEADS)).reshape(1, LANES)

        qkv, rest, flog = _inproj(x2, norm_w[l].reshape(1, D_MODEL), wa, wb, wc)
        qkv = qkv.reshape(batch, seq, N_QKV)
        rest3 = rest.reshape(batch, seq, N_REST)
        c = _gate_cumsum(flog.reshape(batch, seq, LANES), bf_pad, batch, seq)
        y_fox = _fox_attention(qkv, c)
        y_s5a, y_s5b = _s5(rest3, _s5_params(s5_a_re[l], s5_a_im[l], s5_b_re[l], s5_b_im[l],
                                     s5_c_re[l], s5_c_im[l], s5_d[l], s5_log_dt[l], s5_w_glu[l]))
        y_ret = _retention(rest3, cos_t, sin_t, lg, ret_gn_w[l].reshape(1, D_RET))
        x2 = _outproj(y_fox.reshape(m, D_FOX), y_s5a.reshape(m, LANES), y_s5b.reshape(m, LANES),
                      y_ret.reshape(m, D_RET), rest, x2, w_out[l].astype(BF16), fn_w,
                      final=(l == depth - 1))
    return x2.reshape(batch, seq, D_MODEL)
```

```python
import functools
import math

import jax
import jax.numpy as jnp
from jax import lax
from jax.experimental import pallas as pl
from jax.experimental.pallas import tpu as pltpu

F32 = jnp.float32
BF16 = jnp.bfloat16

D_MODEL = 1024
HEAD_DIM = 64
CHUNK = 64
D_FOX = 512
FOX_HEADS = 8
D_S5 = 256
S5_GROUPS = 16
S5_GROUP_CH = 16
S5_STATE = 64
S5_CH = S5_GROUPS * S5_STATE
D_RET = 256
RET_HEADS = 4
ROPE_BASE = 10000.0
EPS = 1e-6

LANES = 128
SUBLANES = 8
N_QKV = 3 * D_FOX
N_REST = D_S5 + 3 * D_RET + D_MODEL
VMEM_LIMIT = 56 * 1024 * 1024

TM_PROJ = 512
T_GATE = 512
T_FOX = 512
T_RET = 256
S5_SEG = 128
S5_BLK = SUBLANES * S5_SEG
NEG_BIG = -1e30
LOG2E = math.log2(math.e)
Q_SCALE = LOG2E / math.sqrt(HEAD_DIM)


def _cparams(sem):
    return pltpu.CompilerParams(dimension_semantics=sem, vmem_limit_bytes=VMEM_LIMIT)


def _inproj_kernel(x_ref, g_ref, wa_ref, wb_ref, wc_ref, oa_ref, ob_ref, oc_ref):
    x = x_ref[...]
    ms = jnp.mean(x * x, axis=-1, keepdims=True)
    h = (x * lax.rsqrt(ms + EPS) * g_ref[...]).astype(BF16)
    qkv = jnp.dot(h, wa_ref[...], preferred_element_type=F32)
    oa_ref[:, :D_FOX] = (qkv[:, :D_FOX] * Q_SCALE).astype(oa_ref.dtype)
    oa_ref[:, D_FOX:] = qkv[:, D_FOX:].astype(oa_ref.dtype)
    ob_ref[...] = jnp.dot(h, wb_ref[...], preferred_element_type=F32)
    oc_ref[...] = jnp.dot(h, wc_ref[...], preferred_element_type=F32)


def _inproj(x2, g, wa, wb, wc):
    m = x2.shape[0]
    row = lambda i: (i, 0)
    fixed = lambda i: (0, 0)
    return pl.pallas_call(
        _inproj_kernel,
        grid=(m // TM_PROJ,),
        in_specs=[
            pl.BlockSpec((TM_PROJ, D_MODEL), row),
            pl.BlockSpec((1, D_MODEL), fixed),
            pl.BlockSpec((D_MODEL, N_QKV), fixed),
            pl.BlockSpec((D_MODEL, N_REST), fixed),
            pl.BlockSpec((D_MODEL, LANES), fixed),
        ],
        out_specs=[
            pl.BlockSpec((TM_PROJ, N_QKV), row),
            pl.BlockSpec((TM_PROJ, N_REST), row),
            pl.BlockSpec((TM_PROJ, LANES), row),
        ],
        out_shape=[
            jax.ShapeDtypeStruct((m, N_QKV), BF16),
            jax.ShapeDtypeStruct((m, N_REST), F32),
            jax.ShapeDtypeStruct((m, LANES), F32),
        ],
        compiler_params=_cparams(("parallel",)),
        name="inproj",
    )(x2, g, wa, wb, wc)


def _gate_kernel(fl_ref, bf_ref, c_ref, carry_ref):
    @pl.when(pl.program_id(1) == 0)
    def _():
        carry_ref[...] = jnp.zeros_like(carry_ref)

    z = fl_ref[0] + bf_ref[...]
    logf = jnp.minimum(z, 0.0) - jnp.log1p(jnp.exp(-jnp.abs(z)))
    r = lax.broadcasted_iota(jnp.int32, (T_GATE, T_GATE), 0)
    s = lax.broadcasted_iota(jnp.int32, (T_GATE, T_GATE), 1)
    tri = jnp.where(s <= r, 1.0, 0.0).astype(F32)
    cs = jnp.dot(tri, logf, precision=lax.Precision.HIGHEST,
                 preferred_element_type=F32) + carry_ref[...]
    carry_ref[...] = cs[T_GATE - 1:T_GATE, :]
    c_ref[0] = cs


def _gate_cumsum(flog, bf_pad, batch, seq):
    return pl.pallas_call(
        _gate_kernel,
        grid=(batch, seq // T_GATE),
        in_specs=[
            pl.BlockSpec((1, T_GATE, LANES), lambda b, j: (b, j, 0)),
            pl.BlockSpec((1, LANES), lambda b, j: (0, 0)),
        ],
        out_specs=pl.BlockSpec((1, T_GATE, LANES), lambda b, j: (b, j, 0)),
        out_shape=jax.ShapeDtypeStruct((batch, seq, LANES), F32),
        scratch_shapes=[pltpu.VMEM((1, LANES), F32)],
        compiler_params=_cparams(("arbitrary", "arbitrary")),
        name="fox_gate_cumsum",
    )(flog, bf_pad)


def _fox_kernel(q_ref, k_ref, v_ref, c_ref, o_ref, kaug_ref, vaug_ref):
    t = T_FOX
    hp = pl.program_id(1)
    i = pl.program_id(2)
    seq = k_ref.shape[1]
    lane = lax.broadcasted_iota(jnp.int32, (t, LANES), 1)
    own = (lane < HEAD_DIM, lane >= HEAD_DIM)
    extra0 = (HEAD_DIM, 0)

    def head_column(c_tile, e):
        return jnp.sum(jnp.where(lane[:c_tile.shape[0]] == 2 * hp + e, c_tile, 0.0),
                       axis=-1, keepdims=True)

    @pl.when(i == 0)
    def _():
        def build(j, _):
            rows = pl.ds(pl.multiple_of(j * t, t), t)
            k = k_ref[0, rows, :]
            v = v_ref[0, rows, :]
            c_tile = c_ref[0, rows, :]
            for e in range(2):
                b = -LOG2E * head_column(c_tile, e)
                hi = b.astype(BF16).astype(F32)
                mid = (b - hi).astype(BF16).astype(F32)
                lo = b - hi - mid
                x0 = extra0[e]
                extra = jnp.where(lane == x0, hi, jnp.where(lane == x0 + 1, mid, jnp.where(
                    lane == x0 + 2, lo, jnp.where(lane == x0 + 3, 1.0, 0.0))))
                ka = jnp.where(own[e], k.astype(F32), extra)
                kaug_ref[e, :, rows] = ka.T.astype(BF16)
                vaug_ref[e, rows, :] = jnp.where(own[e], v, jnp.ones_like(v))
            return 0

        lax.fori_loop(0, seq // t, build, 0)

    q = q_ref[0]
    c_row0 = c_ref[0, pl.ds(pl.multiple_of(i * t, t), SUBLANES), :][0:1, :]
    qa = []
    for e in range(2):
        cq0 = LOG2E * head_column(c_row0, e)
        x0 = extra0[e]
        extra = jnp.where(lane == x0 + 3, cq0,
                          jnp.where((lane >= x0) & (lane < x0 + 3), 1.0, 0.0))
        qa.append(jnp.where(own[e], q, extra.astype(BF16)))

    def tile(j, carry, masked):
        m, acc = carry
        rows = pl.ds(pl.multiple_of(j * t, t), t)
        new_m, new_acc = [], []
        for e in range(2):
            s = jnp.dot(qa[e], kaug_ref[e, :, rows], preferred_element_type=F32)
            if masked:
                r = lax.broadcasted_iota(jnp.int32, (t, t), 0)
                cidx = lax.broadcasted_iota(jnp.int32, (t, t), 1)
                s = jnp.where(cidx <= r, s, NEG_BIG)
            m_new = jnp.maximum(m[e], jnp.max(s, axis=-1, keepdims=True))
            alpha = jnp.exp2(m[e] - m_new)
            p = jnp.exp2(s - m_new).astype(BF16)
            new_m.append(m_new)
            new_acc.append(acc[e] * alpha + jnp.dot(p, vaug_ref[e, rows, :],
                                                    preferred_element_type=F32))
        return tuple(new_m), tuple(new_acc)

    init = ((jnp.full((t, 1), NEG_BIG, F32),) * 2, (jnp.zeros((t, LANES), F32),) * 2)
    carry = lax.fori_loop(0, i, lambda j, c: tile(j, c, False), init)
    _, acc = tile(i, carry, True)
    outs = [acc[e] / pltpu.roll(acc[e], HEAD_DIM, 1) for e in range(2)]
    o_ref[0] = jnp.where(own[0], outs[0], outs[1])


def _fox_attention(qkv, c):
    batch, seq, _ = qkv.shape
    n_pairs = D_FOX // LANES
    return pl.pallas_call(
        _fox_kernel,
        grid=(batch, n_pairs, seq // T_FOX),
        in_specs=[
            pl.BlockSpec((1, T_FOX, LANES), lambda b, h, i: (b, i, h)),
            pl.BlockSpec((1, seq, LANES), lambda b, h, i: (b, 0, n_pairs + h)),
            pl.BlockSpec((1, seq, LANES), lambda b, h, i: (b, 0, 2 * n_pairs + h)),
            pl.BlockSpec((1, seq, LANES), lambda b, h, i: (b, 0, 0)),
        ],
        out_specs=pl.BlockSpec((1, T_FOX, LANES), lambda b, h, i: (b, i, h)),
        out_shape=jax.ShapeDtypeStruct((batch, seq, D_FOX), F32),
        scratch_shapes=[pltpu.VMEM((2, LANES, seq), BF16), pltpu.VMEM((2, seq, LANES), BF16)],
        compiler_params=_cparams(("parallel", "parallel", "arbitrary")),
        name="fox_attention",
    )(qkv, qkv, qkv, c)


def _ret_kernel(q_ref, k_ref, v_ref, cos_ref, sin_ref, lg_ref, gn_ref, o_ref, state_ref):
    t = T_RET

    @pl.when(pl.program_id(2) == 0)
    def _():
        state_ref[...] = jnp.zeros_like(state_ref)

    lane = lax.broadcasted_iota(jnp.int32, (t, LANES), 1)
    first = lane < HEAD_DIM
    low_half = (lane % HEAD_DIM) < (HEAD_DIM // 2)
    cos = cos_ref[...]
    sin = sin_ref[...]

    def rotary(x):
        swapped = jnp.where(low_half, pltpu.roll(x, LANES - HEAD_DIM // 2, 1),
                            pltpu.roll(x, HEAD_DIM // 2, 1))
        return x * cos + swapped * sin

    q = rotary(q_ref[0])
    k = rotary(k_ref[0])
    v = v_ref[0]
    vb = v.astype(BF16)
    kb = k.astype(BF16)
    scale = 1.0 / math.sqrt(HEAD_DIM)
    lg_lane = lg_ref[...]
    pos = lax.broadcasted_iota(jnp.int32, (t, 1), 0).astype(F32)
    r = lax.broadcasted_iota(jnp.int32, (t, t), 0)
    s = lax.broadcasted_iota(jnp.int32, (t, t), 1)
    dist = jnp.abs(r - s).astype(F32)
    visible = (s // CHUNK) <= (r // CHUNK)
    zero = jnp.zeros_like(q)
    inner = []
    for e in range(2):
        lg = lg_lane[:, e * HEAD_DIM:e * HEAD_DIM + 1]
        dmat = jnp.where(visible, jnp.exp(lg * dist), 0.0)
        qe = jnp.where(first, q, zero) if e == 0 else jnp.where(first, zero, q)
        sc = lax.dot_general(qe.astype(BF16), kb, (((1,), (1,)), ((), ())),
                             preferred_element_type=F32) * scale * dmat
        inner.append(jnp.dot(sc.astype(BF16), vb, preferred_element_type=F32))
    o = jnp.where(first, inner[0], inner[1])

    rb = lax.broadcasted_iota(jnp.int32, (LANES, LANES), 0)
    cb = lax.broadcasted_iota(jnp.int32, (LANES, LANES), 1)
    same_head = (rb // HEAD_DIM) == (cb // HEAD_DIM)
    state = state_ref[...]
    wq = jnp.exp(lg_lane * (pos + 1.0))
    o = o + jnp.dot((q * wq).astype(BF16), state.astype(BF16),
                    preferred_element_type=F32) * scale
    wk = jnp.exp(lg_lane * (t - 1.0 - pos))
    upd = lax.dot_general((k * wk).astype(BF16), vb, (((0,), (0,)), ((), ())),
                          preferred_element_type=F32)
    state_ref[...] = jnp.where(same_head, state * jnp.exp(lg_lane * float(t)) + upd, 0.0)

    def head_mean(x):
        s0 = jnp.sum(jnp.where(first, x, 0.0), axis=-1, keepdims=True)
        s1 = jnp.sum(jnp.where(first, 0.0, x), axis=-1, keepdims=True)
        return jnp.where(first, s0, s1) * (1.0 / HEAD_DIM)

    d = o - head_mean(o)
    var = head_mean(d * d)
    o_ref[0] = d * lax.rsqrt(var + EPS) * gn_ref[...]


def _retention(rest, cos_t, sin_t, lg, gn_w):
    batch, seq, _ = rest.shape
    n_pairs = D_RET // LANES
    base = D_S5 // LANES
    tok = lambda off: pl.BlockSpec((1, T_RET, LANES), lambda b, h, i, off=off: (b, i, off + h))
    return pl.pallas_call(
        _ret_kernel,
        grid=(batch, n_pairs, seq // T_RET),
        in_specs=[
            tok(base), tok(base + n_pairs), tok(base + 2 * n_pairs),
            pl.BlockSpec((T_RET, LANES), lambda b, h, i: (i, 0)),
            pl.BlockSpec((T_RET, LANES), lambda b, h, i: (i, 0)),
            pl.BlockSpec((1, LANES), lambda b, h, i: (0, h)),
            pl.BlockSpec((1, LANES), lambda b, h, i: (0, h)),
        ],
        out_specs=pl.BlockSpec((1, T_RET, LANES), lambda b, h, i: (b, i, h)),
        out_shape=jax.ShapeDtypeStruct((batch, seq, D_RET), F32),
        scratch_shapes=[pltpu.VMEM((LANES, LANES), F32)],
        compiler_params=_cparams(("arbitrary", "arbitrary", "arbitrary")),
        name="retention",
    )(rest, rest, rest, cos_t, sin_t, lg, gn_w)


def _s5_kernel(ua_ref, ub_ref, are_ref, aim_ref, ldt_ref, br_ref, bi_ref, cr_ref, ci_ref, d_ref,
               wg_ref, oa_ref, ob_ref, wb_ref, wc_ref, lam_ref, pow_ref, up_ref, x_ref, y_ref,
               state_ref):
    n = S5_CH

    @pl.when((pl.program_id(0) == 0) & (pl.program_id(1) == 0))
    def _():
        dt = jnp.exp(ldt_ref[...])
        ar = are_ref[...]
        ai = aim_ref[...]
        mag = jnp.exp(ar * dt)
        lr = mag * jnp.cos(ai * dt)
        li = mag * jnp.sin(ai * dt)
        den = ar * ar + ai * ai
        fr = ((lr - 1.0) * ar + li * ai) / den
        fi = (li * ar - (lr - 1.0) * ai) / den
        b_r = br_ref[...]
        b_i = bi_ref[...]
        wb_ref[:, :n] = (fr * b_r - fi * b_i).astype(BF16)
        wb_ref[:, n:] = (fr * b_i + fi * b_r).astype(BF16)
        wc_ref[:n, :] = cr_ref[...].astype(BF16)
        wc_ref[n:, :] = (-ci_ref[...]).astype(BF16)
        lam_ref[0:1, :] = lr
        lam_ref[1:2, :] = li

        def pw(p, carry):
            pr, pi = carry
            pow_ref[pl.ds(p, 1), :n] = pr
            pow_ref[pl.ds(p, 1), n:] = pi
            return pr * lr - pi * li, pr * li + pi * lr

        lax.fori_loop(0, S5_SEG, pw, (lr, li))

    @pl.when(pl.program_id(1) == 0)
    def _():
        state_ref[...] = jnp.zeros_like(state_ref)

    def permute_in(p, _):
        rows = pl.ds(pl.multiple_of(p * SUBLANES, SUBLANES), SUBLANES)
        up_ref[rows, :LANES] = ua_ref[0, pl.ds(p, SUBLANES, stride=S5_SEG), :]
        up_ref[rows, LANES:] = ub_ref[0, pl.ds(p, SUBLANES, stride=S5_SEG), :]
        return 0

    lax.fori_loop(0, S5_SEG, permute_in, 0)
    u = up_ref[...]
    x_ref[...] = jnp.dot(u.astype(BF16), wb_ref[...], preferred_element_type=F32)

    lr8 = jnp.broadcast_to(lam_ref[0:1, :], (SUBLANES, n))
    li8 = jnp.broadcast_to(lam_ref[1:2, :], (SUBLANES, n))

    def scan(p, carry):
        xr, xi = carry
        rows = pl.ds(pl.multiple_of(p * SUBLANES, SUBLANES), SUBLANES)
        nr = lr8 * xr - li8 * xi + x_ref[rows, :n]
        ni = lr8 * xi + li8 * xr + x_ref[rows, n:]
        x_ref[rows, :n] = nr
        x_ref[rows, n:] = ni
        return nr, ni

    zeros8 = jnp.zeros((SUBLANES, n), F32)
    end_r, end_i = lax.fori_loop(0, S5_SEG, scan, (zeros8, zeros8))

    gr = pow_ref[S5_SEG - 1:S5_SEG, :n]
    gi = pow_ref[S5_SEG - 1:S5_SEG, n:]
    cr = state_ref[0:1, :]
    ci = state_ref[1:2, :]
    ins_r, ins_i = [], []
    for j in range(SUBLANES):
        ins_r.append(cr)
        ins_i.append(ci)
        cr, ci = (gr * cr - gi * ci + end_r[j:j + 1, :], gr * ci + gi * cr + end_i[j:j + 1, :])
    state_ref[0:1, :] = cr
    state_ref[1:2, :] = ci
    in_r = jnp.concatenate(ins_r, axis=0)
    in_i = jnp.concatenate(ins_i, axis=0)

    def fixup(p, _):
        rows = pl.ds(pl.multiple_of(p * SUBLANES, SUBLANES), SUBLANES)
        pr = pow_ref[pl.ds(p, 1), :n]
        pi = pow_ref[pl.ds(p, 1), n:]
        x_ref[rows, :n] = x_ref[rows, :n] + (pr * in_r - pi * in_i)
        x_ref[rows, n:] = x_ref[rows, n:] + (pr * in_i + pi * in_r)
        return 0

    lax.fori_loop(0, S5_SEG, fixup, 0)

    y = jnp.dot(x_ref[...].astype(BF16), wc_ref[...], preferred_element_type=F32)
    y = jax.nn.gelu(y + d_ref[...] * u)
    z = jnp.dot(y.astype(BF16), wg_ref[...], preferred_element_type=F32)
    y_ref[...] = y * jax.nn.sigmoid(z)

    def permute_out(p, _):
        rows = pl.ds(pl.multiple_of(p * SUBLANES, SUBLANES), SUBLANES)
        oa_ref[0, pl.ds(p, SUBLANES, stride=S5_SEG), :] = y_ref[rows, :LANES]
        ob_ref[0, pl.ds(p, SUBLANES, stride=S5_SEG), :] = y_ref[rows, LANES:]
        return 0

    lax.fori_loop(0, S5_SEG, permute_out, 0)


def _s5(rest, prm):
    batch, seq, _ = rest.shape
    fixed = lambda b, j: (0, 0)
    row = pl.BlockSpec((1, S5_CH), fixed)
    half = lambda col: pl.BlockSpec((1, S5_BLK, LANES), lambda b, j, col=col: (b, j, col))
    out_half = jax.ShapeDtypeStruct((batch, seq, LANES), F32)
    return pl.pallas_call(
        _s5_kernel,
        grid=(batch, seq // S5_BLK),
        in_specs=[
            half(0), half(1),
            row, row, row,
            pl.BlockSpec((D_S5, S5_CH), fixed), pl.BlockSpec((D_S5, S5_CH), fixed),
            pl.BlockSpec((S5_CH, D_S5), fixed), pl.BlockSpec((S5_CH, D_S5), fixed),
            pl.BlockSpec((1, D_S5), fixed),
            pl.BlockSpec((D_S5, D_S5), fixed),
        ],
        out_specs=[half(0), half(0)],
        out_shape=[out_half, out_half],
        scratch_shapes=[
            pltpu.VMEM((D_S5, 2 * S5_CH), BF16),
            pltpu.VMEM((2 * S5_CH, D_S5), BF16),
            pltpu.VMEM((2, S5_CH), F32),
            pltpu.VMEM((S5_SEG, 2 * S5_CH), F32),
            pltpu.VMEM((S5_BLK, D_S5), F32),
            pltpu.VMEM((S5_BLK, 2 * S5_CH), F32),
            pltpu.VMEM((S5_BLK, D_S5), F32),
            pltpu.VMEM((2, S5_CH), F32),
        ],
        compiler_params=_cparams(("arbitrary", "arbitrary")),
        name="s5",
    )(rest, rest, prm["a_re"], prm["a_im"], prm["log_dt"], prm["b_re"], prm["b_im"],
      prm["c_re"], prm["c_im"], prm["d"], prm["w_glu"])


def _outproj_kernel(yf_ref, ysa_ref, ysb_ref, yr_ref, gate_ref, x_ref, w_ref, fn_ref, o_ref, *,
                    final):
    g = gate_ref[...]
    g = g * jax.nn.sigmoid(g)
    acc = x_ref[...]
    off = 0
    for y_ref in (yf_ref, ysa_ref, ysb_ref, yr_ref):
        width = y_ref.shape[-1]
        y = (y_ref[...] * g[:, off:off + width]).astype(BF16)
        acc = acc + jnp.dot(y, w_ref[off:off + width, :], preferred_element_type=F32)
        off += width
    if final:
        ms = jnp.mean(acc * acc, axis=-1, keepdims=True)
        acc = acc * lax.rsqrt(ms + EPS) * fn_ref[...]
    o_ref[...] = acc


def _outproj(y_fox, y_s5a, y_s5b, y_ret, rest, x2, w_out, fn_w, final):
    m = x2.shape[0]
    row = lambda i: (i, 0)
    fixed = lambda i: (0, 0)
    return pl.pallas_call(
        functools.partial(_outproj_kernel, final=final),
        grid=(m // TM_PROJ,),
        in_specs=[
            pl.BlockSpec((TM_PROJ, D_FOX), row),
            pl.BlockSpec((TM_PROJ, LANES), row),
            pl.BlockSpec((TM_PROJ, LANES), row),
            pl.BlockSpec((TM_PROJ, D_RET), row),
            pl.BlockSpec((TM_PROJ, D_MODEL), lambda i: (i, 1)),
            pl.BlockSpec((TM_PROJ, D_MODEL), row),
            pl.BlockSpec((D_MODEL, D_MODEL), fixed),
            pl.BlockSpec((1, D_MODEL), fixed),
        ],
        out_specs=pl.BlockSpec((TM_PROJ, D_MODEL), row),
        out_shape=jax.ShapeDtypeStruct((m, D_MODEL), F32),
        compiler_params=_cparams(("parallel",)),
        name="outproj_final" if final else "outproj",
    )(y_fox, y_s5a, y_s5b, y_ret, rest, x2, w_out, fn_w)


def _rotary_tables(seq):
    half = HEAD_DIM // 2
    freqs = ROPE_BASE ** (-jnp.arange(half, dtype=F32) / half)
    ang = jnp.arange(seq, dtype=F32)[:, None] * freqs[None, :]
    cos = jnp.cos(ang)
    sin = jnp.sin(ang)
    cos_t = jnp.tile(cos, (1, LANES // half))
    sin_t = jnp.tile(jnp.concatenate([-sin, sin], axis=-1), (1, LANES // HEAD_DIM))
    return cos_t, sin_t


def _s5_params(a_re, a_im, b_re, b_im, c_re, c_im, d, log_dt, w_glu):
    eye = jnp.eye(S5_GROUPS, dtype=F32)
    bd_b = lambda b: jnp.einsum("gph,gk->ghkp", b, eye).reshape(D_S5, S5_CH)
    bd_c = lambda c: jnp.einsum("ghp,gk->gpkh", c, eye).reshape(S5_CH, D_S5)
    return {
        "a_re": a_re.reshape(1, S5_CH), "a_im": a_im.reshape(1, S5_CH),
        "log_dt": jnp.repeat(log_dt, S5_STATE).reshape(1, S5_CH),
        "b_re": bd_b(b_re), "b_im": bd_b(b_im), "c_re": bd_c(c_re), "c_im": bd_c(c_im),
        "d": d.reshape(1, D_S5), "w_glu": w_glu.astype(BF16),
    }


def kernel(x, norm_w, w_in, fox_b_f, s5_a_re, s5_a_im, s5_b_re, s5_b_im, s5_c_re, s5_c_im,
           s5_d, s5_log_dt, s5_w_glu, ret_gn_w, w_out, final_norm_w):
    batch, seq, _ = x.shape
    depth = w_in.shape[0]
    m = batch * seq
    cos_t, sin_t = _rotary_tables(seq)
    log_gamma = jnp.log1p(-(2.0 ** (-5.0 - jnp.arange(RET_HEADS, dtype=F32))))
    lg = jnp.repeat(log_gamma, HEAD_DIM).reshape(1, D_RET)
    fn_w = final_norm_w.reshape(1, D_MODEL)
    flog_lo = N_QKV
    flog_hi = N_QKV + FOX_HEADS

    x2 = x.reshape(m, D_MODEL)
    for l in range(depth):
        wa = w_in[l, :, :flog_lo].astype(BF16)
        wc = jnp.pad(w_in[l, :, flog_lo:flog_hi], ((0, 0), (0, LANES - FOX_HEADS))).astype(BF16)
        wb = w_in[l, :, flog_hi:].astype(BF16)
        bf_pad = jnp.pad(fox_b_f[l], (0, LANES - FOX_HEADS)).reshape(1, LANES)

        qkv, rest, flog = _inproj(x2, norm_w[l].reshape(1, D_MODEL), wa, wb, wc)
        qkv = qkv.reshape(batch, seq, N_QKV)
        rest3 = rest.reshape(batch, seq, N_REST)
        c = _gate_cumsum(flog.reshape(batch, seq, LANES), bf_pad, batch, seq)
        y_fox = _fox_attention(qkv, c)
        y_s5a, y_s5b = _s5(rest3, _s5_params(s5_a_re[l], s5_a_im[l], s5_b_re[l], s5_b_im[l],
                                     s5_c_re[l], s5_c_im[l], s5_d[l], s5_log_dt[l], s5_w_glu[l]))
        y_ret = _retention(rest3, cos_t, sin_t, lg, ret_gn_w[l].reshape(1, D_RET))
        x2 = _outproj(y_fox.reshape(m, D_FOX), y_s5a.reshape(m, LANES), y_s5b.reshape(m, LANES),
                      y_ret.reshape(m, D_RET), rest, x2, w_out[l].astype(BF16), fn_w,
                      final=(l == depth - 1))
    return x2.reshape(batch, seq, D_MODEL)
```

```python
import functools
import math

import jax
import jax.numpy as jnp
from jax import lax
from jax.experimental import pallas as pl
from jax.experimental.pallas import tpu as pltpu

F32 = jnp.float32
BF16 = jnp.bfloat16

D_MODEL = 1024
HEAD_DIM = 64
CHUNK = 64
D_FOX = 512
FOX_HEADS = 8
D_S5 = 256
S5_GROUPS = 16
S5_GROUP_CH = 16
S5_STATE = 64
S5_CH = S5_GROUPS * S5_STATE
D_RET = 256
RET_HEADS = 4
ROPE_BASE = 10000.0
EPS = 1e-6

LANES = 128
SUBLANES = 8
N_QKV = 3 * D_FOX
N_REST = D_S5 + 3 * D_RET + D_MODEL
VMEM_LIMIT = 56 * 1024 * 1024

TM_PROJ = 512
T_GATE = 128
T_FOX = 512
FOX_PAIRS = 2
T_RET = 256
RET_BLK = 512
S5_SEG = 128
S5_BLK = SUBLANES * S5_SEG
NEG_BIG = -1e30
LOG2E = math.log2(math.e)
Q_SCALE = LOG2E / math.sqrt(HEAD_DIM)


def _cparams(sem, flags=None):
    return pltpu.CompilerParams(dimension_semantics=sem, vmem_limit_bytes=VMEM_LIMIT, flags=flags)


def _inproj_kernel(x_ref, g_ref, wa_ref, wb_ref, wc_ref, bf_ref, oa_ref, ob_ref, c_ref,
                   carry_ref, *, tiles_per_seq):
    @pl.when(pl.program_id(0) % tiles_per_seq == 0)
    def _():
        carry_ref[...] = jnp.zeros_like(carry_ref)

    x = x_ref[...]
    ms = jnp.mean(x * x, axis=-1, keepdims=True)
    h = (x * lax.rsqrt(ms + EPS) * g_ref[...]).astype(BF16)
    qkv = jnp.dot(h, wa_ref[...], preferred_element_type=F32)
    oa_ref[:, :D_FOX] = (qkv[:, :D_FOX] * Q_SCALE).astype(oa_ref.dtype)
    oa_ref[:, D_FOX:] = qkv[:, D_FOX:].astype(oa_ref.dtype)
    ob_ref[...] = jnp.dot(h, wb_ref[...], preferred_element_type=F32)

    z = jnp.dot(h, wc_ref[...], preferred_element_type=F32) + bf_ref[...]
    logf = jnp.minimum(z, 0.0) - jnp.log1p(jnp.exp(-jnp.abs(z)))
    r = lax.broadcasted_iota(jnp.int32, (T_GATE, T_GATE), 0)
    s = lax.broadcasted_iota(jnp.int32, (T_GATE, T_GATE), 1)
    tri = jnp.where(s <= r, 1.0, 0.0).astype(F32)
    carry = carry_ref[...]
    for blk in range(TM_PROJ // T_GATE):
        rows = slice(blk * T_GATE, (blk + 1) * T_GATE)
        cs = jnp.dot(tri, logf[rows, :], precision=lax.Precision.HIGHEST,
                     preferred_element_type=F32) + carry
        c_ref[rows, :] = cs
        carry = cs[T_GATE - 1:T_GATE, :]
    carry_ref[...] = carry


def _inproj(x2, g, wa, wb, wc, bf_pad, seq):
    m = x2.shape[0]
    row = lambda i: (i, 0)
    fixed = lambda i: (0, 0)
    return pl.pallas_call(
        functools.partial(_inproj_kernel, tiles_per_seq=seq // TM_PROJ),
        grid=(m // TM_PROJ,),
        in_specs=[
            pl.BlockSpec((TM_PROJ, D_MODEL), row),
            pl.BlockSpec((1, D_MODEL), fixed),
            pl.BlockSpec((D_MODEL, N_QKV), fixed),
            pl.BlockSpec((D_MODEL, N_REST), fixed),
            pl.BlockSpec((D_MODEL, LANES), fixed),
            pl.BlockSpec((1, LANES), fixed),
        ],
        out_specs=[
            pl.BlockSpec((TM_PROJ, N_QKV), row),
            pl.BlockSpec((TM_PROJ, N_REST), row),
            pl.BlockSpec((TM_PROJ, LANES), row),
        ],
        out_shape=[
            jax.ShapeDtypeStruct((m, N_QKV), BF16),
            jax.ShapeDtypeStruct((m, N_REST), F32),
            jax.ShapeDtypeStruct((m, LANES), F32),
        ],
        scratch_shapes=[pltpu.VMEM((1, LANES), F32)],
        compiler_params=_cparams(("arbitrary",)),
        name="inproj",
    )(x2, g, wa, wb, wc, bf_pad)


def _fox_kernel(q_ref, k_ref, v_ref, c_ref, o_ref, kaug_ref, vaug_ref):
    t = T_FOX
    hp = pl.program_id(1)
    i = pl.program_id(2)
    seq = k_ref.shape[1]
    lane = lax.broadcasted_iota(jnp.int32, (t, LANES), 1)
    own = (lane < HEAD_DIM, lane >= HEAD_DIM)
    extra0 = (HEAD_DIM, 0)
    heads = [(g, e) for g in range(FOX_PAIRS) for e in range(2)]
    cols = lambda g: slice(g * LANES, (g + 1) * LANES)

    def head_column(c_tile, g, e):
        head = 2 * (hp * FOX_PAIRS + g) + e
        return jnp.sum(jnp.where(lane[:c_tile.shape[0]] == head, c_tile, 0.0),
                       axis=-1, keepdims=True)

    @pl.when(i == 0)
    def _():
        def build(j, _):
            rows = pl.ds(pl.multiple_of(j * t, t), t)
            c_tile = c_ref[0, rows, :]
            for n, (g, e) in enumerate(heads):
                k = k_ref[0, rows, cols(g)]
                v = v_ref[0, rows, cols(g)]
                b = -LOG2E * head_column(c_tile, g, e)
                hi = b.astype(BF16).astype(F32)
                mid = (b - hi).astype(BF16).astype(F32)
                lo = b - hi - mid
                x0 = extra0[e]
                extra = jnp.where(lane == x0, hi, jnp.where(lane == x0 + 1, mid, jnp.where(
                    lane == x0 + 2, lo, jnp.where(lane == x0 + 3, 1.0, 0.0))))
                ka = jnp.where(own[e], k.astype(F32), extra)
                kaug_ref[n, :, rows] = ka.T.astype(BF16)
                vaug_ref[n, rows, :] = jnp.where(own[e], v, jnp.ones_like(v))
            return 0

        lax.fori_loop(0, seq // t, build, 0)

    c_row0 = c_ref[0, pl.ds(pl.multiple_of(i * t, t), SUBLANES), :][0:1, :]
    qa = []
    for g, e in heads:
        cq0 = LOG2E * head_column(c_row0, g, e)
        x0 = extra0[e]
        extra = jnp.where(lane == x0 + 3, cq0,
                          jnp.where((lane >= x0) & (lane < x0 + 3), 1.0, 0.0))
        qa.append(jnp.where(own[e], q_ref[0, :, cols(g)], extra.astype(BF16)))

    def tile(j, carry, masked):
        m, acc = carry
        rows = pl.ds(pl.multiple_of(j * t, t), t)
        new_m, new_acc = [], []
        for n in range(len(heads)):
            s = jnp.dot(qa[n], kaug_ref[n, :, rows], preferred_element_type=F32)
            if masked:
                r = lax.broadcasted_iota(jnp.int32, (t, t), 0)
                cidx = lax.broadcasted_iota(jnp.int32, (t, t), 1)
                s = jnp.where(cidx <= r, s, NEG_BIG)
            m_new = jnp.maximum(m[n], jnp.max(s, axis=-1, keepdims=True))
            alpha = jnp.exp2(m[n] - m_new)
            p = jnp.exp2(s - m_new).astype(BF16)
            new_m.append(m_new)
            new_acc.append(acc[n] * alpha + jnp.dot(p, vaug_ref[n, rows, :],
                                                    preferred_element_type=F32))
        return tuple(new_m), tuple(new_acc)

    init = ((jnp.full((t, 1), NEG_BIG, F32),) * len(heads),
            (jnp.zeros((t, LANES), F32),) * len(heads))
    carry = lax.fori_loop(0, i, lambda j, c: tile(j, c, False), init)
    _, acc = tile(i, carry, True)
    outs = [a / pltpu.roll(a, HEAD_DIM, 1) for a in acc]
    for g in range(FOX_PAIRS):
        o_ref[0, :, cols(g)] = jnp.where(own[0], outs[2 * g], outs[2 * g + 1])


def _fox_attention(qkv, c):
    batch, seq, _ = qkv.shape
    width = FOX_PAIRS * LANES
    n_blk = D_FOX // width
    n_heads = 2 * FOX_PAIRS
    return pl.pallas_call(
        _fox_kernel,
        grid=(batch, n_blk, seq // T_FOX),
        in_specs=[
            pl.BlockSpec((1, T_FOX, width), lambda b, h, i: (b, i, h)),
            pl.BlockSpec((1, seq, width), lambda b, h, i: (b, 0, n_blk + h)),
            pl.BlockSpec((1, seq, width), lambda b, h, i: (b, 0, 2 * n_blk + h)),
            pl.BlockSpec((1, seq, LANES), lambda b, h, i: (b, 0, 0)),
        ],
        out_specs=pl.BlockSpec((1, T_FOX, width), lambda b, h, i: (b, i, h)),
        out_shape=jax.ShapeDtypeStruct((batch, seq, D_FOX), F32),
        scratch_shapes=[pltpu.VMEM((n_heads, LANES, seq), BF16),
                        pltpu.VMEM((n_heads, seq, LANES), BF16)],
        compiler_params=_cparams(("parallel", "parallel", "arbitrary")),
        name="fox_attention",
    )(qkv, qkv, qkv, c)


def _ret_kernel(q_ref, k_ref, v_ref, cos_ref, sin_ref, lg_ref, gn_ref, o_ref,
                state_ref, dmat_ref, wq_ref, wk_ref):
    t = T_RET
    n_pairs = D_RET // LANES
    scale = 1.0 / math.sqrt(HEAD_DIM)
    lane = lax.broadcasted_iota(jnp.int32, (t, LANES), 1)
    first = lane < HEAD_DIM
    low_half = (lane % HEAD_DIM) < (HEAD_DIM // 2)
    cols = lambda p: slice(p * LANES, (p + 1) * LANES)

    @pl.when((pl.program_id(0) == 0) & (pl.program_id(1) == 0))
    def _():
        pos = lax.broadcasted_iota(jnp.int32, (t, 1), 0).astype(F32)
        r = lax.broadcasted_iota(jnp.int32, (t, t), 0)
        s = lax.broadcasted_iota(jnp.int32, (t, t), 1)
        dist = jnp.abs(r - s).astype(F32)
        visible = (s // CHUNK) <= (r // CHUNK)
        for h in range(RET_HEADS):
            lg = lg_ref[:, h * HEAD_DIM:h * HEAD_DIM + 1]
            dmat_ref[h] = jnp.where(visible, jnp.exp(lg * dist) * scale, 0.0)
        wq_ref[...] = jnp.exp(lg_ref[...] * (pos + 1.0)) * scale
        wk_ref[...] = jnp.exp(lg_ref[...] * (t - 1.0 - pos))

    @pl.when(pl.program_id(1) == 0)
    def _():
        state_ref[...] = jnp.zeros_like(state_ref)

    rb = lax.broadcasted_iota(jnp.int32, (LANES, LANES), 0)
    cb = lax.broadcasted_iota(jnp.int32, (LANES, LANES), 1)
    same_head = (rb // HEAD_DIM) == (cb // HEAD_DIM)

    def head_mean(x):
        s0 = jnp.sum(jnp.where(first, x, 0.0), axis=-1, keepdims=True)
        s1 = jnp.sum(jnp.where(first, 0.0, x), axis=-1, keepdims=True)
        return jnp.where(first, s0, s1) * (1.0 / HEAD_DIM)

    for sub in range(RET_BLK // t):
        rows = slice(sub * t, (sub + 1) * t)
        cos = cos_ref[rows, :]
        sin = sin_ref[rows, :]

        def rotary(x):
            swapped = jnp.where(low_half, pltpu.roll(x, LANES - HEAD_DIM // 2, 1),
                                pltpu.roll(x, HEAD_DIM // 2, 1))
            return x * cos + swapped * sin

        for p in range(n_pairs):
            q = rotary(q_ref[0, rows, cols(p)])
            k = rotary(k_ref[0, rows, cols(p)])
            vb = v_ref[0, rows, cols(p)].astype(BF16)
            kb = k.astype(BF16)
            zero = jnp.zeros_like(q)
            inner = []
            for e in range(2):
                qe = jnp.where(first, q, zero) if e == 0 else jnp.where(first, zero, q)
                sc = lax.dot_general(qe.astype(BF16), kb, (((1,), (1,)), ((), ())),
                                     preferred_element_type=F32) * dmat_ref[2 * p + e]
                inner.append(jnp.dot(sc.astype(BF16), vb, preferred_element_type=F32))
            state = state_ref[p]
            o = jnp.where(first, inner[0], inner[1]) + jnp.dot(
                (q * wq_ref[:, cols(p)]).astype(BF16), state.astype(BF16),
                preferred_element_type=F32)
            upd = lax.dot_general((k * wk_ref[:, cols(p)]).astype(BF16), vb,
                                  (((0,), (0,)), ((), ())), preferred_element_type=F32)
            decay = jnp.exp(lg_ref[:, cols(p)] * float(t))
            state_ref[p] = jnp.where(same_head, state * decay + upd, 0.0)
            d = o - head_mean(o)
            var = head_mean(d * d)
            o_ref[0, rows, cols(p)] = d * lax.rsqrt(var + EPS) * gn_ref[:, cols(p)]


def _retention(rest, cos_t, sin_t, lg, gn_w):
    batch, seq, _ = rest.shape
    n_pairs = D_RET // LANES
    tok = lambda blk: pl.BlockSpec((1, RET_BLK, D_RET), lambda b, i, blk=blk: (b, i, blk))
    fixed = pl.BlockSpec((1, D_RET), lambda b, i: (0, 0))
    table = pl.BlockSpec((RET_BLK, LANES), lambda b, i: (i, 0))
    return pl.pallas_call(
        _ret_kernel,
        grid=(batch, seq // RET_BLK),
        in_specs=[tok(1), tok(2), tok(3), table, table, fixed, fixed],
        out_specs=pl.BlockSpec((1, RET_BLK, D_RET), lambda b, i: (b, i, 0)),
        out_shape=jax.ShapeDtypeStruct((batch, seq, D_RET), F32),
        scratch_shapes=[
            pltpu.VMEM((n_pairs, LANES, LANES), F32),
            pltpu.VMEM((RET_HEADS, T_RET, T_RET), F32),
            pltpu.VMEM((T_RET, D_RET), F32),
            pltpu.VMEM((T_RET, D_RET), F32),
        ],
        compiler_params=_cparams(("arbitrary", "arbitrary")),
        name="retention",
    )(rest, rest, rest, cos_t, sin_t, lg, gn_w)


def _s5_kernel(ua_ref, ub_ref, are_ref, aim_ref, ldt_ref, br_ref, bi_ref, cr_ref, ci_ref, d_ref,
               wg_ref, oa_ref, ob_ref, wb_ref, wc_ref, lam_ref, pow_ref, up_ref, x_ref, y_ref,
               state_ref):
    n = S5_CH

    @pl.when((pl.program_id(0) == 0) & (pl.program_id(1) == 0))
    def _():
        dt = jnp.exp(ldt_ref[...])
        ar = are_ref[...]
        ai = aim_ref[...]
        mag = jnp.exp(ar * dt)
        lr = mag * jnp.cos(ai * dt)
        li = mag * jnp.sin(ai * dt)
        den = ar * ar + ai * ai
        fr = ((lr - 1.0) * ar + li * ai) / den
        fi = (li * ar - (lr - 1.0) * ai) / den
        b_r = br_ref[...]
        b_i = bi_ref[...]
        wb_ref[:, :n] = (fr * b_r - fi * b_i).astype(BF16)
        wb_ref[:, n:] = (fr * b_i + fi * b_r).astype(BF16)
        wc_ref[:n, :] = cr_ref[...].astype(BF16)
        wc_ref[n:, :] = (-ci_ref[...]).astype(BF16)
        lam_ref[0:1, :] = lr
        lam_ref[1:2, :] = li

        def pw(p, carry):
            pr, pi = carry
            pow_ref[pl.ds(p, 1), :n] = pr
            pow_ref[pl.ds(p, 1), n:] = pi
            return pr * lr - pi * li, pr * li + pi * lr

        lax.fori_loop(0, S5_SEG, pw, (lr, li))

    @pl.when(pl.program_id(1) == 0)
    def _():
        state_ref[...] = jnp.zeros_like(state_ref)

    def permute_in(p, _):
        rows = pl.ds(pl.multiple_of(p * SUBLANES, SUBLANES), SUBLANES)
        up_ref[rows, :LANES] = ua_ref[0, pl.ds(p, SUBLANES, stride=S5_SEG), :]
        up_ref[rows, LANES:] = ub_ref[0, pl.ds(p, SUBLANES, stride=S5_SEG), :]
        return 0

    lax.fori_loop(0, S5_SEG, permute_in, 0)
    u = up_ref[...]
    x_ref[...] = jnp.dot(u.astype(BF16), wb_ref[...], preferred_element_type=F32)

    lr8 = jnp.broadcast_to(lam_ref[0:1, :], (SUBLANES, n))
    li8 = jnp.broadcast_to(lam_ref[1:2, :], (SUBLANES, n))

    def scan(p, carry):
        xr, xi = carry
        rows = pl.ds(pl.multiple_of(p * SUBLANES, SUBLANES), SUBLANES)
        nr = lr8 * xr - li8 * xi + x_ref[rows, :n]
        ni = lr8 * xi + li8 * xr + x_ref[rows, n:]
        x_ref[rows, :n] = nr
        x_ref[rows, n:] = ni
        return nr, ni

    zeros8 = jnp.zeros((SUBLANES, n), F32)
    end_r, end_i = lax.fori_loop(0, S5_SEG, scan, (zeros8, zeros8))

    gr = pow_ref[S5_SEG - 1:S5_SEG, :n]
    gi = pow_ref[S5_SEG - 1:S5_SEG, n:]
    cr = state_ref[0:1, :]
    ci = state_ref[1:2, :]
    ins_r, ins_i = [], []
    for j in range(SUBLANES):
        ins_r.append(cr)
        ins_i.append(ci)
        cr, ci = (gr * cr - gi * ci + end_r[j:j + 1, :], gr * ci + gi * cr + end_i[j:j + 1, :])
    state_ref[0:1, :] = cr
    state_ref[1:2, :] = ci
    in_r = jnp.concatenate(ins_r, axis=0)
    in_i = jnp.concatenate(ins_i, axis=0)

    def fixup(p, _):
        rows = pl.ds(pl.multiple_of(p * SUBLANES, SUBLANES), SUBLANES)
        pr = pow_ref[pl.ds(p, 1), :n]
        pi = pow_ref[pl.ds(p, 1), n:]
        x_ref[rows, :n] = x_ref[rows, :n] + (pr * in_r - pi * in_i)
        x_ref[rows, n:] = x_ref[rows, n:] + (pr * in_i + pi * in_r)
        return 0

    lax.fori_loop(0, S5_SEG, fixup, 0)

    y = jnp.dot(x_ref[...].astype(BF16), wc_ref[...], preferred_element_type=F32)
    y = jax.nn.gelu(y + d_ref[...] * u)
    z = jnp.dot(y.astype(BF16), wg_ref[...], preferred_element_type=F32)
    y_ref[...] = y * jax.nn.sigmoid(z)

    def permute_out(p, _):
        rows = pl.ds(pl.multiple_of(p * SUBLANES, SUBLANES), SUBLANES)
        oa_ref[0, pl.ds(p, SUBLANES, stride=S5_SEG), :] = y_ref[rows, :LANES]
        ob_ref[0, pl.ds(p, SUBLANES, stride=S5_SEG), :] = y_ref[rows, LANES:]
        return 0

    lax.fori_loop(0, S5_SEG, permute_out, 0)


def _s5(rest, prm):
    batch, seq, _ = rest.shape
    fixed = lambda b, j: (0, 0)
    row = pl.BlockSpec((1, S5_CH), fixed)
    half = lambda col: pl.BlockSpec((1, S5_BLK, LANES), lambda b, j, col=col: (b, j, col))
    out_half = jax.ShapeDtypeStruct((batch, seq, LANES), F32)
    return pl.pallas_call(
        _s5_kernel,
        grid=(batch, seq // S5_BLK),
        in_specs=[
            half(0), half(1),
            row, row, row,
            pl.BlockSpec((D_S5, S5_CH), fixed), pl.BlockSpec((D_S5, S5_CH), fixed),
            pl.BlockSpec((S5_CH, D_S5), fixed), pl.BlockSpec((S5_CH, D_S5), fixed),
            pl.BlockSpec((1, D_S5), fixed),
            pl.BlockSpec((D_S5, D_S5), fixed),
        ],
        out_specs=[half(0), half(0)],
        out_shape=[out_half, out_half],
        scratch_shapes=[
            pltpu.VMEM((D_S5, 2 * S5_CH), BF16),
            pltpu.VMEM((2 * S5_CH, D_S5), BF16),
            pltpu.VMEM((2, S5_CH), F32),
            pltpu.VMEM((S5_SEG, 2 * S5_CH), F32),
            pltpu.VMEM((S5_BLK, D_S5), F32),
            pltpu.VMEM((S5_BLK, 2 * S5_CH), F32),
            pltpu.VMEM((S5_BLK, D_S5), F32),
            pltpu.VMEM((2, S5_CH), F32),
        ],
        compiler_params=_cparams(("arbitrary", "arbitrary")),
        name="s5",
    )(rest, rest, prm["a_re"], prm["a_im"], prm["log_dt"], prm["b_re"], prm["b_im"],
      prm["c_re"], prm["c_im"], prm["d"], prm["w_glu"])


def _outproj_kernel(yf_ref, ysa_ref, ysb_ref, yr_ref, gate_ref, x_ref, w_ref, fn_ref, o_ref, *,
                    final):
    g = gate_ref[...]
    g = g * jax.nn.sigmoid(g)
    acc = x_ref[...]
    off = 0
    for y_ref in (yf_ref, ysa_ref, ysb_ref, yr_ref):
        width = y_ref.shape[-1]
        y = (y_ref[...] * g[:, off:off + width]).astype(BF16)
        acc = acc + jnp.dot(y, w_ref[off:off + width, :], preferred_element_type=F32)
        off += width
    if final:
        ms = jnp.mean(acc * acc, axis=-1, keepdims=True)
        acc = acc * lax.rsqrt(ms + EPS) * fn_ref[...]
    o_ref[...] = acc


def _outproj(y_fox, y_s5a, y_s5b, y_ret, rest, x2, w_out, fn_w, final):
    m = x2.shape[0]
    row = lambda i: (i, 0)
    fixed = lambda i: (0, 0)
    return pl.pallas_call(
        functools.partial(_outproj_kernel, final=final),
        grid=(m // TM_PROJ,),
        in_specs=[
            pl.BlockSpec((TM_PROJ, D_FOX), row),
            pl.BlockSpec((TM_PROJ, LANES), row),
            pl.BlockSpec((TM_PROJ, LANES), row),
            pl.BlockSpec((TM_PROJ, D_RET), row),
            pl.BlockSpec((TM_PROJ, D_MODEL), lambda i: (i, 1)),
            pl.BlockSpec((TM_PROJ, D_MODEL), row),
            pl.BlockSpec((D_MODEL, D_MODEL), fixed),
            pl.BlockSpec((1, D_MODEL), fixed),
        ],
        out_specs=pl.BlockSpec((TM_PROJ, D_MODEL), row),
        out_shape=jax.ShapeDtypeStruct((m, D_MODEL), F32),
        compiler_params=_cparams(("parallel",)),
        name="outproj_final" if final else "outproj",
    )(y_fox, y_s5a, y_s5b, y_ret, rest, x2, w_out, fn_w)


def _rotary_tables(seq):
    half = HEAD_DIM // 2
    freqs = ROPE_BASE ** (-jnp.arange(half, dtype=F32) / half)
    ang = jnp.arange(seq, dtype=F32)[:, None] * freqs[None, :]
    cos = jnp.cos(ang)
    sin = jnp.sin(ang)
    cos_t = jnp.tile(cos, (1, LANES // half))
    sin_t = jnp.tile(jnp.concatenate([-sin, sin], axis=-1), (1, LANES // HEAD_DIM))
    return cos_t, sin_t


def _s5_params(a_re, a_im, b_re, b_im, c_re, c_im, d, log_dt, w_glu):
    eye = jnp.eye(S5_GROUPS, dtype=F32)
    bd_b = lambda b: jnp.einsum("gph,gk->ghkp", b, eye).reshape(D_S5, S5_CH)
    bd_c = lambda c: jnp.einsum("ghp,gk->gpkh", c, eye).reshape(S5_CH, D_S5)
    return {
        "a_re": a_re.reshape(1, S5_CH), "a_im": a_im.reshape(1, S5_CH),
        "log_dt": jnp.repeat(log_dt, S5_STATE).reshape(1, S5_CH),
        "b_re": bd_b(b_re), "b_im": bd_b(b_im), "c_re": bd_c(c_re), "c_im": bd_c(c_im),
        "d": d.reshape(1, D_S5), "w_glu": w_glu.astype(BF16),
    }


def kernel(x, norm_w, w_in, fox_b_f, s5_a_re, s5_a_im, s5_b_re, s5_b_im, s5_c_re, s5_c_im,
           s5_d, s5_log_dt, s5_w_glu, ret_gn_w, w_out, final_norm_w):
    batch, seq, _ = x.shape
    depth = w_in.shape[0]
    m = batch * seq
    cos_t, sin_t = _rotary_tables(seq)
    log_gamma = jnp.log1p(-(2.0 ** (-5.0 - jnp.arange(RET_HEADS, dtype=F32))))
    lg = jnp.repeat(log_gamma, HEAD_DIM).reshape(1, D_RET)
    fn_w = final_norm_w.reshape(1, D_MODEL)
    flog_lo = N_QKV
    flog_hi = N_QKV + FOX_HEADS

    x2 = x.reshape(m, D_MODEL)
    for l in range(depth):
        wa = w_in[l, :, :flog_lo].astype(BF16)
        wc = jnp.pad(w_in[l, :, flog_lo:flog_hi], ((0, 0), (0, LANES - FOX_HEADS))).astype(BF16)
        wb = w_in[l, :, flog_hi:].astype(BF16)
        bf_pad = jnp.pad(fox_b_f[l], (0, LANES - FOX_HEADS)).reshape(1, LANES)

        qkv, rest, c = _inproj(x2, norm_w[l].reshape(1, D_MODEL), wa, wb, wc, bf_pad, seq)
        qkv = qkv.reshape(batch, seq, N_QKV)
        rest3 = rest.reshape(batch, seq, N_REST)
        y_fox = _fox_attention(qkv, c.reshape(batch, seq, LANES))
        y_s5a, y_s5b = _s5(rest3, _s5_params(s5_a_re[l], s5_a_im[l], s5_b_re[l], s5_b_im[l],
                                     s5_c_re[l], s5_c_im[l], s5_d[l], s5_log_dt[l], s5_w_glu[l]))
        y_ret = _retention(rest3, cos_t, sin_t, lg, ret_gn_w[l].reshape(1, D_RET))
        x2 = _outproj(y_fox.reshape(m, D_FOX), y_s5a.reshape(m, LANES), y_s5b.reshape(m, LANES),
                      y_ret.reshape(m, D_RET), rest, x2, w_out[l].astype(BF16), fn_w,
                      final=(l == depth - 1))
    return x2.reshape(batch, seq, D_MODEL)
```

```python
import functools
import math

import jax
import jax.numpy as jnp
from jax import lax
from jax.experimental import pallas as pl
from jax.experimental.pallas import tpu as pltpu

F32 = jnp.float32
BF16 = jnp.bfloat16

D_MODEL = 1024
HEAD_DIM = 64
CHUNK = 64
D_FOX = 512
FOX_HEADS = 8
D_S5 = 256
S5_GROUPS = 16
S5_GROUP_CH = 16
S5_STATE = 64
S5_CH = S5_GROUPS * S5_STATE
D_RET = 256
RET_HEADS = 4
ROPE_BASE = 10000.0
EPS = 1e-6

LANES = 128
SUBLANES = 8
N_QKV = 3 * D_FOX
N_REST = D_S5 + D_MODEL + 3 * D_RET
VMEM_LIMIT = 56 * 1024 * 1024

TM_PROJ = 512
T_FOX = 512
T_RET = 256
RET_BLK = 512
S5_SEG = 128
S5_BLK = SUBLANES * S5_SEG
NEG_BIG = -1e30
LOG2E = math.log2(math.e)
Q_SCALE = LOG2E / math.sqrt(HEAD_DIM)


def _cparams(sem, flags=None):
    return pltpu.CompilerParams(dimension_semantics=sem, vmem_limit_bytes=VMEM_LIMIT, flags=flags)


def _inproj_kernel(x_ref, g_ref, wa_ref, wb_ref, wc_ref, bf_ref, oa_ref, su_ref, ob_ref, c_ref,
                   carry_ref, *, tiles_per_seq):
    @pl.when(pl.program_id(0) % tiles_per_seq == 0)
    def _():
        carry_ref[...] = jnp.zeros_like(carry_ref)

    x = x_ref[...]
    ms = jnp.mean(x * x, axis=-1, keepdims=True)
    h = (x * lax.rsqrt(ms + EPS) * g_ref[...]).astype(BF16)
    qkv = jnp.dot(h, wa_ref[...], preferred_element_type=F32)
    oa_ref[:, :D_FOX] = (qkv[:, :D_FOX] * Q_SCALE).astype(oa_ref.dtype)
    oa_ref[:, D_FOX:] = qkv[:, D_FOX:].astype(oa_ref.dtype)
    rest = jnp.dot(h, wb_ref[...], preferred_element_type=F32)
    su_ref[...] = rest[:, :D_S5]
    ob_ref[...] = rest[:, D_S5:].astype(ob_ref.dtype)

    z = jnp.dot(h, wc_ref[...], preferred_element_type=F32) + bf_ref[...]
    logf = jnp.minimum(z, 0.0) - jnp.log1p(jnp.exp(-jnp.abs(z)))
    hi = logf.astype(BF16)
    rem = logf - hi.astype(F32)
    mid = rem.astype(BF16)
    lo = (rem - mid.astype(F32)).astype(BF16)
    r = lax.broadcasted_iota(jnp.int32, (TM_PROJ, TM_PROJ), 0)
    s = lax.broadcasted_iota(jnp.int32, (TM_PROJ, TM_PROJ), 1)
    tri = jnp.where(s <= r, 1.0, 0.0).astype(BF16)
    cs = carry_ref[...] + sum(jnp.dot(tri, piece, preferred_element_type=F32)
                              for piece in (lo, mid, hi))
    c_ref[...] = cs
    carry_ref[...] = cs[TM_PROJ - 1:TM_PROJ, :]


def _inproj(x2, g, wa, wb, wc, bf_pad, seq):
    m = x2.shape[0]
    row = lambda i: (i, 0)
    fixed = lambda i: (0, 0)
    return pl.pallas_call(
        functools.partial(_inproj_kernel, tiles_per_seq=seq // TM_PROJ),
        grid=(m // TM_PROJ,),
        in_specs=[
            pl.BlockSpec((TM_PROJ, D_MODEL), row),
            pl.BlockSpec((1, D_MODEL), fixed),
            pl.BlockSpec((D_MODEL, N_QKV), fixed),
            pl.BlockSpec((D_MODEL, N_REST), fixed),
            pl.BlockSpec((D_MODEL, LANES), fixed),
            pl.BlockSpec((1, LANES), fixed),
        ],
        out_specs=[
            pl.BlockSpec((TM_PROJ, N_QKV), row),
            pl.BlockSpec((TM_PROJ, D_S5), row),
            pl.BlockSpec((TM_PROJ, N_REST - D_S5), row),
            pl.BlockSpec((TM_PROJ, LANES), row),
        ],
        out_shape=[
            jax.ShapeDtypeStruct((m, N_QKV), BF16),
            jax.ShapeDtypeStruct((m, D_S5), F32),
            jax.ShapeDtypeStruct((m, N_REST - D_S5), BF16),
            jax.ShapeDtypeStruct((m, LANES), F32),
        ],
        scratch_shapes=[pltpu.VMEM((1, LANES), F32)],
        compiler_params=_cparams(("arbitrary",)),
        name="inproj",
    )(x2, g, wa, wb, wc, bf_pad)


def _fox_kernel(q_ref, k_ref, v_ref, c_ref, o_ref, kaug_ref, vaug_ref):
    t = T_FOX
    i = pl.program_id(1)
    seq = k_ref.shape[1]
    lane = lax.broadcasted_iota(jnp.int32, (t, LANES), 1)
    lane1 = lane[0:1, :]
    own = (lane < HEAD_DIM, lane >= HEAD_DIM)
    n_pairs = D_FOX // LANES
    heads = [(g, e) for g in range(n_pairs) for e in range(2)]
    cols = lambda g: slice(g * LANES, (g + 1) * LANES)
    x0 = lambda e: HEAD_DIM if e == 0 else 0
    shift = lambda g, e: (x0(e) - (2 * g + e)) % LANES

    @pl.when(i == 0)
    def _():
        def build(j, _):
            rows = pl.ds(pl.multiple_of(j * t, t), t)
            b = -LOG2E * c_ref[0, rows, :]
            hi = b.astype(BF16).astype(F32)
            mid = (b - hi).astype(BF16).astype(F32)
            lo = b - hi - mid
            pieces = jnp.where(lane < 8, hi, jnp.where(lane < 16, pltpu.roll(mid, 8, 1), jnp.where(
                lane < 24, pltpu.roll(lo, 16, 1), jnp.where(lane < 32, 1.0, 0.0))))
            for n, (g, e) in enumerate(heads):
                extra = pltpu.roll(pieces, shift(g, e), 1).astype(BF16)
                kaug_ref[n, rows, :] = jnp.where(own[e], k_ref[0, rows, cols(g)], extra)
                v = v_ref[0, rows, cols(g)]
                vaug_ref[n, rows, :] = jnp.where(own[e], v, jnp.ones_like(v))
            return 0

        lax.fori_loop(0, seq // t, build, 0)

    cq0 = LOG2E * c_ref[0, pl.ds(pl.multiple_of(i * t, t), SUBLANES), :][0:1, :]
    qa = []
    for g, e in heads:
        ones = (lane1 == x0(e)) | (lane1 == x0(e) + 8) | (lane1 == x0(e) + 16)
        extra = jnp.where(ones, 1.0, jnp.where(
            lane1 == x0(e) + 24, pltpu.roll(cq0, (shift(g, e) + 24) % LANES, 1), 0.0))
        qa.append(jnp.where(own[e], q_ref[0, :, cols(g)], extra.astype(BF16)))

    def tile(j, carry, masked):
        m, acc = carry
        rows = pl.ds(pl.multiple_of(j * t, t), t)
        new_m, new_acc = [], []
        for n in range(len(heads)):
            s = lax.dot_general(qa[n], kaug_ref[n, rows, :], (((1,), (1,)), ((), ())),
                                preferred_element_type=F32)
            if masked:
                r = lax.broadcasted_iota(jnp.int32, (t, t), 0)
                cidx = lax.broadcasted_iota(jnp.int32, (t, t), 1)
                s = jnp.where(cidx <= r, s, NEG_BIG)
            m_new = jnp.maximum(m[n], jnp.max(s, axis=-1, keepdims=True))
            alpha = jnp.exp2(m[n] - m_new)
            p = jnp.exp2(s - m_new).astype(BF16)
            new_m.append(m_new)
            new_acc.append(acc[n] * alpha + jnp.dot(p, vaug_ref[n, rows, :],
                                                    preferred_element_type=F32))
        return tuple(new_m), tuple(new_acc)

    init = ((jnp.full((t, 1), NEG_BIG, F32),) * len(heads),
            (jnp.zeros((t, LANES), F32),) * len(heads))
    carry = lax.fori_loop(0, i, lambda j, c: tile(j, c, False), init)
    _, acc = tile(i, carry, True)
    outs = [a / pltpu.roll(a, HEAD_DIM, 1) for a in acc]
    for g in range(n_pairs):
        o_ref[0, :, cols(g)] = jnp.where(own[0], outs[2 * g], outs[2 * g + 1]).astype(o_ref.dtype)


def _fox_attention(qkv, c):
    batch, seq, _ = qkv.shape
    n_heads = D_FOX // HEAD_DIM
    once = pl.Buffered(1)
    return pl.pallas_call(
        _fox_kernel,
        grid=(batch, seq // T_FOX),
        in_specs=[
            pl.BlockSpec((1, T_FOX, D_FOX), lambda b, i: (b, i, 0)),
            pl.BlockSpec((1, seq, D_FOX), lambda b, i: (b, 0, 1), pipeline_mode=once),
            pl.BlockSpec((1, seq, D_FOX), lambda b, i: (b, 0, 2), pipeline_mode=once),
            pl.BlockSpec((1, seq, LANES), lambda b, i: (b, 0, 0), pipeline_mode=once),
        ],
        out_specs=pl.BlockSpec((1, T_FOX, D_FOX), lambda b, i: (b, i, 0)),
        out_shape=jax.ShapeDtypeStruct((batch, seq, D_FOX), BF16),
        scratch_shapes=[pltpu.VMEM((n_heads, seq, LANES), BF16),
                        pltpu.VMEM((n_heads, seq, LANES), BF16)],
        compiler_params=_cparams(("parallel", "arbitrary")),
        name="fox_attention",
    )(qkv, qkv, qkv, c)


def _ret_kernel(q_ref, k_ref, v_ref, cos_ref, sin_ref, lg_ref, gn_ref, o_ref,
                state_ref, dmat_ref, wq_ref, wk_ref):
    t = T_RET
    n_pairs = D_RET // LANES
    scale = 1.0 / math.sqrt(HEAD_DIM)
    lane = lax.broadcasted_iota(jnp.int32, (t, LANES), 1)
    first = lane < HEAD_DIM
    low_half = (lane % HEAD_DIM) < (HEAD_DIM // 2)
    cols = lambda p: slice(p * LANES, (p + 1) * LANES)

    @pl.when((pl.program_id(0) == 0) & (pl.program_id(1) == 0))
    def _():
        pos = lax.broadcasted_iota(jnp.int32, (t, 1), 0).astype(F32)
        r = lax.broadcasted_iota(jnp.int32, (t, t), 0)
        s = lax.broadcasted_iota(jnp.int32, (t, t), 1)
        dist = jnp.abs(r - s).astype(F32)
        visible = (s // CHUNK) <= (r // CHUNK)
        for h in range(RET_HEADS):
            lg = lg_ref[:, h * HEAD_DIM:h * HEAD_DIM + 1]
            dmat_ref[h] = jnp.where(visible, jnp.exp(lg * dist) * scale, 0.0)
        wq_ref[...] = jnp.exp(lg_ref[...] * (pos + 1.0)) * scale
        wk_ref[...] = jnp.exp(lg_ref[...] * (t - 1.0 - pos))

    @pl.when(pl.program_id(1) == 0)
    def _():
        state_ref[...] = jnp.zeros_like(state_ref)

    rb = lax.broadcasted_iota(jnp.int32, (LANES, LANES), 0)
    cb = lax.broadcasted_iota(jnp.int32, (LANES, LANES), 1)
    same_head = (rb // HEAD_DIM) == (cb // HEAD_DIM)

    def head_mean(x):
        s0 = jnp.sum(jnp.where(first, x, 0.0), axis=-1, keepdims=True)
        s1 = jnp.sum(jnp.where(first, 0.0, x), axis=-1, keepdims=True)
        return jnp.where(first, s0, s1) * (1.0 / HEAD_DIM)

    for sub in range(RET_BLK // t):
        rows = slice(sub * t, (sub + 1) * t)
        cos = cos_ref[rows, :]
        sin = sin_ref[rows, :]

        def rotary(x):
            swapped = jnp.where(low_half, pltpu.roll(x, LANES - HEAD_DIM // 2, 1),
                                pltpu.roll(x, HEAD_DIM // 2, 1))
            return x * cos + swapped * sin

        for p in range(n_pairs):
            q = rotary(q_ref[0, rows, cols(p)].astype(F32))
            k = rotary(k_ref[0, rows, cols(p)].astype(F32))
            vb = v_ref[0, rows, cols(p)]
            kb = k.astype(BF16)
            zero = jnp.zeros_like(q)
            inner = []
            for e in range(2):
                qe = jnp.where(first, q, zero) if e == 0 else jnp.where(first, zero, q)
                sc = lax.dot_general(qe.astype(BF16), kb, (((1,), (1,)), ((), ())),
                                     preferred_element_type=F32) * dmat_ref[2 * p + e]
                inner.append(jnp.dot(sc.astype(BF16), vb, preferred_element_type=F32))
            state = state_ref[p]
            o = jnp.where(first, inner[0], inner[1]) + jnp.dot(
                (q * wq_ref[:, cols(p)]).astype(BF16), state.astype(BF16),
                preferred_element_type=F32)
            upd = lax.dot_general((k * wk_ref[:, cols(p)]).astype(BF16), vb,
                                  (((0,), (0,)), ((), ())), preferred_element_type=F32)
            decay = jnp.exp(lg_ref[:, cols(p)] * float(t))
            state_ref[p] = jnp.where(same_head, state * decay + upd, 0.0)
            d = o - head_mean(o)
            var = head_mean(d * d)
            o_ref[0, rows, cols(p)] = (d * lax.rsqrt(var + EPS) * gn_ref[:, cols(p)]).astype(
                o_ref.dtype)


def _retention(rest, cos_t, sin_t, lg, gn_w):
    batch, seq, _ = rest.shape
    n_pairs = D_RET // LANES
    tok = lambda blk: pl.BlockSpec((1, RET_BLK, D_RET), lambda b, i, blk=blk: (b, i, blk))
    fixed = pl.BlockSpec((1, D_RET), lambda b, i: (0, 0))
    table = pl.BlockSpec((RET_BLK, LANES), lambda b, i: (i, 0))
    return pl.pallas_call(
        _ret_kernel,
        grid=(batch, seq // RET_BLK),
        in_specs=[tok(4), tok(5), tok(6), table, table, fixed, fixed],
        out_specs=pl.BlockSpec((1, RET_BLK, D_RET), lambda b, i: (b, i, 0)),
        out_shape=jax.ShapeDtypeStruct((batch, seq, D_RET), BF16),
        scratch_shapes=[
            pltpu.VMEM((n_pairs, LANES, LANES), F32),
            pltpu.VMEM((RET_HEADS, T_RET, T_RET), F32),
            pltpu.VMEM((T_RET, D_RET), F32),
            pltpu.VMEM((T_RET, D_RET), F32),
        ],
        compiler_params=_cparams(("arbitrary", "arbitrary")),
        name="retention",
    )(rest, rest, rest, cos_t, sin_t, lg, gn_w)


def _s5_kernel(ua_ref, ub_ref, are_ref, aim_ref, ldt_ref, br_ref, bi_ref, cr_ref, ci_ref, d_ref,
               wg_ref, oa_ref, ob_ref, wb_ref, wc_ref, lam_ref, pow_ref, up_ref, x_ref, y_ref,
               state_ref):
    n = S5_CH

    @pl.when((pl.program_id(0) == 0) & (pl.program_id(1) == 0))
    def _():
        dt = jnp.exp(ldt_ref[...])
        ar = are_ref[...]
        ai = aim_ref[...]
        mag = jnp.exp(ar * dt)
        lr = mag * jnp.cos(ai * dt)
        li = mag * jnp.sin(ai * dt)
        den = ar * ar + ai * ai
        fr = ((lr - 1.0) * ar + li * ai) / den
        fi = (li * ar - (lr - 1.0) * ai) / den
        b_r = br_ref[...]
        b_i = bi_ref[...]
        wb_ref[:, :n] = (fr * b_r - fi * b_i).astype(BF16)
        wb_ref[:, n:] = (fr * b_i + fi * b_r).astype(BF16)
        wc_ref[:n, :] = cr_ref[...].astype(BF16)
        wc_ref[n:, :] = (-ci_ref[...]).astype(BF16)
        lam_ref[0:1, :] = lr
        lam_ref[1:2, :] = li

        def pw(p, carry):
            pr, pi = carry
            pow_ref[pl.ds(p, 1), :n] = pr
            pow_ref[pl.ds(p, 1), n:] = pi
            return pr * lr - pi * li, pr * li + pi * lr

        lax.fori_loop(0, S5_SEG, pw, (lr, li))

    @pl.when(pl.program_id(1) == 0)
    def _():
        state_ref[...] = jnp.zeros_like(state_ref)

    def permute_in(p, _):
        rows = pl.ds(pl.multiple_of(p * SUBLANES, SUBLANES), SUBLANES)
        up_ref[rows, :LANES] = ua_ref[0, pl.ds(p, SUBLANES, stride=S5_SEG), :]
        up_ref[rows, LANES:] = ub_ref[0, pl.ds(p, SUBLANES, stride=S5_SEG), :]
        return 0

    lax.fori_loop(0, S5_SEG, permute_in, 0)
    u = up_ref[...]
    x_ref[...] = jnp.dot(u.astype(BF16), wb_ref[...], preferred_element_type=F32)

    lr8 = jnp.broadcast_to(lam_ref[0:1, :], (SUBLANES, n))
    li8 = jnp.broadcast_to(lam_ref[1:2, :], (SUBLANES, n))

    def scan(p, carry):
        xr, xi = carry
        rows = pl.ds(pl.multiple_of(p * SUBLANES, SUBLANES), SUBLANES)
        nr = lr8 * xr - li8 * xi + x_ref[rows, :n]
        ni = lr8 * xi + li8 * xr + x_ref[rows, n:]
        x_ref[rows, :n] = nr
        x_ref[rows, n:] = ni
        return nr, ni

    zeros8 = jnp.zeros((SUBLANES, n), F32)
    end_r, end_i = lax.fori_loop(0, S5_SEG, scan, (zeros8, zeros8))

    gr = pow_ref[S5_SEG - 1:S5_SEG, :n]
    gi = pow_ref[S5_SEG - 1:S5_SEG, n:]
    cr = state_ref[0:1, :]
    ci = state_ref[1:2, :]
    ins_r, ins_i = [], []
    for j in range(SUBLANES):
        ins_r.append(cr)
        ins_i.append(ci)
        cr, ci = (gr * cr - gi * ci + end_r[j:j + 1, :], gr * ci + gi * cr + end_i[j:j + 1, :])
    state_ref[0:1, :] = cr
    state_ref[1:2, :] = ci
    in_r = jnp.concatenate(ins_r, axis=0)
    in_i = jnp.concatenate(ins_i, axis=0)

    def fixup(p, _):
        rows = pl.ds(pl.multiple_of(p * SUBLANES, SUBLANES), SUBLANES)
        pr = pow_ref[pl.ds(p, 1), :n]
        pi = pow_ref[pl.ds(p, 1), n:]
        x_ref[rows, :n] = x_ref[rows, :n] + (pr * in_r - pi * in_i)
        x_ref[rows, n:] = x_ref[rows, n:] + (pr * in_i + pi * in_r)
        return 0

    lax.fori_loop(0, S5_SEG, fixup, 0)

    y = jnp.dot(x_ref[...].astype(BF16), wc_ref[...], preferred_element_type=F32)
    y = jax.nn.gelu(y + d_ref[...] * u)
    z = jnp.dot(y.astype(BF16), wg_ref[...], preferred_element_type=F32)
    y_ref[...] = y * jax.nn.sigmoid(z)

    def permute_out(p, _):
        rows = pl.ds(pl.multiple_of(p * SUBLANES, SUBLANES), SUBLANES)
        oa_ref[0, pl.ds(p, SUBLANES, stride=S5_SEG), :] = y_ref[rows, :LANES]
        ob_ref[0, pl.ds(p, SUBLANES, stride=S5_SEG), :] = y_ref[rows, LANES:]
        return 0

    lax.fori_loop(0, S5_SEG, permute_out, 0)


def _s5(su, prm):
    batch, seq, _ = su.shape
    fixed = lambda b, j: (0, 0)
    row = pl.BlockSpec((1, S5_CH), fixed)
    half = lambda col: pl.BlockSpec((1, S5_BLK, LANES), lambda b, j, col=col: (b, j, col))
    out_half = jax.ShapeDtypeStruct((batch, seq, LANES), F32)
    return pl.pallas_call(
        _s5_kernel,
        grid=(batch, seq // S5_BLK),
        in_specs=[
            half(0), half(1),
            row, row, row,
            pl.BlockSpec((D_S5, S5_CH), fixed), pl.BlockSpec((D_S5, S5_CH), fixed),
            pl.BlockSpec((S5_CH, D_S5), fixed), pl.BlockSpec((S5_CH, D_S5), fixed),
            pl.BlockSpec((1, D_S5), fixed),
            pl.BlockSpec((D_S5, D_S5), fixed),
        ],
        out_specs=[half(0), half(0)],
        out_shape=[out_half, out_half],
        scratch_shapes=[
            pltpu.VMEM((D_S5, 2 * S5_CH), BF16),
            pltpu.VMEM((2 * S5_CH, D_S5), BF16),
            pltpu.VMEM((2, S5_CH), F32),
            pltpu.VMEM((S5_SEG, 2 * S5_CH), F32),
            pltpu.VMEM((S5_BLK, D_S5), F32),
            pltpu.VMEM((S5_BLK, 2 * S5_CH), F32),
            pltpu.VMEM((S5_BLK, D_S5), F32),
            pltpu.VMEM((2, S5_CH), F32),
        ],
        compiler_params=_cparams(("arbitrary", "arbitrary")),
        name="s5",
    )(su, su, prm["a_re"], prm["a_im"], prm["log_dt"], prm["b_re"], prm["b_im"],
      prm["c_re"], prm["c_im"], prm["d"], prm["w_glu"])


def _outproj_kernel(yf_ref, ysa_ref, ysb_ref, yr_ref, gate_ref, x_ref, w_ref, fn_ref, o_ref, *,
                    final):
    g = gate_ref[...].astype(F32)
    g = g * jax.nn.sigmoid(g)
    acc = x_ref[...]
    off = 0
    for y_ref in (yf_ref, ysa_ref, ysb_ref, yr_ref):
        width = y_ref.shape[-1]
        y = (y_ref[...] * g[:, off:off + width]).astype(BF16)
        acc = acc + jnp.dot(y, w_ref[off:off + width, :], preferred_element_type=F32)
        off += width
    if final:
        ms = jnp.mean(acc * acc, axis=-1, keepdims=True)
        acc = acc * lax.rsqrt(ms + EPS) * fn_ref[...]
    o_ref[...] = acc


def _outproj(y_fox, y_s5a, y_s5b, y_ret, rest, x2, w_out, fn_w, final):
    m = x2.shape[0]
    row = lambda i: (i, 0)
    fixed = lambda i: (0, 0)
    return pl.pallas_call(
        functools.partial(_outproj_kernel, final=final),
        grid=(m // TM_PROJ,),
        in_specs=[
            pl.BlockSpec((TM_PROJ, D_FOX), row),
            pl.BlockSpec((TM_PROJ, LANES), row),
            pl.BlockSpec((TM_PROJ, LANES), row),
            pl.BlockSpec((TM_PROJ, D_RET), row),
            pl.BlockSpec((TM_PROJ, D_MODEL), row),
            pl.BlockSpec((TM_PROJ, D_MODEL), row),
            pl.BlockSpec((D_MODEL, D_MODEL), fixed),
            pl.BlockSpec((1, D_MODEL), fixed),
        ],
        out_specs=pl.BlockSpec((TM_PROJ, D_MODEL), row),
        out_shape=jax.ShapeDtypeStruct((m, D_MODEL), F32),
        compiler_params=_cparams(("parallel",)),
        name="outproj_final" if final else "outproj",
    )(y_fox, y_s5a, y_s5b, y_ret, rest, x2, w_out, fn_w)


def _rotary_tables(seq):
    half = HEAD_DIM // 2
    freqs = ROPE_BASE ** (-jnp.arange(half, dtype=F32) / half)
    ang = jnp.arange(seq, dtype=F32)[:, None] * freqs[None, :]
    cos = jnp.cos(ang)
    sin = jnp.sin(ang)
    cos_t = jnp.tile(cos, (1, LANES // half))
    sin_t = jnp.tile(jnp.concatenate([-sin, sin], axis=-1), (1, LANES // HEAD_DIM))
    return cos_t, sin_t


def _s5_params(a_re, a_im, b_re, b_im, c_re, c_im, d, log_dt, w_glu):
    eye = jnp.eye(S5_GROUPS, dtype=F32)
    bd_b = lambda b: jnp.einsum("gph,gk->ghkp", b, eye).reshape(D_S5, S5_CH)
    bd_c = lambda c: jnp.einsum("ghp,gk->gpkh", c, eye).reshape(S5_CH, D_S5)
    return {
        "a_re": a_re.reshape(1, S5_CH), "a_im": a_im.reshape(1, S5_CH),
        "log_dt": jnp.repeat(log_dt, S5_STATE).reshape(1, S5_CH),
        "b_re": bd_b(b_re), "b_im": bd_b(b_im), "c_re": bd_c(c_re), "c_im": bd_c(c_im),
        "d": d.reshape(1, D_S5), "w_glu": w_glu.astype(BF16),
    }


def kernel(x, norm_w, w_in, fox_b_f, s5_a_re, s5_a_im, s5_b_re, s5_b_im, s5_c_re, s5_c_im,
           s5_d, s5_log_dt, s5_w_glu, ret_gn_w, w_out, final_norm_w):
    batch, seq, _ = x.shape
    depth = w_in.shape[0]
    m = batch * seq
    cos_t, sin_t = _rotary_tables(seq)
    log_gamma = jnp.log1p(-(2.0 ** (-5.0 - jnp.arange(RET_HEADS, dtype=F32))))
    lg = jnp.repeat(log_gamma, HEAD_DIM).reshape(1, D_RET)
    fn_w = final_norm_w.reshape(1, D_MODEL)
    flog_lo = N_QKV
    flog_hi = N_QKV + FOX_HEADS

    x2 = x.reshape(m, D_MODEL)
    for l in range(depth):
        wa = w_in[l, :, :flog_lo].astype(BF16)
        wc = jnp.pad(w_in[l, :, flog_lo:flog_hi], ((0, 0), (0, LANES - FOX_HEADS))).astype(BF16)
        gate_lo = flog_hi + D_S5 + 3 * D_RET
        wb = jnp.concatenate([w_in[l, :, flog_hi:flog_hi + D_S5], w_in[l, :, gate_lo:],
                              w_in[l, :, flog_hi + D_S5:gate_lo]], axis=1).astype(BF16)
        bf_pad = jnp.pad(fox_b_f[l], (0, LANES - FOX_HEADS)).reshape(1, LANES)

        qkv, su, rest, c = _inproj(x2, norm_w[l].reshape(1, D_MODEL), wa, wb, wc, bf_pad, seq)
        qkv = qkv.reshape(batch, seq, N_QKV)
        rest3 = rest.reshape(batch, seq, N_REST - D_S5)
        y_fox = _fox_attention(qkv, c.reshape(batch, seq, LANES))
        y_s5a, y_s5b = _s5(su.reshape(batch, seq, D_S5), _s5_params(s5_a_re[l], s5_a_im[l], s5_b_re[l], s5_b_im[l],
                                     s5_c_re[l], s5_c_im[l], s5_d[l], s5_log_dt[l], s5_w_glu[l]))
        y_ret = _retention(rest3, cos_t, sin_t, lg, ret_gn_w[l].reshape(1, D_RET))
        x2 = _outproj(y_fox.reshape(m, D_FOX), y_s5a.reshape(m, LANES), y_s5b.reshape(m, LANES),
                      y_ret.reshape(m, D_RET), rest, x2, w_out[l].astype(BF16), fn_w,
                      final=(l == depth - 1))
    return x2.reshape(batch, seq, D_MODEL)
```

```python
import functools
import math

import jax
import jax.numpy as jnp
from jax import lax
from jax.experimental import pallas as pl
from jax.experimental.pallas import tpu as pltpu

F32 = jnp.float32
BF16 = jnp.bfloat16

D_MODEL = 1024
HEAD_DIM = 64
CHUNK = 64
D_FOX = 512
FOX_HEADS = 8
D_S5 = 256
S5_GROUPS = 16
S5_GROUP_CH = 16
S5_STATE = 64
D_RET = 256
RET_HEADS = 4
ROPE_BASE = 10000.0
EPS = 1e-6

LANES = 128
SUBLANES = 8
N_QKV = 3 * D_FOX
N_REST = D_S5 + D_MODEL + 3 * D_RET
VMEM_LIMIT = 56 * 1024 * 1024

TM_PROJ = 512
T_FOX = 512
T_RET = 256
RET_BLK = 512
S5_T = 256
NEG_BIG = -1e30
LOG2E = math.log2(math.e)
Q_SCALE = LOG2E / math.sqrt(HEAD_DIM)


def _cparams(sem, flags=None):
    return pltpu.CompilerParams(dimension_semantics=sem, vmem_limit_bytes=VMEM_LIMIT, flags=flags)


def _rms_norm(x, g):
    ms = jnp.mean(x * x, axis=-1, keepdims=True)
    return x * lax.rsqrt(ms + EPS) * g


def _norm_kernel(x_ref, g_ref, h_ref):
    h_ref[...] = _rms_norm(x_ref[...], g_ref[...]).astype(h_ref.dtype)


def _first_norm(x2, g):
    m = x2.shape[0]
    row = lambda i: (i, 0)
    return pl.pallas_call(
        _norm_kernel,
        grid=(m // TM_PROJ,),
        in_specs=[pl.BlockSpec((TM_PROJ, D_MODEL), row), pl.BlockSpec((1, D_MODEL), lambda i: (0, 0))],
        out_specs=pl.BlockSpec((TM_PROJ, D_MODEL), row),
        out_shape=jax.ShapeDtypeStruct((m, D_MODEL), BF16),
        compiler_params=_cparams(("parallel",)),
        name="first_norm",
    )(x2, g)


def _inproj_kernel(h_ref, wa_ref, wb_ref, wc_ref, oa_ref, su_ref, ob_ref, fl_ref):
    h = h_ref[...]
    qkv = jnp.dot(h, wa_ref[...], preferred_element_type=F32)
    oa_ref[:, :D_FOX] = (qkv[:, :D_FOX] * Q_SCALE).astype(oa_ref.dtype)
    oa_ref[:, D_FOX:] = qkv[:, D_FOX:].astype(oa_ref.dtype)
    rest = jnp.dot(h, wb_ref[...], preferred_element_type=F32)
    su_ref[...] = rest[:, :D_S5]
    ob_ref[...] = rest[:, D_S5:].astype(ob_ref.dtype)
    fl_ref[...] = jnp.dot(h, wc_ref[...], preferred_element_type=F32)


def _inproj(h, wa, wb, wc):
    m = h.shape[0]
    row = lambda i: (i, 0)
    fixed = lambda i: (0, 0)
    return pl.pallas_call(
        _inproj_kernel,
        grid=(m // TM_PROJ,),
        in_specs=[
            pl.BlockSpec((TM_PROJ, D_MODEL), row),
            pl.BlockSpec((D_MODEL, N_QKV), fixed),
            pl.BlockSpec((D_MODEL, N_REST), fixed),
            pl.BlockSpec((D_MODEL, LANES), fixed),
        ],
        out_specs=[
            pl.BlockSpec((TM_PROJ, N_QKV), row),
            pl.BlockSpec((TM_PROJ, D_S5), row),
            pl.BlockSpec((TM_PROJ, N_REST - D_S5), row),
            pl.BlockSpec((TM_PROJ, LANES), row),
        ],
        out_shape=[
            jax.ShapeDtypeStruct((m, N_QKV), BF16),
            jax.ShapeDtypeStruct((m, D_S5), F32),
            jax.ShapeDtypeStruct((m, N_REST - D_S5), BF16),
            jax.ShapeDtypeStruct((m, LANES), F32),
        ],
        compiler_params=_cparams(("parallel",)),
        name="inproj",
    )(h, wa, wb, wc)


def _fox_kernel(q_ref, k_ref, v_ref, fl_ref, bf_ref, o_ref, kaug_ref, vaug_ref, c_ref):
    t = T_FOX
    i = pl.program_id(1)
    seq = k_ref.shape[1]
    lane = lax.broadcasted_iota(jnp.int32, (t, LANES), 1)
    lane1 = lane[0:1, :]
    own = (lane < HEAD_DIM, lane >= HEAD_DIM)
    n_pairs = D_FOX // LANES
    heads = [(g, e) for g in range(n_pairs) for e in range(2)]
    cols = lambda g: slice(g * LANES, (g + 1) * LANES)
    x0 = lambda e: HEAD_DIM if e == 0 else 0
    shift = lambda g, e: (x0(e) - (2 * g + e)) % LANES

    @pl.when(i == 0)
    def _():
        r = lax.broadcasted_iota(jnp.int32, (t, t), 0)
        s = lax.broadcasted_iota(jnp.int32, (t, t), 1)
        tri = jnp.where(s <= r, 1.0, 0.0).astype(BF16)

        def build(j, carry):
            rows = pl.ds(pl.multiple_of(j * t, t), t)
            z = fl_ref[0, rows, :] + bf_ref[...]
            logf = jnp.minimum(z, 0.0) - jnp.log1p(jnp.exp(-jnp.abs(z)))
            top = logf.astype(BF16)
            rem = logf - top.astype(F32)
            middle = rem.astype(BF16)
            bottom = (rem - middle.astype(F32)).astype(BF16)
            c = carry + sum(jnp.dot(tri, piece, preferred_element_type=F32)
                            for piece in (bottom, middle, top))
            c_ref[rows, :] = c
            b = -LOG2E * c
            hi = b.astype(BF16).astype(F32)
            mid = (b - hi).astype(BF16).astype(F32)
            lo = b - hi - mid
            pieces = jnp.where(lane < 8, hi, jnp.where(lane < 16, pltpu.roll(mid, 8, 1), jnp.where(
                lane < 24, pltpu.roll(lo, 16, 1), jnp.where(lane < 32, 1.0, 0.0))))
            for n, (g, e) in enumerate(heads):
                extra = pltpu.roll(pieces, shift(g, e), 1).astype(BF16)
                kaug_ref[n, rows, :] = jnp.where(own[e], k_ref[0, rows, cols(g)], extra)
                v = v_ref[0, rows, cols(g)]
                vaug_ref[n, rows, :] = jnp.where(own[e], v, jnp.ones_like(v))
            return c[t - 1:t, :]

        lax.fori_loop(0, seq // t, build, jnp.zeros((1, LANES), F32))

    cq0 = LOG2E * c_ref[pl.ds(pl.multiple_of(i * t, t), SUBLANES), :][0:1, :]
    qa = []
    for g, e in heads:
        ones = (lane1 == x0(e)) | (lane1 == x0(e) + 8) | (lane1 == x0(e) + 16)
        extra = jnp.where(ones, 1.0, jnp.where(
            lane1 == x0(e) + 24, pltpu.roll(cq0, (shift(g, e) + 24) % LANES, 1), 0.0))
        qa.append(jnp.where(own[e], q_ref[0, :, cols(g)], extra.astype(BF16)))

    def tile(j, carry, masked):
        m, acc = carry
        rows = pl.ds(pl.multiple_of(j * t, t), t)
        new_m, new_acc = [], []
        for n in range(len(heads)):
            s = lax.dot_general(qa[n], kaug_ref[n, rows, :], (((1,), (1,)), ((), ())),
                                preferred_element_type=F32)
            if masked:
                r = lax.broadcasted_iota(jnp.int32, (t, t), 0)
                cidx = lax.broadcasted_iota(jnp.int32, (t, t), 1)
                s = jnp.where(cidx <= r, s, NEG_BIG)
            m_new = jnp.maximum(m[n], jnp.max(s, axis=-1, keepdims=True))
            alpha = jnp.exp2(m[n] - m_new)
            p = jnp.exp2(s - m_new).astype(BF16)
            new_m.append(m_new)
            new_acc.append(acc[n] * alpha + jnp.dot(p, vaug_ref[n, rows, :],
                                                    preferred_element_type=F32))
        return tuple(new_m), tuple(new_acc)

    init = ((jnp.full((t, 1), NEG_BIG, F32),) * len(heads),
            (jnp.zeros((t, LANES), F32),) * len(heads))
    carry = lax.fori_loop(0, i, lambda j, c: tile(j, c, False), init)
    _, acc = tile(i, carry, True)
    outs = [a / pltpu.roll(a, HEAD_DIM, 1) for a in acc]
    for g in range(n_pairs):
        o_ref[0, :, cols(g)] = jnp.where(own[0], outs[2 * g], outs[2 * g + 1]).astype(o_ref.dtype)


def _fox_attention(qkv, flog, bf_pad):
    batch, seq, _ = qkv.shape
    n_heads = D_FOX // HEAD_DIM
    once = pl.Buffered(1)
    return pl.pallas_call(
        _fox_kernel,
        grid=(batch, seq // T_FOX),
        in_specs=[
            pl.BlockSpec((1, T_FOX, D_FOX), lambda b, i: (b, i, 0)),
            pl.BlockSpec((1, seq, D_FOX), lambda b, i: (b, 0, 1), pipeline_mode=once),
            pl.BlockSpec((1, seq, D_FOX), lambda b, i: (b, 0, 2), pipeline_mode=once),
            pl.BlockSpec((1, seq, LANES), lambda b, i: (b, 0, 0), pipeline_mode=once),
            pl.BlockSpec((1, LANES), lambda b, i: (0, 0)),
        ],
        out_specs=pl.BlockSpec((1, T_FOX, D_FOX), lambda b, i: (b, i, 0)),
        out_shape=jax.ShapeDtypeStruct((batch, seq, D_FOX), BF16),
        scratch_shapes=[pltpu.VMEM((n_heads, seq, LANES), BF16),
                        pltpu.VMEM((n_heads, seq, LANES), BF16),
                        pltpu.VMEM((seq, LANES), F32)],
        compiler_params=_cparams(("parallel", "arbitrary")),
        name="fox_attention",
    )(qkv, qkv, qkv, flog, bf_pad)


def _ret_kernel(q_ref, k_ref, v_ref, cos_ref, sin_ref, lg_ref, gn_ref, o_ref,
                state_ref, dmat_ref, wq_ref, wk_ref):
    t = T_RET
    n_pairs = D_RET // LANES
    scale = 1.0 / math.sqrt(HEAD_DIM)
    lane = lax.broadcasted_iota(jnp.int32, (t, LANES), 1)
    first = lane < HEAD_DIM
    low_half = (lane % HEAD_DIM) < (HEAD_DIM // 2)
    cols = lambda p: slice(p * LANES, (p + 1) * LANES)

    @pl.when((pl.program_id(0) == 0) & (pl.program_id(1) == 0))
    def _():
        pos = lax.broadcasted_iota(jnp.int32, (t, 1), 0).astype(F32)
        r = lax.broadcasted_iota(jnp.int32, (t, t), 0)
        s = lax.broadcasted_iota(jnp.int32, (t, t), 1)
        dist = jnp.abs(r - s).astype(F32)
        visible = (s // CHUNK) <= (r // CHUNK)
        for h in range(RET_HEADS):
            lg = lg_ref[:, h * HEAD_DIM:h * HEAD_DIM + 1]
            dmat_ref[h] = jnp.where(visible, jnp.exp(lg * dist) * scale, 0.0)
        wq_ref[...] = jnp.exp(lg_ref[...] * (pos + 1.0)) * scale
        wk_ref[...] = jnp.exp(lg_ref[...] * (t - 1.0 - pos))

    @pl.when(pl.program_id(1) == 0)
    def _():
        state_ref[...] = jnp.zeros_like(state_ref)

    rb = lax.broadcasted_iota(jnp.int32, (LANES, LANES), 0)
    cb = lax.broadcasted_iota(jnp.int32, (LANES, LANES), 1)
    same_head = (rb // HEAD_DIM) == (cb // HEAD_DIM)

    def head_mean(x):
        s0 = jnp.sum(jnp.where(first, x, 0.0), axis=-1, keepdims=True)
        s1 = jnp.sum(jnp.where(first, 0.0, x), axis=-1, keepdims=True)
        return jnp.where(first, s0, s1) * (1.0 / HEAD_DIM)

    for sub in range(RET_BLK // t):
        rows = slice(sub * t, (sub + 1) * t)
        cos = cos_ref[rows, :]
        sin = sin_ref[rows, :]

        def rotary(x):
            swapped = jnp.where(low_half, pltpu.roll(x, LANES - HEAD_DIM // 2, 1),
                                pltpu.roll(x, HEAD_DIM // 2, 1))
            return x * cos + swapped * sin

        for p in range(n_pairs):
            q = rotary(q_ref[0, rows, cols(p)].astype(F32))
            k = rotary(k_ref[0, rows, cols(p)].astype(F32))
            vb = v_ref[0, rows, cols(p)]
            kb = k.astype(BF16)
            zero = jnp.zeros_like(q)
            inner = []
            for e in range(2):
                qe = jnp.where(first, q, zero) if e == 0 else jnp.where(first, zero, q)
                sc = lax.dot_general(qe.astype(BF16), kb, (((1,), (1,)), ((), ())),
                                     preferred_element_type=F32) * dmat_ref[2 * p + e]
                inner.append(jnp.dot(sc.astype(BF16), vb, preferred_element_type=F32))
            state = state_ref[p]
            o = jnp.where(first, inner[0], inner[1]) + jnp.dot(
                (q * wq_ref[:, cols(p)]).astype(BF16), state.astype(BF16),
                preferred_element_type=F32)
            upd = lax.dot_general((k * wk_ref[:, cols(p)]).astype(BF16), vb,
                                  (((0,), (0,)), ((), ())), preferred_element_type=F32)
            decay = jnp.exp(lg_ref[:, cols(p)] * float(t))
            state_ref[p] = jnp.where(same_head, state * decay + upd, 0.0)
            d = o - head_mean(o)
            var = head_mean(d * d)
            o_ref[0, rows, cols(p)] = (d * lax.rsqrt(var + EPS) * gn_ref[:, cols(p)]).astype(
                o_ref.dtype)


def _retention(rest, cos_t, sin_t, lg, gn_w):
    batch, seq, _ = rest.shape
    n_pairs = D_RET // LANES
    tok = lambda blk: pl.BlockSpec((1, RET_BLK, D_RET), lambda b, i, blk=blk: (b, i, blk))
    fixed = pl.BlockSpec((1, D_RET), lambda b, i: (0, 0))
    table = pl.BlockSpec((RET_BLK, LANES), lambda b, i: (i, 0))
    return pl.pallas_call(
        _ret_kernel,
        grid=(batch, seq // RET_BLK),
        in_specs=[tok(4), tok(5), tok(6), table, table, fixed, fixed],
        out_specs=pl.BlockSpec((1, RET_BLK, D_RET), lambda b, i: (b, i, 0)),
        out_shape=jax.ShapeDtypeStruct((batch, seq, D_RET), BF16),
        scratch_shapes=[
            pltpu.VMEM((n_pairs, LANES, LANES), F32),
            pltpu.VMEM((RET_HEADS, T_RET, T_RET), F32),
            pltpu.VMEM((T_RET, D_RET), F32),
            pltpu.VMEM((T_RET, D_RET), F32),
        ],
        compiler_params=_cparams(("arbitrary", "arbitrary")),
        name="retention",
    )(rest, rest, rest, cos_t, sin_t, lg, gn_w)


def _s5_kernel(u_ref, are_ref, aim_ref, ldt_ref, br_ref, bi_ref, cr_ref, ci_ref, d_ref, wg_ref,
               o_ref, wb_ref, wc_ref, lam_ref, lhs_ref, x_ref, z_ref, state_ref):
    nb = u_ref.shape[0]
    t5 = S5_T
    win = D_S5 // SUBLANES

    @pl.when(pl.program_id(0) == 0)
    def _():
        dt = jnp.exp(ldt_ref[...])
        ar = are_ref[...]
        ai = aim_ref[...]
        mag = jnp.exp(ar * dt)
        lr = mag * jnp.cos(ai * dt)
        li = mag * jnp.sin(ai * dt)
        den = ar * ar + ai * ai
        fr = ((lr - 1.0) * ar + li * ai) / den
        fi = (li * ar - (lr - 1.0) * ai) / den
        lam_ref[0] = lr
        lam_ref[1] = li
        for j in range(SUBLANES):
            rows = slice(win * j, win * (j + 1))
            f_r = fr[j:j + 1, :]
            f_i = fi[j:j + 1, :]
            b_r = br_ref[rows, :]
            b_i = bi_ref[rows, :]
            wb_ref[rows, :LANES] = (f_r * b_r - f_i * b_i).astype(BF16)
            wb_ref[rows, LANES:] = (f_r * b_i + f_i * b_r).astype(BF16)
        wc_ref[:LANES, :] = cr_ref[...].astype(BF16)
        wc_ref[LANES:, :] = (-ci_ref[...]).astype(BF16)
        state_ref[...] = jnp.zeros_like(state_ref)

    sub = lax.broadcasted_iota(jnp.int32, (2 * SUBLANES, D_S5), 0)
    keep = sub % SUBLANES == lax.broadcasted_iota(jnp.int32, (2 * SUBLANES, D_S5), 1) // win
    first = sub < SUBLANES

    def expand(tt, _):
        for b in range(nb):
            tile = u_ref[b, pl.ds(pl.multiple_of(tt * SUBLANES, SUBLANES), SUBLANES), :]
            for s in range(0, SUBLANES, 2):
                two = jnp.where(first,
                                jnp.broadcast_to(tile[s:s + 1, :], (2 * SUBLANES, D_S5)),
                                jnp.broadcast_to(tile[s + 1:s + 2, :], (2 * SUBLANES, D_S5)))
                dst = pl.ds(pl.multiple_of((tt * SUBLANES + s) * SUBLANES, 2 * SUBLANES),
                            2 * SUBLANES)
                lhs_ref[b, dst, :] = jnp.where(keep, two, 0.0).astype(BF16)
        return 0

    lax.fori_loop(0, t5 // SUBLANES, expand, 0)
    for b in range(nb):
        x_ref[b] = jnp.dot(lhs_ref[b], wb_ref[...], preferred_element_type=F32)

    lr = lam_ref[0]
    li = lam_ref[1]

    def scan(t, carry):
        rows = pl.ds(pl.multiple_of(t * SUBLANES, SUBLANES), SUBLANES)
        new = []
        for b in range(nb):
            xr, xi = carry[b]
            nr = lr * xr - li * xi + x_ref[b, rows, :LANES]
            ni = lr * xi + li * xr + x_ref[b, rows, LANES:]
            x_ref[b, rows, :LANES] = nr
            x_ref[b, rows, LANES:] = ni
            new.append((nr, ni))
        return tuple(new)

    init = tuple((state_ref[b, :, :LANES], state_ref[b, :, LANES:]) for b in range(nb))
    final = lax.fori_loop(0, t5, scan, init, unroll=2)
    for b in range(nb):
        state_ref[b, :, :LANES] = final[b][0]
        state_ref[b, :, LANES:] = final[b][1]

    window = lax.broadcasted_iota(jnp.int32, (t5, LANES), 1) // win
    for b in range(nb):
        z = jnp.dot(x_ref[b].astype(BF16), wc_ref[...], preferred_element_type=F32)
        z_ref[b, 0] = z[:, :LANES]
        z_ref[b, 1] = z[:, LANES:]
    for b in range(nb):
        halves = []
        for hh in range(2):
            per = SUBLANES // 2
            y = z_ref[b, hh, pl.ds(per * hh + per - 1, t5, stride=SUBLANES), :]
            for jj in range(per - 2, -1, -1):
                y = jnp.where(window == jj,
                              z_ref[b, hh, pl.ds(per * hh + jj, t5, stride=SUBLANES), :], y)
            halves.append(y)
        u = u_ref[b]
        y = jax.nn.gelu(jnp.concatenate(halves, axis=1) + d_ref[...] * u)
        gate = jnp.dot(y.astype(BF16), wg_ref[...], preferred_element_type=F32)
        o_ref[b] = (y * jax.nn.sigmoid(gate)).astype(o_ref.dtype)


def _s5(su, prm):
    batch, seq, _ = su.shape
    fixed = lambda i: (0, 0)
    lam = pl.BlockSpec((SUBLANES, LANES), fixed)
    rows8 = SUBLANES * S5_T
    return pl.pallas_call(
        _s5_kernel,
        grid=(seq // S5_T,),
        in_specs=[
            pl.BlockSpec((batch, S5_T, D_S5), lambda i: (0, i, 0)),
            lam, lam, lam,
            pl.BlockSpec((D_S5, LANES), fixed), pl.BlockSpec((D_S5, LANES), fixed),
            pl.BlockSpec((LANES, D_S5), fixed), pl.BlockSpec((LANES, D_S5), fixed),
            pl.BlockSpec((1, D_S5), fixed),
            pl.BlockSpec((D_S5, D_S5), fixed),
        ],
        out_specs=pl.BlockSpec((batch, S5_T, D_S5), lambda i: (0, i, 0)),
        out_shape=jax.ShapeDtypeStruct((batch, seq, D_S5), BF16),
        scratch_shapes=[
            pltpu.VMEM((D_S5, 2 * LANES), BF16),
            pltpu.VMEM((2 * LANES, D_S5), BF16),
            pltpu.VMEM((2, SUBLANES, LANES), F32),
            pltpu.VMEM((batch, rows8, D_S5), BF16),
            pltpu.VMEM((batch, rows8, 2 * LANES), F32),
            pltpu.VMEM((batch, 2, rows8, LANES), F32),
            pltpu.VMEM((batch, SUBLANES, 2 * LANES), F32),
        ],
        compiler_params=_cparams(("arbitrary",)),
        name="s5",
    )(su, prm["a_re"], prm["a_im"], prm["log_dt"], prm["b_re"], prm["b_im"],
      prm["c_re"], prm["c_im"], prm["d"], prm["w_glu"])


def _outproj_kernel(yf_ref, ys_ref, yr_ref, gate_ref, x_ref, w_ref, g_ref, *o_refs, final):
    g = gate_ref[...].astype(F32)
    g = g * jax.nn.sigmoid(g)
    acc = x_ref[...]
    off = 0
    for y_ref in (yf_ref, ys_ref, yr_ref):
        width = y_ref.shape[-1]
        y = (y_ref[...] * g[:, off:off + width]).astype(BF16)
        acc = acc + jnp.dot(y, w_ref[off:off + width, :], preferred_element_type=F32)
        off += width
    normed = _rms_norm(acc, g_ref[...])
    if final:
        o_refs[0][...] = normed
    else:
        o_refs[0][...] = acc
        o_refs[1][...] = normed.astype(o_refs[1].dtype)


def _outproj(y_fox, y_s5, y_ret, rest, x2, w_out, norm_g, final):
    m = x2.shape[0]
    row = lambda i: (i, 0)
    fixed = lambda i: (0, 0)
    tile = pl.BlockSpec((TM_PROJ, D_MODEL), row)
    x_shape = jax.ShapeDtypeStruct((m, D_MODEL), F32)
    return pl.pallas_call(
        functools.partial(_outproj_kernel, final=final),
        grid=(m // TM_PROJ,),
        in_specs=[
            pl.BlockSpec((TM_PROJ, D_FOX), row),
            pl.BlockSpec((TM_PROJ, D_S5), row),
            pl.BlockSpec((TM_PROJ, D_RET), row),
            tile,
            tile,
            pl.BlockSpec((D_MODEL, D_MODEL), fixed),
            pl.BlockSpec((1, D_MODEL), fixed),
        ],
        out_specs=tile if final else [tile, tile],
        out_shape=x_shape if final else [x_shape, jax.ShapeDtypeStruct((m, D_MODEL), BF16)],
        compiler_params=_cparams(("parallel",)),
        name="outproj_final" if final else "outproj",
    )(y_fox, y_s5, y_ret, rest, x2, w_out, norm_g)


def _rotary_tables(seq):
    half = HEAD_DIM // 2
    freqs = ROPE_BASE ** (-jnp.arange(half, dtype=F32) / half)
    ang = jnp.arange(seq, dtype=F32)[:, None] * freqs[None, :]
    cos = jnp.cos(ang)
    sin = jnp.sin(ang)
    cos_t = jnp.tile(cos, (1, LANES // half))
    sin_t = jnp.tile(jnp.concatenate([-sin, sin], axis=-1), (1, LANES // HEAD_DIM))
    return cos_t, sin_t


def _s5_params(a_re, a_im, b_re, b_im, c_re, c_im, d, log_dt, w_glu):
    pair = jnp.eye(2, dtype=F32)
    half = S5_GROUPS // 2
    pack_b = lambda b: jnp.einsum("jnph,kn->jkhnp", b.reshape(half, 2, S5_STATE, S5_GROUP_CH),
                                  pair).reshape(D_S5, LANES)
    pack_c = lambda c: jnp.einsum("jnhp,kn->npjkh", c.reshape(half, 2, S5_GROUP_CH, S5_STATE),
                                  pair).reshape(LANES, D_S5)
    tile = lambda v: v.reshape(SUBLANES, LANES)
    return {
        "a_re": tile(a_re), "a_im": tile(a_im), "log_dt": tile(jnp.repeat(log_dt, S5_STATE)),
        "b_re": pack_b(b_re), "b_im": pack_b(b_im), "c_re": pack_c(c_re), "c_im": pack_c(c_im),
        "d": d.reshape(1, D_S5), "w_glu": w_glu.astype(BF16),
    }


def kernel(x, norm_w, w_in, fox_b_f, s5_a_re, s5_a_im, s5_b_re, s5_b_im, s5_c_re, s5_c_im,
           s5_d, s5_log_dt, s5_w_glu, ret_gn_w, w_out, final_norm_w):
    batch, seq, _ = x.shape
    depth = w_in.shape[0]
    m = batch * seq
    cos_t, sin_t = _rotary_tables(seq)
    log_gamma = jnp.log1p(-(2.0 ** (-5.0 - jnp.arange(RET_HEADS, dtype=F32))))
    lg = jnp.repeat(log_gamma, HEAD_DIM).reshape(1, D_RET)
    norm_g = jnp.concatenate([norm_w, final_norm_w[None]], axis=0).reshape(depth + 1, 1, D_MODEL)
    flog_lo = N_QKV
    flog_hi = N_QKV + FOX_HEADS

    x2 = x.reshape(m, D_MODEL)
    h = _first_norm(x2, norm_g[0])
    for l in range(depth):
        wa = w_in[l, :, :flog_lo].astype(BF16)
        wc = jnp.pad(w_in[l, :, flog_lo:flog_hi], ((0, 0), (0, LANES - FOX_HEADS))).astype(BF16)
        gate_lo = flog_hi + D_S5 + 3 * D_RET
        wb = jnp.concatenate([w_in[l, :, flog_hi:flog_hi + D_S5], w_in[l, :, gate_lo:],
                              w_in[l, :, flog_hi + D_S5:gate_lo]], axis=1).astype(BF16)
        bf_pad = jnp.pad(fox_b_f[l], (0, LANES - FOX_HEADS)).reshape(1, LANES)

        qkv, su, rest, flog = _inproj(h, wa, wb, wc)
        qkv = qkv.reshape(batch, seq, N_QKV)
        rest3 = rest.reshape(batch, seq, N_REST - D_S5)
        y_fox = _fox_attention(qkv, flog.reshape(batch, seq, LANES), bf_pad)
        y_s5 = _s5(su.reshape(batch, seq, D_S5),
                   _s5_params(s5_a_re[l], s5_a_im[l], s5_b_re[l], s5_b_im[l], s5_c_re[l],
                              s5_c_im[l], s5_d[l], s5_log_dt[l], s5_w_glu[l]))
        y_ret = _retention(rest3, cos_t, sin_t, lg, ret_gn_w[l].reshape(1, D_RET))
        out = _outproj(y_fox.reshape(m, D_FOX), y_s5.reshape(m, D_S5), y_ret.reshape(m, D_RET),
                       rest, x2, w_out[l].astype(BF16), norm_g[l + 1], final=(l == depth - 1))
        if l < depth - 1:
            x2, h = out
    return out.reshape(batch, seq, D_MODEL)
```

```python
import functools
import math

import jax
import jax.numpy as jnp
from jax import lax
from jax.experimental import pallas as pl
from jax.experimental.pallas import tpu as pltpu

F32 = jnp.float32
BF16 = jnp.bfloat16

D_MODEL = 1024
HEAD_DIM = 64
CHUNK = 64
D_FOX = 512
FOX_HEADS = 8
D_S5 = 256
S5_GROUPS = 16
S5_GROUP_CH = 16
S5_STATE = 64
D_RET = 256
RET_HEADS = 4
ROPE_BASE = 10000.0
EPS = 1e-6

LANES = 128
SUBLANES = 8
N_QKV = 3 * D_FOX
N_REST = D_S5 + D_MODEL + 3 * D_RET
VMEM_LIMIT = 56 * 1024 * 1024

TM_PROJ = 512
T_FOX = 512
T_RET = 256
RET_BLK = 512
S5_T = 256
NEG_BIG = -1e30
FOX_SKIP_LOG2 = 48.0
LOG2E = math.log2(math.e)
Q_SCALE = LOG2E / math.sqrt(HEAD_DIM)


def _cparams(sem, flags=None):
    return pltpu.CompilerParams(dimension_semantics=sem, vmem_limit_bytes=VMEM_LIMIT, flags=flags)


def _rms_norm(x, g):
    ms = jnp.mean(x * x, axis=-1, keepdims=True)
    return x * lax.rsqrt(ms + EPS) * g


def _norm_kernel(x_ref, g_ref, h_ref):
    h_ref[...] = _rms_norm(x_ref[...], g_ref[...]).astype(h_ref.dtype)


def _first_norm(x2, g):
    m = x2.shape[0]
    row = lambda i: (i, 0)
    return pl.pallas_call(
        _norm_kernel,
        grid=(m // TM_PROJ,),
        in_specs=[pl.BlockSpec((TM_PROJ, D_MODEL), row), pl.BlockSpec((1, D_MODEL), lambda i: (0, 0))],
        out_specs=pl.BlockSpec((TM_PROJ, D_MODEL), row),
        out_shape=jax.ShapeDtypeStruct((m, D_MODEL), BF16),
        compiler_params=_cparams(("parallel",)),
        name="first_norm",
    )(x2, g)


def _inproj_kernel(h_ref, wa_ref, wb_ref, wc_ref, oa_ref, su_ref, ob_ref, fl_ref):
    h = h_ref[...]
    qkv = jnp.dot(h, wa_ref[...], preferred_element_type=F32)
    oa_ref[:, :D_FOX] = (qkv[:, :D_FOX] * Q_SCALE).astype(oa_ref.dtype)
    oa_ref[:, D_FOX:] = qkv[:, D_FOX:].astype(oa_ref.dtype)
    rest = jnp.dot(h, wb_ref[...], preferred_element_type=F32)
    su_ref[...] = rest[:, :D_S5]
    ob_ref[...] = rest[:, D_S5:].astype(ob_ref.dtype)
    fl_ref[...] = jnp.dot(h, wc_ref[...], preferred_element_type=F32)


def _inproj(h, wa, wb, wc):
    m = h.shape[0]
    row = lambda i: (i, 0)
    fixed = lambda i: (0, 0)
    return pl.pallas_call(
        _inproj_kernel,
        grid=(m // TM_PROJ,),
        in_specs=[
            pl.BlockSpec((TM_PROJ, D_MODEL), row),
            pl.BlockSpec((D_MODEL, N_QKV), fixed),
            pl.BlockSpec((D_MODEL, N_REST), fixed),
            pl.BlockSpec((D_MODEL, LANES), fixed),
        ],
        out_specs=[
            pl.BlockSpec((TM_PROJ, N_QKV), row),
            pl.BlockSpec((TM_PROJ, D_S5), row),
            pl.BlockSpec((TM_PROJ, N_REST - D_S5), row),
            pl.BlockSpec((TM_PROJ, LANES), row),
        ],
        out_shape=[
            jax.ShapeDtypeStruct((m, N_QKV), BF16),
            jax.ShapeDtypeStruct((m, D_S5), F32),
            jax.ShapeDtypeStruct((m, N_REST - D_S5), BF16),
            jax.ShapeDtypeStruct((m, LANES), F32),
        ],
        compiler_params=_cparams(("parallel",)),
        name="inproj",
    )(h, wa, wb, wc)


def _fox_kernel(q_ref, k_ref, v_ref, fl_ref, bf_ref, o_ref, kaug_ref, vaug_ref, c_ref, clast_ref,
                knorm_ref):
    t = T_FOX
    i = pl.program_id(1)
    seq = k_ref.shape[1]
    lane = lax.broadcasted_iota(jnp.int32, (t, LANES), 1)
    lane1 = lane[0:1, :]
    own = (lane < HEAD_DIM, lane >= HEAD_DIM)
    n_pairs = D_FOX // LANES
    heads = [(g, e) for g in range(n_pairs) for e in range(2)]
    cols = lambda g: slice(g * LANES, (g + 1) * LANES)
    x0 = lambda e: HEAD_DIM if e == 0 else 0
    shift = lambda g, e: (x0(e) - (2 * g + e)) % LANES

    def max_sq_norms(ref, rows):
        out = jnp.zeros((1, LANES), F32)
        for g in range(n_pairs):
            x = ref[0, rows, cols(g)].astype(F32)
            sq = x * x
            for e in range(2):
                norms = jnp.sum(jnp.where(own[e], sq, 0.0), axis=-1, keepdims=True)
                out = jnp.where(lane1 == 2 * g + e, jnp.max(norms, axis=0, keepdims=True), out)
        return out

    @pl.when(i == 0)
    def _():
        r = lax.broadcasted_iota(jnp.int32, (t, t), 0)
        s = lax.broadcasted_iota(jnp.int32, (t, t), 1)
        tri = jnp.where(s <= r, 1.0, 0.0).astype(BF16)

        def build(j, carry):
            rows = pl.ds(pl.multiple_of(j * t, t), t)
            z = fl_ref[0, rows, :] + bf_ref[...]
            logf = jnp.minimum(z, 0.0) - jnp.log1p(jnp.exp(-jnp.abs(z)))
            top = logf.astype(BF16)
            rem = logf - top.astype(F32)
            middle = rem.astype(BF16)
            bottom = (rem - middle.astype(F32)).astype(BF16)
            c = carry + sum(jnp.dot(tri, piece, preferred_element_type=F32)
                            for piece in (bottom, middle, top))
            c_ref[rows, :] = c
            clast_ref[pl.ds(j, 1), :] = c[t - 1:t, :]
            knorm_ref[pl.ds(j, 1), :] = max_sq_norms(k_ref, rows)
            b = -LOG2E * c
            hi = b.astype(BF16).astype(F32)
            mid = (b - hi).astype(BF16).astype(F32)
            lo = b - hi - mid
            pieces = jnp.where(lane < 8, hi, jnp.where(lane < 16, pltpu.roll(mid, 8, 1), jnp.where(
                lane < 24, pltpu.roll(lo, 16, 1), jnp.where(lane < 32, 1.0, 0.0))))
            for n, (g, e) in enumerate(heads):
                extra = pltpu.roll(pieces, shift(g, e), 1).astype(BF16)
                kaug_ref[n, rows, :] = jnp.where(own[e], k_ref[0, rows, cols(g)], extra)
                v = v_ref[0, rows, cols(g)]
                vaug_ref[n, rows, :] = jnp.where(own[e], v, jnp.ones_like(v))
            return c[t - 1:t, :]

        lax.fori_loop(0, seq // t, build, jnp.zeros((1, LANES), F32))

    cq0 = LOG2E * c_ref[pl.ds(pl.multiple_of(i * t, t), SUBLANES), :][0:1, :]
    qa = []
    for g, e in heads:
        ones = (lane1 == x0(e)) | (lane1 == x0(e) + 8) | (lane1 == x0(e) + 16)
        extra = jnp.where(ones, 1.0, jnp.where(
            lane1 == x0(e) + 24, pltpu.roll(cq0, (shift(g, e) + 24) % LANES, 1), 0.0))
        qa.append(jnp.where(own[e], q_ref[0, :, cols(g)], extra.astype(BF16)))

    def tile(j, carry, masked):
        m, acc = carry
        rows = pl.ds(pl.multiple_of(j * t, t), t)
        new_m, new_acc = [], []
        for n in range(len(heads)):
            s = lax.dot_general(qa[n], kaug_ref[n, rows, :], (((1,), (1,)), ((), ())),
                                preferred_element_type=F32)
            if masked:
                r = lax.broadcasted_iota(jnp.int32, (t, t), 0)
                cidx = lax.broadcasted_iota(jnp.int32, (t, t), 1)
                s = jnp.where(cidx <= r, s, NEG_BIG)
            m_new = jnp.maximum(m[n], jnp.max(s, axis=-1, keepdims=True))
            alpha = jnp.exp2(m[n] - m_new)
            p = jnp.exp2(s - m_new).astype(BF16)
            new_m.append(m_new)
            new_acc.append(acc[n] * alpha + jnp.dot(p, vaug_ref[n, rows, :],
                                                    preferred_element_type=F32))
        return tuple(new_m), tuple(new_acc)

    n_tiles = seq // t
    tile_id = lax.broadcasted_iota(jnp.int32, (n_tiles, LANES), 0)
    qk = jnp.sqrt(max_sq_norms(q_ref, slice(None)) * knorm_ref[...])
    qk_diag = jnp.sum(jnp.where(tile_id == i, qk, 0.0), axis=0, keepdims=True)
    upper = qk + (cq0 - LOG2E * clast_ref[...])
    needed = (upper >= -qk_diag - FOX_SKIP_LOG2) & (lane[:n_tiles] < FOX_HEADS)
    any_needed = jnp.max(jnp.where(needed, 1.0, 0.0), axis=-1, keepdims=True)
    first_needed = jnp.min(jnp.where((any_needed > 0.0) | (tile_id[:, 0:1] >= i),
                                     tile_id[:, 0:1], n_tiles))
    init = ((jnp.full((t, 1), NEG_BIG, F32),) * len(heads),
            (jnp.zeros((t, LANES), F32),) * len(heads))
    carry = lax.fori_loop(jnp.minimum(first_needed, i), i, lambda j, c: tile(j, c, False), init)
    _, acc = tile(i, carry, True)
    outs = [a / pltpu.roll(a, HEAD_DIM, 1) for a in acc]
    for g in range(n_pairs):
        o_ref[0, :, cols(g)] = jnp.where(own[0], outs[2 * g], outs[2 * g + 1]).astype(o_ref.dtype)


def _fox_attention(qkv, flog, bf_pad):
    batch, seq, _ = qkv.shape
    n_heads = D_FOX // HEAD_DIM
    once = pl.Buffered(1)
    return pl.pallas_call(
        _fox_kernel,
        grid=(batch, seq // T_FOX),
        in_specs=[
            pl.BlockSpec((1, T_FOX, D_FOX), lambda b, i: (b, i, 0)),
            pl.BlockSpec((1, seq, D_FOX), lambda b, i: (b, 0, 1), pipeline_mode=once),
            pl.BlockSpec((1, seq, D_FOX), lambda b, i: (b, 0, 2), pipeline_mode=once),
            pl.BlockSpec((1, seq, LANES), lambda b, i: (b, 0, 0), pipeline_mode=once),
            pl.BlockSpec((1, LANES), lambda b, i: (0, 0)),
        ],
        out_specs=pl.BlockSpec((1, T_FOX, D_FOX), lambda b, i: (b, i, 0)),
        out_shape=jax.ShapeDtypeStruct((batch, seq, D_FOX), BF16),
        scratch_shapes=[pltpu.VMEM((n_heads, seq, LANES), BF16),
                        pltpu.VMEM((n_heads, seq, LANES), BF16),
                        pltpu.VMEM((seq, LANES), F32),
                        pltpu.VMEM((seq // T_FOX, LANES), F32),
                        pltpu.VMEM((seq // T_FOX, LANES), F32)],
        compiler_params=_cparams(("parallel", "arbitrary")),
        name="fox_attention",
    )(qkv, qkv, qkv, flog, bf_pad)


def _ret_kernel(q_ref, k_ref, v_ref, cos_ref, sin_ref, lg_ref, gn_ref, o_ref,
                state_ref, dmat_ref, wq_ref, wk_ref):
    t = T_RET
    n_pairs = D_RET // LANES
    scale = 1.0 / math.sqrt(HEAD_DIM)
    lane = lax.broadcasted_iota(jnp.int32, (t, LANES), 1)
    first = lane < HEAD_DIM
    low_half = (lane % HEAD_DIM) < (HEAD_DIM // 2)
    cols = lambda p: slice(p * LANES, (p + 1) * LANES)

    @pl.when((pl.program_id(0) == 0) & (pl.program_id(1) == 0))
    def _():
        pos = lax.broadcasted_iota(jnp.int32, (t, 1), 0).astype(F32)
        r = lax.broadcasted_iota(jnp.int32, (t, t), 0)
        s = lax.broadcasted_iota(jnp.int32, (t, t), 1)
        dist = jnp.abs(r - s).astype(F32)
        visible = (s // CHUNK) <= (r // CHUNK)
        for h in range(RET_HEADS):
            lg = lg_ref[:, h * HEAD_DIM:h * HEAD_DIM + 1]
            dmat_ref[h] = jnp.where(visible, jnp.exp(lg * dist) * scale, 0.0)
        wq_ref[...] = jnp.exp(lg_ref[...] * (pos + 1.0)) * scale
        wk_ref[...] = jnp.exp(lg_ref[...] * (t - 1.0 - pos))

    @pl.when(pl.program_id(1) == 0)
    def _():
        state_ref[...] = jnp.zeros_like(state_ref)

    rb = lax.broadcasted_iota(jnp.int32, (LANES, LANES), 0)
    cb = lax.broadcasted_iota(jnp.int32, (LANES, LANES), 1)
    same_head = (rb // HEAD_DIM) == (cb // HEAD_DIM)

    def head_mean(x):
        s0 = jnp.sum(jnp.where(first, x, 0.0), axis=-1, keepdims=True)
        s1 = jnp.sum(jnp.where(first, 0.0, x), axis=-1, keepdims=True)
        return jnp.where(first, s0, s1) * (1.0 / HEAD_DIM)

    for sub in range(RET_BLK // t):
        rows = slice(sub * t, (sub + 1) * t)
        cos = cos_ref[rows, :]
        sin = sin_ref[rows, :]

        def rotary(x):
            swapped = jnp.where(low_half, pltpu.roll(x, LANES - HEAD_DIM // 2, 1),
                                pltpu.roll(x, HEAD_DIM // 2, 1))
            return x * cos + swapped * sin

        for p in range(n_pairs):
            q = rotary(q_ref[0, rows, cols(p)].astype(F32))
            k = rotary(k_ref[0, rows, cols(p)].astype(F32))
            vb = v_ref[0, rows, cols(p)]
            kb = k.astype(BF16)
            zero = jnp.zeros_like(q)
            inner = []
            for e in range(2):
                qe = jnp.where(first, q, zero) if e == 0 else jnp.where(first, zero, q)
                sc = lax.dot_general(qe.astype(BF16), kb, (((1,), (1,)), ((), ())),
                                     preferred_element_type=F32) * dmat_ref[2 * p + e]
                inner.append(jnp.dot(sc.astype(BF16), vb, preferred_element_type=F32))
            state = state_ref[p]
            o = jnp.where(first, inner[0], inner[1]) + jnp.dot(
                (q * wq_ref[:, cols(p)]).astype(BF16), state.astype(BF16),
                preferred_element_type=F32)
            upd = lax.dot_general((k * wk_ref[:, cols(p)]).astype(BF16), vb,
                                  (((0,), (0,)), ((), ())), preferred_element_type=F32)
            decay = jnp.exp(lg_ref[:, cols(p)] * float(t))
            state_ref[p] = jnp.where(same_head, state * decay + upd, 0.0)
            d = o - head_mean(o)
            var = head_mean(d * d)
            o_ref[0, rows, cols(p)] = (d * lax.rsqrt(var + EPS) * gn_ref[:, cols(p)]).astype(
                o_ref.dtype)


def _retention(rest, cos_t, sin_t, lg, gn_w):
    batch, seq, _ = rest.shape
    n_pairs = D_RET // LANES
    tok = lambda blk: pl.BlockSpec((1, RET_BLK, D_RET), lambda b, i, blk=blk: (b, i, blk))
    fixed = pl.BlockSpec((1, D_RET), lambda b, i: (0, 0))
    table = pl.BlockSpec((RET_BLK, LANES), lambda b, i: (i, 0))
    return pl.pallas_call(
        _ret_kernel,
        grid=(batch, seq // RET_BLK),
        in_specs=[tok(4), tok(5), tok(6), table, table, fixed, fixed],
        out_specs=pl.BlockSpec((1, RET_BLK, D_RET), lambda b, i: (b, i, 0)),
        out_shape=jax.ShapeDtypeStruct((batch, seq, D_RET), BF16),
        scratch_shapes=[
            pltpu.VMEM((n_pairs, LANES, LANES), F32),
            pltpu.VMEM((RET_HEADS, T_RET, T_RET), F32),
            pltpu.VMEM((T_RET, D_RET), F32),
            pltpu.VMEM((T_RET, D_RET), F32),
        ],
        compiler_params=_cparams(("arbitrary", "arbitrary")),
        name="retention",
    )(rest, rest, rest, cos_t, sin_t, lg, gn_w)


def _s5_kernel(u_ref, are_ref, aim_ref, ldt_ref, br_ref, bi_ref, cr_ref, ci_ref, d_ref, wg_ref,
               o_ref, wb_ref, wc_ref, lam_ref, lhs_ref, x_ref, z_ref, state_ref):
    nb = u_ref.shape[0]
    t5 = S5_T
    win = D_S5 // SUBLANES

    @pl.when(pl.program_id(0) == 0)
    def _():
        dt = jnp.exp(ldt_ref[...])
        ar = are_ref[...]
        ai = aim_ref[...]
        mag = jnp.exp(ar * dt)
        lr = mag * jnp.cos(ai * dt)
        li = mag * jnp.sin(ai * dt)
        den = ar * ar + ai * ai
        fr = ((lr - 1.0) * ar + li * ai) / den
        fi = (li * ar - (lr - 1.0) * ai) / den
        lam_ref[0] = lr
        lam_ref[1] = li
        for j in range(SUBLANES):
            rows = slice(win * j, win * (j + 1))
            f_r = fr[j:j + 1, :]
            f_i = fi[j:j + 1, :]
            b_r = br_ref[rows, :]
            b_i = bi_ref[rows, :]
            wb_ref[rows, :LANES] = (f_r * b_r - f_i * b_i).astype(BF16)
            wb_ref[rows, LANES:] = (f_r * b_i + f_i * b_r).astype(BF16)
        wc_ref[:LANES, :] = cr_ref[...].astype(BF16)
        wc_ref[LANES:, :] = (-ci_ref[...]).astype(BF16)
        state_ref[...] = jnp.zeros_like(state_ref)

    sub = lax.broadcasted_iota(jnp.int32, (2 * SUBLANES, D_S5), 0)
    keep = sub % SUBLANES == lax.broadcasted_iota(jnp.int32, (2 * SUBLANES, D_S5), 1) // win
    first = sub < SUBLANES

    def expand(tt, _):
        for b in range(nb):
            tile = u_ref[b, pl.ds(pl.multiple_of(tt * SUBLANES, SUBLANES), SUBLANES), :]
            for s in range(0, SUBLANES, 2):
                two = jnp.where(first,
                                jnp.broadcast_to(tile[s:s + 1, :], (2 * SUBLANES, D_S5)),
                                jnp.broadcast_to(tile[s + 1:s + 2, :], (2 * SUBLANES, D_S5)))
                dst = pl.ds(pl.multiple_of((tt * SUBLANES + s) * SUBLANES, 2 * SUBLANES),
                            2 * SUBLANES)
                lhs_ref[b, dst, :] = jnp.where(keep, two, 0.0).astype(BF16)
        return 0

    lax.fori_loop(0, t5 // SUBLANES, expand, 0)
    for b in range(nb):
        x_ref[b] = jnp.dot(lhs_ref[b], wb_ref[...], preferred_element_type=F32)

    lr = lam_ref[0]
    li = lam_ref[1]

    def scan(t, carry):
        rows = pl.ds(pl.multiple_of(t * SUBLANES, SUBLANES), SUBLANES)
        new = []
        for b in range(nb):
            xr, xi = carry[b]
            nr = lr * xr - li * xi + x_ref[b, rows, :LANES]
            ni = lr * xi + li * xr + x_ref[b, rows, LANES:]
            x_ref[b, rows, :LANES] = nr
            x_ref[b, rows, LANES:] = ni
            new.append((nr, ni))
        return tuple(new)

    init = tuple((state_ref[b, :, :LANES], state_ref[b, :, LANES:]) for b in range(nb))
    final = lax.fori_loop(0, t5, scan, init, unroll=2)
    for b in range(nb):
        state_ref[b, :, :LANES] = final[b][0]
        state_ref[b, :, LANES:] = final[b][1]

    window = lax.broadcasted_iota(jnp.int32, (t5, LANES), 1) // win
    for b in range(nb):
        z = jnp.dot(x_ref[b].astype(BF16), wc_ref[...], preferred_element_type=F32)
        z_ref[b, 0] = z[:, :LANES]
        z_ref[b, 1] = z[:, LANES:]
    for b in range(nb):
        halves = []
        for hh in range(2):
            per = SUBLANES // 2
            y = z_ref[b, hh, pl.ds(per * hh + per - 1, t5, stride=SUBLANES), :]
            for jj in range(per - 2, -1, -1):
                y = jnp.where(window == jj,
                              z_ref[b, hh, pl.ds(per * hh + jj, t5, stride=SUBLANES), :], y)
            halves.append(y)
        u = u_ref[b]
        y = jax.nn.gelu(jnp.concatenate(halves, axis=1) + d_ref[...] * u)
        gate = jnp.dot(y.astype(BF16), wg_ref[...], preferred_element_type=F32)
        o_ref[b] = (y * jax.nn.sigmoid(gate)).astype(o_ref.dtype)


def _s5(su, prm):
    batch, seq, _ = su.shape
    fixed = lambda i: (0, 0)
    lam = pl.BlockSpec((SUBLANES, LANES), fixed)
    rows8 = SUBLANES * S5_T
    return pl.pallas_call(
        _s5_kernel,
        grid=(seq // S5_T,),
        in_specs=[
            pl.BlockSpec((batch, S5_T, D_S5), lambda i: (0, i, 0)),
            lam, lam, lam,
            pl.BlockSpec((D_S5, LANES), fixed), pl.BlockSpec((D_S5, LANES), fixed),
            pl.BlockSpec((LANES, D_S5), fixed), pl.BlockSpec((LANES, D_S5), fixed),
            pl.BlockSpec((1, D_S5), fixed),
            pl.BlockSpec((D_S5, D_S5), fixed),
        ],
        out_specs=pl.BlockSpec((batch, S5_T, D_S5), lambda i: (0, i, 0)),
        out_shape=jax.ShapeDtypeStruct((batch, seq, D_S5), BF16),
        scratch_shapes=[
            pltpu.VMEM((D_S5, 2 * LANES), BF16),
            pltpu.VMEM((2 * LANES, D_S5), BF16),
            pltpu.VMEM((2, SUBLANES, LANES), F32),
            pltpu.VMEM((batch, rows8, D_S5), BF16),
            pltpu.VMEM((batch, rows8, 2 * LANES), F32),
            pltpu.VMEM((batch, 2, rows8, LANES), F32),
            pltpu.VMEM((batch, SUBLANES, 2 * LANES), F32),
        ],
        compiler_params=_cparams(("arbitrary",)),
        name="s5",
    )(su, prm["a_re"], prm["a_im"], prm["log_dt"], prm["b_re"], prm["b_im"],
      prm["c_re"], prm["c_im"], prm["d"], prm["w_glu"])


def _outproj_kernel(yf_ref, ys_ref, yr_ref, gate_ref, x_ref, w_ref, g_ref, *o_refs, final):
    g = gate_ref[...].astype(F32)
    g = g * jax.nn.sigmoid(g)
    acc = x_ref[...]
    off = 0
    for y_ref in (yf_ref, ys_ref, yr_ref):
        width = y_ref.shape[-1]
        y = (y_ref[...] * g[:, off:off + width]).astype(BF16)
        acc = acc + jnp.dot(y, w_ref[off:off + width, :], preferred_element_type=F32)
        off += width
    normed = _rms_norm(acc, g_ref[...])
    if final:
        o_refs[0][...] = normed
    else:
        o_refs[0][...] = acc
        o_refs[1][...] = normed.astype(o_refs[1].dtype)


def _outproj(y_fox, y_s5, y_ret, rest, x2, w_out, norm_g, final):
    m = x2.shape[0]
    row = lambda i: (i, 0)
    fixed = lambda i: (0, 0)
    tile = pl.BlockSpec((TM_PROJ, D_MODEL), row)
    x_shape = jax.ShapeDtypeStruct((m, D_MODEL), F32)
    return pl.pallas_call(
        functools.partial(_outproj_kernel, final=final),
        grid=(m // TM_PROJ,),
        in_specs=[
            pl.BlockSpec((TM_PROJ, D_FOX), row),
            pl.BlockSpec((TM_PROJ, D_S5), row),
            pl.BlockSpec((TM_PROJ, D_RET), row),
            tile,
            tile,
            pl.BlockSpec((D_MODEL, D_MODEL), fixed),
            pl.BlockSpec((1, D_MODEL), fixed),
        ],
        out_specs=tile if final else [tile, tile],
        out_shape=x_shape if final else [x_shape, jax.ShapeDtypeStruct((m, D_MODEL), BF16)],
        compiler_params=_cparams(("parallel",)),
        name="outproj_final" if final else "outproj",
    )(y_fox, y_s5, y_ret, rest, x2, w_out, norm_g)


def _rotary_tables(seq):
    half = HEAD_DIM // 2
    freqs = ROPE_BASE ** (-jnp.arange(half, dtype=F32) / half)
    ang = jnp.arange(seq, dtype=F32)[:, None] * freqs[None, :]
    cos = jnp.cos(ang)
    sin = jnp.sin(ang)
    cos_t = jnp.tile(cos, (1, LANES // half))
    sin_t = jnp.tile(jnp.concatenate([-sin, sin], axis=-1), (1, LANES // HEAD_DIM))
    return cos_t, sin_t


def _s5_params(a_re, a_im, b_re, b_im, c_re, c_im, d, log_dt, w_glu):
    pair = jnp.eye(2, dtype=F32)
    half = S5_GROUPS // 2
    pack_b = lambda b: jnp.einsum("jnph,kn->jkhnp", b.reshape(half, 2, S5_STATE, S5_GROUP_CH),
                                  pair).reshape(D_S5, LANES)
    pack_c = lambda c: jnp.einsum("jnhp,kn->npjkh", c.reshape(half, 2, S5_GROUP_CH, S5_STATE),
                                  pair).reshape(LANES, D_S5)
    tile = lambda v: v.reshape(SUBLANES, LANES)
    return {
        "a_re": tile(a_re), "a_im": tile(a_im), "log_dt": tile(jnp.repeat(log_dt, S5_STATE)),
        "b_re": pack_b(b_re), "b_im": pack_b(b_im), "c_re": pack_c(c_re), "c_im": pack_c(c_im),
        "d": d.reshape(1, D_S5), "w_glu": w_glu.astype(BF16),
    }


def kernel(x, norm_w, w_in, fox_b_f, s5_a_re, s5_a_im, s5_b_re, s5_b_im, s5_c_re, s5_c_im,
           s5_d, s5_log_dt, s5_w_glu, ret_gn_w, w_out, final_norm_w):
    batch, seq, _ = x.shape
    depth = w_in.shape[0]
    m = batch * seq
    cos_t, sin_t = _rotary_tables(seq)
    log_gamma = jnp.log1p(-(2.0 ** (-5.0 - jnp.arange(RET_HEADS, dtype=F32))))
    lg = jnp.repeat(log_gamma, HEAD_DIM).reshape(1, D_RET)
    norm_g = jnp.concatenate([norm_w, final_norm_w[None]], axis=0).reshape(depth + 1, 1, D_MODEL)
    flog_lo = N_QKV
    flog_hi = N_QKV + FOX_HEADS

    x2 = x.reshape(m, D_MODEL)
    h = _first_norm(x2, norm_g[0])
    for l in range(depth):
        wa = w_in[l, :, :flog_lo].astype(BF16)
        wc = jnp.pad(w_in[l, :, flog_lo:flog_hi], ((0, 0), (0, LANES - FOX_HEADS))).astype(BF16)
        gate_lo = flog_hi + D_S5 + 3 * D_RET
        wb = jnp.concatenate([w_in[l, :, flog_hi:flog_hi + D_S5], w_in[l, :, gate_lo:],
                              w_in[l, :, flog_hi + D_S5:gate_lo]], axis=1).astype(BF16)
        bf_pad = jnp.pad(fox_b_f[l], (0, LANES - FOX_HEADS)).reshape(1, LANES)

        qkv, su, rest, flog = _inproj(h, wa, wb, wc)
        qkv = qkv.reshape(batch, seq, N_QKV)
        rest3 = rest.reshape(batch, seq, N_REST - D_S5)
        y_fox = _fox_attention(qkv, flog.reshape(batch, seq, LANES), bf_pad)
        y_s5 = _s5(su.reshape(batch, seq, D_S5),
                   _s5_params(s5_a_re[l], s5_a_im[l], s5_b_re[l], s5_b_im[l], s5_c_re[l],
                              s5_c_im[l], s5_d[l], s5_log_dt[l], s5_w_glu[l]))
        y_ret = _retention(rest3, cos_t, sin_t, lg, ret_gn_w[l].reshape(1, D_RET))
        out = _outproj(y_fox.reshape(m, D_FOX), y_s5.reshape(m, D_S5), y_ret.reshape(m, D_RET),
                       rest, x2, w_out[l].astype(BF16), norm_g[l + 1], final=(l == depth - 1))
        if l < depth - 1:
            x2, h = out
    return out.reshape(batch, seq, D_MODEL)
```

```python
import functools
import math

import jax
import jax.numpy as jnp
from jax import lax
from jax.experimental import pallas as pl
from jax.experimental.pallas import tpu as pltpu

F32 = jnp.float32
BF16 = jnp.bfloat16

D_MODEL = 1024
HEAD_DIM = 64
CHUNK = 64
D_FOX = 512
FOX_HEADS = 8
D_S5 = 256
S5_GROUPS = 16
S5_GROUP_CH = 16
S5_STATE = 64
D_RET = 256
RET_HEADS = 4
ROPE_BASE = 10000.0
EPS = 1e-6

LANES = 128
SUBLANES = 8
N_QKV = 3 * D_FOX
N_REST = D_S5 + D_MODEL + 3 * D_RET
VMEM_LIMIT = 56 * 1024 * 1024

TM_PROJ = 512
T_FOX = 512
T_RET = 256
RET_BLK = 512
S5_T = 256
NEG_BIG = -1e30
FOX_SKIP_LOG2 = 40.0
LOG2E = math.log2(math.e)
Q_SCALE = LOG2E / math.sqrt(HEAD_DIM)


def _cparams(sem, flags=None):
    return pltpu.CompilerParams(dimension_semantics=sem, vmem_limit_bytes=VMEM_LIMIT, flags=flags)


def _rms_norm(x, g):
    ms = jnp.mean(x * x, axis=-1, keepdims=True)
    return x * lax.rsqrt(ms + EPS) * g


def _norm_kernel(x_ref, g_ref, h_ref):
    h_ref[...] = _rms_norm(x_ref[...], g_ref[...]).astype(h_ref.dtype)


def _first_norm(x2, g):
    m = x2.shape[0]
    row = lambda i: (i, 0)
    return pl.pallas_call(
        _norm_kernel,
        grid=(m // TM_PROJ,),
        in_specs=[pl.BlockSpec((TM_PROJ, D_MODEL), row), pl.BlockSpec((1, D_MODEL), lambda i: (0, 0))],
        out_specs=pl.BlockSpec((TM_PROJ, D_MODEL), row),
        out_shape=jax.ShapeDtypeStruct((m, D_MODEL), BF16),
        compiler_params=_cparams(("parallel",)),
        name="first_norm",
    )(x2, g)


def _inproj_kernel(h_ref, wa_ref, wb_ref, wc_ref, oa_ref, su_ref, ob_ref, fl_ref):
    h = h_ref[...]
    qkv = jnp.dot(h, wa_ref[...], preferred_element_type=F32)
    oa_ref[:, :D_FOX] = (qkv[:, :D_FOX] * Q_SCALE).astype(oa_ref.dtype)
    oa_ref[:, D_FOX:] = qkv[:, D_FOX:].astype(oa_ref.dtype)
    rest = jnp.dot(h, wb_ref[...], preferred_element_type=F32)
    su_ref[...] = rest[:, :D_S5]
    ob_ref[...] = rest[:, D_S5:].astype(ob_ref.dtype)
    fl_ref[...] = jnp.dot(h, wc_ref[...], preferred_element_type=F32)


def _inproj(h, wa, wb, wc):
    m = h.shape[0]
    row = lambda i: (i, 0)
    fixed = lambda i: (0, 0)
    return pl.pallas_call(
        _inproj_kernel,
        grid=(m // TM_PROJ,),
        in_specs=[
            pl.BlockSpec((TM_PROJ, D_MODEL), row),
            pl.BlockSpec((D_MODEL, N_QKV), fixed),
            pl.BlockSpec((D_MODEL, N_REST), fixed),
            pl.BlockSpec((D_MODEL, LANES), fixed),
        ],
        out_specs=[
            pl.BlockSpec((TM_PROJ, N_QKV), row),
            pl.BlockSpec((TM_PROJ, D_S5), row),
            pl.BlockSpec((TM_PROJ, N_REST - D_S5), row),
            pl.BlockSpec((TM_PROJ, LANES), row),
        ],
        out_shape=[
            jax.ShapeDtypeStruct((m, N_QKV), BF16),
            jax.ShapeDtypeStruct((m, D_S5), F32),
            jax.ShapeDtypeStruct((m, N_REST - D_S5), BF16),
            jax.ShapeDtypeStruct((m, LANES), F32),
        ],
        compiler_params=_cparams(("parallel",)),
        name="inproj",
    )(h, wa, wb, wc)


def _fox_kernel(q_ref, k_ref, v_ref, fl_ref, bf_ref, o_ref, kaug_ref, vaug_ref, c_ref, clast_ref,
                knorm_ref):
    t = T_FOX
    i = pl.program_id(1)
    seq = k_ref.shape[1]
    lane = lax.broadcasted_iota(jnp.int32, (t, LANES), 1)
    lane1 = lane[0:1, :]
    own = (lane < HEAD_DIM, lane >= HEAD_DIM)
    n_pairs = D_FOX // LANES
    heads = [(g, e) for g in range(n_pairs) for e in range(2)]
    cols = lambda g: slice(g * LANES, (g + 1) * LANES)
    x0 = lambda e: HEAD_DIM if e == 0 else 0
    shift = lambda g, e: (x0(e) - (2 * g + e)) % LANES

    def max_sq_norms(ref, rows):
        out = jnp.zeros((1, LANES), F32)
        for g in range(n_pairs):
            x = ref[0, rows, cols(g)].astype(F32)
            sq = x * x
            for e in range(2):
                norms = jnp.sum(jnp.where(own[e], sq, 0.0), axis=-1, keepdims=True)
                out = jnp.where(lane1 == 2 * g + e, jnp.max(norms, axis=0, keepdims=True), out)
        return out

    @pl.when(i == 0)
    def _():
        r = lax.broadcasted_iota(jnp.int32, (t, t), 0)
        s = lax.broadcasted_iota(jnp.int32, (t, t), 1)
        tri = jnp.where(s <= r, 1.0, 0.0).astype(BF16)

        def build(j, carried):
            carry, k_max = carried
            rows = pl.ds(pl.multiple_of(j * t, t), t)
            z = fl_ref[0, rows, :] + bf_ref[...]
            logf = jnp.minimum(z, 0.0) - jnp.log1p(jnp.exp(-jnp.abs(z)))
            top = logf.astype(BF16)
            rem = logf - top.astype(F32)
            middle = rem.astype(BF16)
            bottom = (rem - middle.astype(F32)).astype(BF16)
            c = carry + sum(jnp.dot(tri, piece, preferred_element_type=F32)
                            for piece in (bottom, middle, top))
            c_ref[rows, :] = c
            clast_ref[pl.ds(j, 1), :] = c[t - 1:t, :]
            k_max = jnp.maximum(k_max, max_sq_norms(k_ref, rows))
            knorm_ref[pl.ds(j, 1), :] = k_max
            b = -LOG2E * c
            hi = b.astype(BF16).astype(F32)
            mid = (b - hi).astype(BF16).astype(F32)
            lo = b - hi - mid
            pieces = jnp.where(lane < 8, hi, jnp.where(lane < 16, pltpu.roll(mid, 8, 1), jnp.where(
                lane < 24, pltpu.roll(lo, 16, 1), jnp.where(lane < 32, 1.0, 0.0))))
            for n, (g, e) in enumerate(heads):
                extra = pltpu.roll(pieces, shift(g, e), 1).astype(BF16)
                kaug_ref[n, rows, :] = jnp.where(own[e], k_ref[0, rows, cols(g)], extra)
                v = v_ref[0, rows, cols(g)]
                vaug_ref[n, rows, :] = jnp.where(own[e], v, jnp.ones_like(v))
            return c[t - 1:t, :], k_max

        zero_row = jnp.zeros((1, LANES), F32)
        lax.fori_loop(0, seq // t, build, (zero_row, zero_row))

    cq0 = LOG2E * c_ref[pl.ds(pl.multiple_of(i * t, t), SUBLANES), :][0:1, :]
    qa = []
    for g, e in heads:
        ones = (lane1 == x0(e)) | (lane1 == x0(e) + 8) | (lane1 == x0(e) + 16)
        extra = jnp.where(ones, 1.0, jnp.where(
            lane1 == x0(e) + 24, pltpu.roll(cq0, (shift(g, e) + 24) % LANES, 1), 0.0))
        qa.append(jnp.where(own[e], q_ref[0, :, cols(g)], extra.astype(BF16)))

    def tile(j, carry, masked):
        m, acc = carry
        rows = pl.ds(pl.multiple_of(j * t, t), t)
        new_m, new_acc = [], []
        for n in range(len(heads)):
            s = lax.dot_general(qa[n], kaug_ref[n, rows, :], (((1,), (1,)), ((), ())),
                                preferred_element_type=F32)
            if masked:
                r = lax.broadcasted_iota(jnp.int32, (t, t), 0)
                cidx = lax.broadcasted_iota(jnp.int32, (t, t), 1)
                s = jnp.where(cidx <= r, s, NEG_BIG)
            m_new = jnp.maximum(m[n], jnp.max(s, axis=-1, keepdims=True))
            alpha = jnp.exp2(m[n] - m_new)
            p = jnp.exp2(s - m_new).astype(BF16)
            new_m.append(m_new)
            new_acc.append(acc[n] * alpha + jnp.dot(p, vaug_ref[n, rows, :],
                                                    preferred_element_type=F32))
        return tuple(new_m), tuple(new_acc)

    n_tiles = seq // t
    tile_id = lax.broadcasted_iota(jnp.int32, (n_tiles, LANES), 0)
    cq0_used = cq0.astype(BF16).astype(F32)
    upper = (jnp.sqrt(max_sq_norms(q_ref, slice(None)) * knorm_ref[...])
             + (cq0_used - LOG2E * clast_ref[...]))

    def needed(j, m):
        row_min = jnp.zeros((1, LANES), F32)
        for n in range(len(heads)):
            row_min = jnp.where(lane1 == n, jnp.min(m[n], axis=0, keepdims=True), row_min)
        upper_j = jnp.sum(jnp.where(tile_id == j, upper, 0.0), axis=0, keepdims=True)
        hit = (upper_j >= row_min - FOX_SKIP_LOG2) & (lane1 < FOX_HEADS)
        return (j >= 0) & (jnp.max(jnp.where(hit, 1.0, 0.0)) > 0.0)

    init = ((jnp.full((t, 1), NEG_BIG, F32),) * len(heads),
            (jnp.zeros((t, LANES), F32),) * len(heads))
    m, acc = tile(i, init, True)

    def walk(state):
        j, _, m, acc = state
        m, acc = tile(j, (m, acc), False)
        return j - 1, needed(j - 1, m), m, acc

    _, _, _, acc = lax.while_loop(lambda state: state[1], walk, (i - 1, needed(i - 1, m), m, acc))
    outs = [a / pltpu.roll(a, HEAD_DIM, 1) for a in acc]
    for g in range(n_pairs):
        o_ref[0, :, cols(g)] = jnp.where(own[0], outs[2 * g], outs[2 * g + 1]).astype(o_ref.dtype)


def _fox_attention(qkv, flog, bf_pad):
    batch, seq, _ = qkv.shape
    n_heads = D_FOX // HEAD_DIM
    once = pl.Buffered(1)
    return pl.pallas_call(
        _fox_kernel,
        grid=(batch, seq // T_FOX),
        in_specs=[
            pl.BlockSpec((1, T_FOX, D_FOX), lambda b, i: (b, i, 0)),
            pl.BlockSpec((1, seq, D_FOX), lambda b, i: (b, 0, 1), pipeline_mode=once),
            pl.BlockSpec((1, seq, D_FOX), lambda b, i: (b, 0, 2), pipeline_mode=once),
            pl.BlockSpec((1, seq, LANES), lambda b, i: (b, 0, 0), pipeline_mode=once),
            pl.BlockSpec((1, LANES), lambda b, i: (0, 0)),
        ],
        out_specs=pl.BlockSpec((1, T_FOX, D_FOX), lambda b, i: (b, i, 0)),
        out_shape=jax.ShapeDtypeStruct((batch, seq, D_FOX), BF16),
        scratch_shapes=[pltpu.VMEM((n_heads, seq, LANES), BF16),
                        pltpu.VMEM((n_heads, seq, LANES), BF16),
                        pltpu.VMEM((seq, LANES), F32),
                        pltpu.VMEM((seq // T_FOX, LANES), F32),
                        pltpu.VMEM((seq // T_FOX, LANES), F32)],
        compiler_params=_cparams(("parallel", "arbitrary")),
        name="fox_attention",
    )(qkv, qkv, qkv, flog, bf_pad)


def _ret_kernel(q_ref, k_ref, v_ref, cos_ref, sin_ref, lg_ref, gn_ref, o_ref,
                state_ref, dmat_ref, wq_ref, wk_ref):
    t = T_RET
    n_pairs = D_RET // LANES
    scale = 1.0 / math.sqrt(HEAD_DIM)
    lane = lax.broadcasted_iota(jnp.int32, (t, LANES), 1)
    first = lane < HEAD_DIM
    low_half = (lane % HEAD_DIM) < (HEAD_DIM // 2)
    cols = lambda p: slice(p * LANES, (p + 1) * LANES)

    @pl.when((pl.program_id(0) == 0) & (pl.program_id(1) == 0))
    def _():
        pos = lax.broadcasted_iota(jnp.int32, (t, 1), 0).astype(F32)
        r = lax.broadcasted_iota(jnp.int32, (t, t), 0)
        s = lax.broadcasted_iota(jnp.int32, (t, t), 1)
        dist = jnp.abs(r - s).astype(F32)
        visible = (s // CHUNK) <= (r // CHUNK)
        for h in range(RET_HEADS):
            lg = lg_ref[:, h * HEAD_DIM:h * HEAD_DIM + 1]
            dmat_ref[h] = jnp.where(visible, jnp.exp(lg * dist) * scale, 0.0)
        wq_ref[...] = jnp.exp(lg_ref[...] * (pos + 1.0)) * scale
        wk_ref[...] = jnp.exp(lg_ref[...] * (t - 1.0 - pos))

    @pl.when(pl.program_id(1) == 0)
    def _():
        state_ref[...] = jnp.zeros_like(state_ref)

    rb = lax.broadcasted_iota(jnp.int32, (LANES, LANES), 0)
    cb = lax.broadcasted_iota(jnp.int32, (LANES, LANES), 1)
    same_head = (rb // HEAD_DIM) == (cb // HEAD_DIM)

    def head_mean(x):
        s0 = jnp.sum(jnp.where(first, x, 0.0), axis=-1, keepdims=True)
        s1 = jnp.sum(jnp.where(first, 0.0, x), axis=-1, keepdims=True)
        return jnp.where(first, s0, s1) * (1.0 / HEAD_DIM)

    for sub in range(RET_BLK // t):
        rows = slice(sub * t, (sub + 1) * t)
        cos = cos_ref[rows, :]
        sin = sin_ref[rows, :]

        def rotary(x):
            swapped = jnp.where(low_half, pltpu.roll(x, LANES - HEAD_DIM // 2, 1),
                                pltpu.roll(x, HEAD_DIM // 2, 1))
            return x * cos + swapped * sin

        for p in range(n_pairs):
            q = rotary(q_ref[0, rows, cols(p)].astype(F32))
            k = rotary(k_ref[0, rows, cols(p)].astype(F32))
            vb = v_ref[0, rows, cols(p)]
            kb = k.astype(BF16)
            zero = jnp.zeros_like(q)
            inner = []
            for e in range(2):
                qe = jnp.where(first, q, zero) if e == 0 else jnp.where(first, zero, q)
                sc = lax.dot_general(qe.astype(BF16), kb, (((1,), (1,)), ((), ())),
                                     preferred_element_type=F32) * dmat_ref[2 * p + e]
                inner.append(jnp.dot(sc.astype(BF16), vb, preferred_element_type=F32))
            state = state_ref[p]
            o = jnp.where(first, inner[0], inner[1]) + jnp.dot(
                (q * wq_ref[:, cols(p)]).astype(BF16), state.astype(BF16),
                preferred_element_type=F32)
            upd = lax.dot_general((k * wk_ref[:, cols(p)]).astype(BF16), vb,
                                  (((0,), (0,)), ((), ())), preferred_element_type=F32)
            decay = jnp.exp(lg_ref[:, cols(p)] * float(t))
            state_ref[p] = jnp.where(same_head, state * decay + upd, 0.0)
            d = o - head_mean(o)
            var = head_mean(d * d)
            o_ref[0, rows, cols(p)] = (d * lax.rsqrt(var + EPS) * gn_ref[:, cols(p)]).astype(
                o_ref.dtype)


def _retention(rest, cos_t, sin_t, lg, gn_w):
    batch, seq, _ = rest.shape
    n_pairs = D_RET // LANES
    tok = lambda blk: pl.BlockSpec((1, RET_BLK, D_RET), lambda b, i, blk=blk: (b, i, blk))
    fixed = pl.BlockSpec((1, D_RET), lambda b, i: (0, 0))
    table = pl.BlockSpec((RET_BLK, LANES), lambda b, i: (i, 0))
    return pl.pallas_call(
        _ret_kernel,
        grid=(batch, seq // RET_BLK),
        in_specs=[tok(4), tok(5), tok(6), table, table, fixed, fixed],
        out_specs=pl.BlockSpec((1, RET_BLK, D_RET), lambda b, i: (b, i, 0)),
        out_shape=jax.ShapeDtypeStruct((batch, seq, D_RET), BF16),
        scratch_shapes=[
            pltpu.VMEM((n_pairs, LANES, LANES), F32),
            pltpu.VMEM((RET_HEADS, T_RET, T_RET), F32),
            pltpu.VMEM((T_RET, D_RET), F32),
            pltpu.VMEM((T_RET, D_RET), F32),
        ],
        compiler_params=_cparams(("arbitrary", "arbitrary")),
        name="retention",
    )(rest, rest, rest, cos_t, sin_t, lg, gn_w)


def _s5_kernel(u_ref, are_ref, aim_ref, ldt_ref, br_ref, bi_ref, cr_ref, ci_ref, d_ref, wg_ref,
               o_ref, wb_ref, wc_ref, lam_ref, lhs_ref, x_ref, z_ref, state_ref):
    nb = u_ref.shape[0]
    t5 = S5_T
    win = D_S5 // SUBLANES

    @pl.when(pl.program_id(0) == 0)
    def _():
        dt = jnp.exp(ldt_ref[...])
        ar = are_ref[...]
        ai = aim_ref[...]
        mag = jnp.exp(ar * dt)
        lr = mag * jnp.cos(ai * dt)
        li = mag * jnp.sin(ai * dt)
        den = ar * ar + ai * ai
        fr = ((lr - 1.0) * ar + li * ai) / den
        fi = (li * ar - (lr - 1.0) * ai) / den
        lam_ref[0] = lr
        lam_ref[1] = li
        for j in range(SUBLANES):
            rows = slice(win * j, win * (j + 1))
            f_r = fr[j:j + 1, :]
            f_i = fi[j:j + 1, :]
            b_r = br_ref[rows, :]
            b_i = bi_ref[rows, :]
            wb_ref[rows, :LANES] = (f_r * b_r - f_i * b_i).astype(BF16)
            wb_ref[rows, LANES:] = (f_r * b_i + f_i * b_r).astype(BF16)
        wc_ref[:LANES, :] = cr_ref[...].astype(BF16)
        wc_ref[LANES:, :] = (-ci_ref[...]).astype(BF16)
        state_ref[...] = jnp.zeros_like(state_ref)

    sub = lax.broadcasted_iota(jnp.int32, (2 * SUBLANES, D_S5), 0)
    keep = sub % SUBLANES == lax.broadcasted_iota(jnp.int32, (2 * SUBLANES, D_S5), 1) // win
    first = sub < SUBLANES

    def expand(tt, _):
        for b in range(nb):
            tile = u_ref[b, pl.ds(pl.multiple_of(tt * SUBLANES, SUBLANES), SUBLANES), :]
            for s in range(0, SUBLANES, 2):
                two = jnp.where(first,
                                jnp.broadcast_to(tile[s:s + 1, :], (2 * SUBLANES, D_S5)),
                                jnp.broadcast_to(tile[s + 1:s + 2, :], (2 * SUBLANES, D_S5)))
                dst = pl.ds(pl.multiple_of((tt * SUBLANES + s) * SUBLANES, 2 * SUBLANES),
                            2 * SUBLANES)
                lhs_ref[b, dst, :] = jnp.where(keep, two, 0.0).astype(BF16)
        return 0

    lax.fori_loop(0, t5 // SUBLANES, expand, 0)
    for b in range(nb):
        x_ref[b] = jnp.dot(lhs_ref[b], wb_ref[...], preferred_element_type=F32)

    lr = lam_ref[0]
    li = lam_ref[1]

    def scan(t, carry):
        rows = pl.ds(pl.multiple_of(t * SUBLANES, SUBLANES), SUBLANES)
        new = []
        for b in range(nb):
            xr, xi = carry[b]
            nr = lr * xr - li * xi + x_ref[b, rows, :LANES]
            ni = lr * xi + li * xr + x_ref[b, rows, LANES:]
            x_ref[b, rows, :LANES] = nr
            x_ref[b, rows, LANES:] = ni
            new.append((nr, ni))
        return tuple(new)

    init = tuple((state_ref[b, :, :LANES], state_ref[b, :, LANES:]) for b in range(nb))
    final = lax.fori_loop(0, t5, scan, init, unroll=2)
    for b in range(nb):
        state_ref[b, :, :LANES] = final[b][0]
        state_ref[b, :, LANES:] = final[b][1]

    window = lax.broadcasted_iota(jnp.int32, (t5, LANES), 1) // win
    for b in range(nb):
        z = jnp.dot(x_ref[b].astype(BF16), wc_ref[...], preferred_element_type=F32)
        z_ref[b, 0] = z[:, :LANES]
        z_ref[b, 1] = z[:, LANES:]
    for b in range(nb):
        halves = []
        for hh in range(2):
            per = SUBLANES // 2
            y = z_ref[b, hh, pl.ds(per * hh + per - 1, t5, stride=SUBLANES), :]
            for jj in range(per - 2, -1, -1):
                y = jnp.where(window == jj,
                              z_ref[b, hh, pl.ds(per * hh + jj, t5, stride=SUBLANES), :], y)
            halves.append(y)
        u = u_ref[b]
        y = jax.nn.gelu(jnp.concatenate(halves, axis=1) + d_ref[...] * u)
        gate = jnp.dot(y.astype(BF16), wg_ref[...], preferred_element_type=F32)
        o_ref[b] = (y * jax.nn.sigmoid(gate)).astype(o_ref.dtype)


def _s5(su, prm):
    batch, seq, _ = su.shape
    fixed = lambda i: (0, 0)
    lam = pl.BlockSpec((SUBLANES, LANES), fixed)
    rows8 = SUBLANES * S5_T
    return pl.pallas_call(
        _s5_kernel,
        grid=(seq // S5_T,),
        in_specs=[
            pl.BlockSpec((batch, S5_T, D_S5), lambda i: (0, i, 0)),
            lam, lam, lam,
            pl.BlockSpec((D_S5, LANES), fixed), pl.BlockSpec((D_S5, LANES), fixed),
            pl.BlockSpec((LANES, D_S5), fixed), pl.BlockSpec((LANES, D_S5), fixed),
            pl.BlockSpec((1, D_S5), fixed),
            pl.BlockSpec((D_S5, D_S5), fixed),
        ],
        out_specs=pl.BlockSpec((batch, S5_T, D_S5), lambda i: (0, i, 0)),
        out_shape=jax.ShapeDtypeStruct((batch, seq, D_S5), BF16),
        scratch_shapes=[
            pltpu.VMEM((D_S5, 2 * LANES), BF16),
            pltpu.VMEM((2 * LANES, D_S5), BF16),
            pltpu.VMEM((2, SUBLANES, LANES), F32),
            pltpu.VMEM((batch, rows8, D_S5), BF16),
            pltpu.VMEM((batch, rows8, 2 * LANES), F32),
            pltpu.VMEM((batch, 2, rows8, LANES), F32),
            pltpu.VMEM((batch, SUBLANES, 2 * LANES), F32),
        ],
        compiler_params=_cparams(("arbitrary",)),
        name="s5",
    )(su, prm["a_re"], prm["a_im"], prm["log_dt"], prm["b_re"], prm["b_im"],
      prm["c_re"], prm["c_im"], prm["d"], prm["w_glu"])


def _outproj_kernel(yf_ref, ys_ref, yr_ref, gate_ref, x_ref, w_ref, g_ref, *o_refs, final):
    g = gate_ref[...].astype(F32)
    g = g * jax.nn.sigmoid(g)
    acc = x_ref[...]
    off = 0
    for y_ref in (yf_ref, ys_ref, yr_ref):
        width = y_ref.shape[-1]
        y = (y_ref[...] * g[:, off:off + width]).astype(BF16)
        acc = acc + jnp.dot(y, w_ref[off:off + width, :], preferred_element_type=F32)
        off += width
    normed = _rms_norm(acc, g_ref[...])
    if final:
        o_refs[0][...] = normed
    else:
        o_refs[0][...] = acc
        o_refs[1][...] = normed.astype(o_refs[1].dtype)


def _outproj(y_fox, y_s5, y_ret, rest, x2, w_out, norm_g, final):
    m = x2.shape[0]
    row = lambda i: (i, 0)
    fixed = lambda i: (0, 0)
    tile = pl.BlockSpec((TM_PROJ, D_MODEL), row)
    x_shape = jax.ShapeDtypeStruct((m, D_MODEL), F32)
    return pl.pallas_call(
        functools.partial(_outproj_kernel, final=final),
        grid=(m // TM_PROJ,),
        in_specs=[
            pl.BlockSpec((TM_PROJ, D_FOX), row),
            pl.BlockSpec((TM_PROJ, D_S5), row),
            pl.BlockSpec((TM_PROJ, D_RET), row),
            tile,
            tile,
            pl.BlockSpec((D_MODEL, D_MODEL), fixed),
            pl.BlockSpec((1, D_MODEL), fixed),
        ],
        out_specs=tile if final else [tile, tile],
        out_shape=x_shape if final else [x_shape, jax.ShapeDtypeStruct((m, D_MODEL), BF16)],
        compiler_params=_cparams(("parallel",)),
        name="outproj_final" if final else "outproj",
    )(y_fox, y_s5, y_ret, rest, x2, w_out, norm_g)


def _rotary_tables(seq):
    half = HEAD_DIM // 2
    freqs = ROPE_BASE ** (-jnp.arange(half, dtype=F32) / half)
    ang = jnp.arange(seq, dtype=F32)[:, None] * freqs[None, :]
    cos = jnp.cos(ang)
    sin = jnp.sin(ang)
    cos_t = jnp.tile(cos, (1, LANES // half))
    sin_t = jnp.tile(jnp.concatenate([-sin, sin], axis=-1), (1, LANES // HEAD_DIM))
    return cos_t, sin_t


def _s5_params(a_re, a_im, b_re, b_im, c_re, c_im, d, log_dt, w_glu):
    pair = jnp.eye(2, dtype=F32)
    half = S5_GROUPS // 2
    pack_b = lambda b: jnp.einsum("jnph,kn->jkhnp", b.reshape(half, 2, S5_STATE, S5_GROUP_CH),
                                  pair).reshape(D_S5, LANES)
    pack_c = lambda c: jnp.einsum("jnhp,kn->npjkh", c.reshape(half, 2, S5_GROUP_CH, S5_STATE),
                                  pair).reshape(LANES, D_S5)
    tile = lambda v: v.reshape(SUBLANES, LANES)
    return {
        "a_re": tile(a_re), "a_im": tile(a_im), "log_dt": tile(jnp.repeat(log_dt, S5_STATE)),
        "b_re": pack_b(b_re), "b_im": pack_b(b_im), "c_re": pack_c(c_re), "c_im": pack_c(c_im),
        "d": d.reshape(1, D_S5), "w_glu": w_glu.astype(BF16),
    }


def kernel(x, norm_w, w_in, fox_b_f, s5_a_re, s5_a_im, s5_b_re, s5_b_im, s5_c_re, s5_c_im,
           s5_d, s5_log_dt, s5_w_glu, ret_gn_w, w_out, final_norm_w):
    batch, seq, _ = x.shape
    depth = w_in.shape[0]
    m = batch * seq
    cos_t, sin_t = _rotary_tables(seq)
    log_gamma = jnp.log1p(-(2.0 ** (-5.0 - jnp.arange(RET_HEADS, dtype=F32))))
    lg = jnp.repeat(log_gamma, HEAD_DIM).reshape(1, D_RET)
    norm_g = jnp.concatenate([norm_w, final_norm_w[None]], axis=0).reshape(depth + 1, 1, D_MODEL)
    flog_lo = N_QKV
    flog_hi = N_QKV + FOX_HEADS

    x2 = x.reshape(m, D_MODEL)
    h = _first_norm(x2, norm_g[0])
    for l in range(depth):
        wa = w_in[l, :, :flog_lo].astype(BF16)
        wc = jnp.pad(w_in[l, :, flog_lo:flog_hi], ((0, 0), (0, LANES - FOX_HEADS))).astype(BF16)
        gate_lo = flog_hi + D_S5 + 3 * D_RET
        wb = jnp.concatenate([w_in[l, :, flog_hi:flog_hi + D_S5], w_in[l, :, gate_lo:],
                              w_in[l, :, flog_hi + D_S5:gate_lo]], axis=1).astype(BF16)
        bf_pad = jnp.pad(fox_b_f[l], (0, LANES - FOX_HEADS)).reshape(1, LANES)

        qkv, su, rest, flog = _inproj(h, wa, wb, wc)
        qkv = qkv.reshape(batch, seq, N_QKV)
        rest3 = rest.reshape(batch, seq, N_REST - D_S5)
        y_fox = _fox_attention(qkv, flog.reshape(batch, seq, LANES), bf_pad)
        y_s5 = _s5(su.reshape(batch, seq, D_S5),
                   _s5_params(s5_a_re[l], s5_a_im[l], s5_b_re[l], s5_b_im[l], s5_c_re[l],
                              s5_c_im[l], s5_d[l], s5_log_dt[l], s5_w_glu[l]))
        y_ret = _retention(rest3, cos_t, sin_t, lg, ret_gn_w[l].reshape(1, D_RET))
        out = _outproj(y_fox.reshape(m, D_FOX), y_s5.reshape(m, D_S5), y_ret.reshape(m, D_RET),
                       rest, x2, w_out[l].astype(BF16), norm_g[l + 1], final=(l == depth - 1))
        if l < depth - 1:
            x2, h = out
    return out.reshape(batch, seq, D_MODEL)
```

```python
import functools
import math

import jax
import jax.numpy as jnp
from jax import lax
from jax.experimental import pallas as pl
from jax.experimental.pallas import tpu as pltpu

F32 = jnp.float32
BF16 = jnp.bfloat16

D_MODEL = 1024
HEAD_DIM = 64
CHUNK = 64
D_FOX = 512
FOX_HEADS = 8
D_S5 = 256
S5_GROUPS = 16
S5_GROUP_CH = 16
S5_STATE = 64
D_RET = 256
RET_HEADS = 4
ROPE_BASE = 10000.0
EPS = 1e-6

LANES = 128
SUBLANES = 8
N_QKV = 3 * D_FOX
N_REST = D_S5 + D_MODEL + 3 * D_RET
N_ALL = N_QKV + N_REST + 128
D_IN_PROJ = N_QKV + FOX_HEADS + N_REST
VMEM_LIMIT = 56 * 1024 * 1024

TM_PROJ = 512
T_FOX = 512
FOX_GROUP = 2
T_RET = 256
RET_BLK = 512
S5_T = 256
NEG_BIG = -1e30
FOX_SKIP_LOG2 = 40.0
FOX_NORM_SLACK = 1.02
LOG2E = math.log2(math.e)
Q_SCALE = LOG2E / math.sqrt(HEAD_DIM)


def _cparams(sem, flags=None):
    return pltpu.CompilerParams(dimension_semantics=sem, vmem_limit_bytes=VMEM_LIMIT, flags=flags)


def _rms_norm(x, g):
    ms = jnp.mean(x * x, axis=-1, keepdims=True)
    return x * lax.rsqrt(ms + EPS) * g


def _norm_kernel(x_ref, g_ref, h_ref):
    h_ref[...] = _rms_norm(x_ref[...], g_ref[0]).astype(h_ref.dtype)


def _first_norm(x2, g):
    m = x2.shape[0]
    row = lambda i: (i, 0)
    return pl.pallas_call(
        _norm_kernel,
        grid=(m // TM_PROJ,),
        in_specs=[pl.BlockSpec((TM_PROJ, D_MODEL), row),
                  pl.BlockSpec((1, 1, D_MODEL), lambda i: (0, 0, 0))],
        out_specs=pl.BlockSpec((TM_PROJ, D_MODEL), row),
        out_shape=jax.ShapeDtypeStruct((m, D_MODEL), BF16),
        compiler_params=_cparams(("parallel",)),
        name="first_norm",
    )(x2, g)


def _inproj_kernel(h_ref, win_ref, oa_ref, su_ref, ob_ref, fl_ref, w_ref):
    @pl.when(pl.program_id(0) == 0)
    def _():
        chunk = 128
        n_tail = D_IN_PROJ - N_QKV
        su_lo = FOX_HEADS
        rqkv_lo = su_lo + D_S5
        gate_lo = rqkv_lo + 3 * D_RET

        def pack(r, _):
            rows = pl.ds(pl.multiple_of(r * chunk, chunk), chunk)
            w_ref[rows, :N_QKV] = win_ref[0, rows, :N_QKV].astype(BF16)
            tail = win_ref[0, rows, N_QKV:]
            cols = N_QKV
            for lo, hi in ((su_lo, rqkv_lo), (gate_lo, n_tail), (rqkv_lo, gate_lo)):
                w_ref[rows, cols:cols + hi - lo] = tail[:, lo:hi].astype(BF16)
                cols += hi - lo
            w_ref[rows, cols:] = jnp.pad(tail[:, :su_lo],
                                         ((0, 0), (0, LANES - FOX_HEADS))).astype(BF16)
            return 0

        lax.fori_loop(0, D_MODEL // chunk, pack, 0)

    h = h_ref[...]
    qkv = jnp.dot(h, w_ref[:, :N_QKV], preferred_element_type=F32)
    oa_ref[:, :D_FOX] = (qkv[:, :D_FOX] * Q_SCALE).astype(oa_ref.dtype)
    oa_ref[:, D_FOX:] = qkv[:, D_FOX:].astype(oa_ref.dtype)
    rest = jnp.dot(h, w_ref[:, N_QKV:N_QKV + N_REST], preferred_element_type=F32)
    su_ref[...] = rest[:, :D_S5]
    ob_ref[...] = rest[:, D_S5:].astype(ob_ref.dtype)
    fl_ref[...] = jnp.dot(h, w_ref[:, N_QKV + N_REST:], preferred_element_type=F32)


def _inproj(h, w_in, layer):
    m = h.shape[0]
    row = lambda i: (i, 0)
    return pl.pallas_call(
        _inproj_kernel,
        grid=(m // TM_PROJ,),
        in_specs=[
            pl.BlockSpec((TM_PROJ, D_MODEL), row),
            pl.BlockSpec((1, D_MODEL, D_IN_PROJ), lambda i: (layer, 0, 0),
                         pipeline_mode=pl.Buffered(1)),
        ],
        out_specs=[
            pl.BlockSpec((TM_PROJ, N_QKV), row),
            pl.BlockSpec((TM_PROJ, D_S5), row),
            pl.BlockSpec((TM_PROJ, N_REST - D_S5), row),
            pl.BlockSpec((TM_PROJ, LANES), row),
        ],
        out_shape=[
            jax.ShapeDtypeStruct((m, N_QKV), BF16),
            jax.ShapeDtypeStruct((m, D_S5), F32),
            jax.ShapeDtypeStruct((m, N_REST - D_S5), BF16),
            jax.ShapeDtypeStruct((m, LANES), F32),
        ],
        scratch_shapes=[pltpu.VMEM((D_MODEL, N_ALL), BF16)],
        compiler_params=_cparams(("arbitrary",)),
        name="inproj",
    )(h, w_in)


def _fox_kernel(q_ref, k_ref, v_ref, fl_ref, bf_ref, o_ref, kaug_ref, vaug_ref, c_ref, clast_ref,
                knorm_ref):
    t = T_FOX
    i = pl.program_id(1)
    seq = k_ref.shape[1]
    lane = lax.broadcasted_iota(jnp.int32, (t, LANES), 1)
    lane1 = lane[0:1, :]
    own = (lane < HEAD_DIM, lane >= HEAD_DIM)
    n_pairs = D_FOX // LANES
    heads = [(g, e) for g in range(n_pairs) for e in range(2)]
    cols = lambda g: slice(g * LANES, (g + 1) * LANES)
    x0 = lambda e: HEAD_DIM if e == 0 else 0
    shift = lambda g, e: (x0(e) - (2 * g + e)) % LANES

    half_id = lax.broadcasted_iota(jnp.int32, (LANES, LANES), 0) // HEAD_DIM
    same_half = (half_id == lax.broadcasted_iota(jnp.int32, (LANES, LANES), 1) // HEAD_DIM)
    head_sum = jnp.where(same_half, 1.0, 0.0).astype(BF16)

    def max_sq_norms(ref, rows):
        out = jnp.zeros((1, LANES), F32)
        for g in range(n_pairs):
            x = ref[0, rows, cols(g)]
            sums = jnp.dot(x * x, head_sum, preferred_element_type=F32)
            top = jnp.max(sums, axis=0, keepdims=True) * FOX_NORM_SLACK
            out = jnp.where(lane1 == 2 * g, top,
                            jnp.where(lane1 == 2 * g + 1, pltpu.roll(top, HEAD_DIM, 1), out))
        return out

    @pl.when(i == 0)
    def _():
        r = lax.broadcasted_iota(jnp.int32, (t, t), 0)
        s = lax.broadcasted_iota(jnp.int32, (t, t), 1)
        tri = jnp.where(s <= r, 1.0, 0.0).astype(BF16)

        def build(j, carried):
            carry, k_max = carried
            rows = pl.ds(pl.multiple_of(j * t, t), t)
            z = fl_ref[0, rows, :] + bf_ref[0]
            logf = jnp.minimum(z, 0.0) - jnp.log1p(jnp.exp(-jnp.abs(z)))
            top = logf.astype(BF16)
            rem = logf - top.astype(F32)
            middle = rem.astype(BF16)
            bottom = (rem - middle.astype(F32)).astype(BF16)
            c = carry + sum(jnp.dot(tri, piece, preferred_element_type=F32)
                            for piece in (bottom, middle, top))
            c_ref[rows, :] = c
            clast_ref[pl.ds(j, 1), :] = c[t - 1:t, :]
            k_max = jnp.maximum(k_max, max_sq_norms(k_ref, rows))
            knorm_ref[pl.ds(j, 1), :] = k_max
            b = -LOG2E * c
            hi = b.astype(BF16).astype(F32)
            mid = (b - hi).astype(BF16).astype(F32)
            lo = b - hi - mid
            pieces = jnp.where(lane < 8, hi, jnp.where(lane < 16, pltpu.roll(mid, 8, 1), jnp.where(
                lane < 24, pltpu.roll(lo, 16, 1), jnp.where(lane < 32, 1.0, 0.0))))
            for n, (g, e) in enumerate(heads):
                extra = pltpu.roll(pieces, shift(g, e), 1).astype(BF16)
                kaug_ref[n, rows, :] = jnp.where(own[e], k_ref[0, rows, cols(g)], extra)
                v = v_ref[0, rows, cols(g)]
                vaug_ref[n, rows, :] = jnp.where(own[e], v, jnp.ones_like(v))
            return c[t - 1:t, :], k_max

        zero_row = jnp.zeros((1, LANES), F32)
        lax.fori_loop(0, seq // t, build, (zero_row, zero_row))

    cq0 = LOG2E * c_ref[pl.ds(pl.multiple_of(i * t, t), SUBLANES), :][0:1, :]
    qa = []
    for g, e in heads:
        ones = (lane1 == x0(e)) | (lane1 == x0(e) + 8) | (lane1 == x0(e) + 16)
        extra = jnp.where(ones, 1.0, jnp.where(
            lane1 == x0(e) + 24, pltpu.roll(cq0, (shift(g, e) + 24) % LANES, 1), 0.0))
        qa.append(jnp.where(own[e], q_ref[0, :, cols(g)], extra.astype(BF16)))

    def tile(j, carry, masked, ids):
        m, acc = carry
        rows = pl.ds(pl.multiple_of(j * t, t), t)
        new_m, new_acc = [], []
        for pos, n in enumerate(ids):
            s = lax.dot_general(qa[n], kaug_ref[n, rows, :], (((1,), (1,)), ((), ())),
                                preferred_element_type=F32)
            if masked:
                r = lax.broadcasted_iota(jnp.int32, (t, t), 0)
                cidx = lax.broadcasted_iota(jnp.int32, (t, t), 1)
                s = jnp.where(cidx <= r, s, NEG_BIG)
            m_new = jnp.maximum(m[pos], jnp.max(s, axis=-1, keepdims=True))
            alpha = jnp.exp2(m[pos] - m_new)
            p = jnp.exp2(s - m_new).astype(BF16)
            new_m.append(m_new)
            new_acc.append(acc[pos] * alpha + jnp.dot(p, vaug_ref[n, rows, :],
                                                      preferred_element_type=F32))
        return tuple(new_m), tuple(new_acc)

    n_tiles = seq // t
    tile_id = lax.broadcasted_iota(jnp.int32, (n_tiles, LANES), 0)
    cq0_used = cq0.astype(BF16).astype(F32)
    upper = (jnp.sqrt(max_sq_norms(q_ref, slice(None)) * knorm_ref[...])
             + (cq0_used - LOG2E * clast_ref[...]))

    def needed(j, m, ids):
        row_min = jnp.zeros((1, LANES), F32)
        for pos, n in enumerate(ids):
            row_min = jnp.where(lane1 == n, jnp.min(m[pos], axis=0, keepdims=True), row_min)
        upper_j = jnp.sum(jnp.where(tile_id == j, upper, 0.0), axis=0, keepdims=True)
        hit = (upper_j >= row_min - FOX_SKIP_LOG2) & (lane1 >= ids[0]) & (lane1 <= ids[-1])
        return (j >= 0) & (jnp.max(jnp.where(hit, 1.0, 0.0)) > 0.0)

    acc = []
    for first in range(0, len(heads), FOX_GROUP):
        ids = list(range(first, first + FOX_GROUP))
        init = ((jnp.full((t, 1), NEG_BIG, F32),) * FOX_GROUP,
                (jnp.zeros((t, LANES), F32),) * FOX_GROUP)
        m, group_acc = tile(i, init, True, ids)

        def walk(state, ids=ids):
            j, _, m, group_acc = state
            m, group_acc = tile(j, (m, group_acc), False, ids)
            return j - 1, needed(j - 1, m, ids), m, group_acc

        state = lax.while_loop(lambda state: state[1], walk,
                               (i - 1, needed(i - 1, m, ids), m, group_acc))
        acc.extend(state[3])
    outs = [a / pltpu.roll(a, HEAD_DIM, 1) for a in acc]
    for g in range(n_pairs):
        o_ref[0, :, cols(g)] = jnp.where(own[0], outs[2 * g], outs[2 * g + 1]).astype(o_ref.dtype)


def _fox_attention(qkv, flog, bf_pad, layer):
    batch, seq, _ = qkv.shape
    n_heads = D_FOX // HEAD_DIM
    once = pl.Buffered(1)
    return pl.pallas_call(
        _fox_kernel,
        grid=(batch, seq // T_FOX),
        in_specs=[
            pl.BlockSpec((1, T_FOX, D_FOX), lambda b, i: (b, i, 0)),
            pl.BlockSpec((1, seq, D_FOX), lambda b, i: (b, 0, 1), pipeline_mode=once),
            pl.BlockSpec((1, seq, D_FOX), lambda b, i: (b, 0, 2), pipeline_mode=once),
            pl.BlockSpec((1, seq, LANES), lambda b, i: (b, 0, 0), pipeline_mode=once),
            pl.BlockSpec((1, 1, LANES), lambda b, i: (layer, 0, 0)),
        ],
        out_specs=pl.BlockSpec((1, T_FOX, D_FOX), lambda b, i: (b, i, 0)),
        out_shape=jax.ShapeDtypeStruct((batch, seq, D_FOX), BF16),
        scratch_shapes=[pltpu.VMEM((n_heads, seq, LANES), BF16),
                        pltpu.VMEM((n_heads, seq, LANES), BF16),
                        pltpu.VMEM((seq, LANES), F32),
                        pltpu.VMEM((seq // T_FOX, LANES), F32),
                        pltpu.VMEM((seq // T_FOX, LANES), F32)],
        compiler_params=_cparams(("parallel", "arbitrary")),
        name="fox_attention",
    )(qkv, qkv, qkv, flog, bf_pad)


def _ret_kernel(q_ref, k_ref, v_ref, cos_ref, sin_ref, lg_ref, gn_ref, o_ref,
                state_ref, dmat_ref, wq_ref, wk_ref):
    t = T_RET
    n_pairs = D_RET // LANES
    scale = 1.0 / math.sqrt(HEAD_DIM)
    lane = lax.broadcasted_iota(jnp.int32, (t, LANES), 1)
    first = lane < HEAD_DIM
    low_half = (lane % HEAD_DIM) < (HEAD_DIM // 2)
    cols = lambda p: slice(p * LANES, (p + 1) * LANES)

    @pl.when((pl.program_id(0) == 0) & (pl.program_id(1) == 0))
    def _():
        pos = lax.broadcasted_iota(jnp.int32, (t, 1), 0).astype(F32)
        r = lax.broadcasted_iota(jnp.int32, (t, t), 0)
        s = lax.broadcasted_iota(jnp.int32, (t, t), 1)
        dist = jnp.abs(r - s).astype(F32)
        visible = (s // CHUNK) <= (r // CHUNK)
        for h in range(RET_HEADS):
            lg = lg_ref[:, h * HEAD_DIM:h * HEAD_DIM + 1]
            dmat_ref[h] = jnp.where(visible, jnp.exp(lg * dist) * scale, 0.0)
        wq_ref[...] = jnp.exp(lg_ref[...] * (pos + 1.0)) * scale
        wk_ref[...] = jnp.exp(lg_ref[...] * (t - 1.0 - pos))

    @pl.when(pl.program_id(1) == 0)
    def _():
        state_ref[...] = jnp.zeros_like(state_ref)

    rb = lax.broadcasted_iota(jnp.int32, (LANES, LANES), 0)
    cb = lax.broadcasted_iota(jnp.int32, (LANES, LANES), 1)
    same_head = (rb // HEAD_DIM) == (cb // HEAD_DIM)

    def head_mean(x):
        s0 = jnp.sum(jnp.where(first, x, 0.0), axis=-1, keepdims=True)
        s1 = jnp.sum(jnp.where(first, 0.0, x), axis=-1, keepdims=True)
        return jnp.where(first, s0, s1) * (1.0 / HEAD_DIM)

    for sub in range(RET_BLK // t):
        rows = slice(sub * t, (sub + 1) * t)
        cos = cos_ref[rows, :]
        sin = sin_ref[rows, :]

        def rotary(x):
            swapped = jnp.where(low_half, pltpu.roll(x, LANES - HEAD_DIM // 2, 1),
                                pltpu.roll(x, HEAD_DIM // 2, 1))
            return x * cos + swapped * sin

        for p in range(n_pairs):
            q = rotary(q_ref[0, rows, cols(p)].astype(F32))
            k = rotary(k_ref[0, rows, cols(p)].astype(F32))
            vb = v_ref[0, rows, cols(p)]
            kb = k.astype(BF16)
            zero = jnp.zeros_like(q)
            inner = []
            for e in range(2):
                qe = jnp.where(first, q, zero) if e == 0 else jnp.where(first, zero, q)
                sc = lax.dot_general(qe.astype(BF16), kb, (((1,), (1,)), ((), ())),
                                     preferred_element_type=F32) * dmat_ref[2 * p + e]
                inner.append(jnp.dot(sc.astype(BF16), vb, preferred_element_type=F32))
            state = state_ref[p]
            o = jnp.where(first, inner[0], inner[1]) + jnp.dot(
                (q * wq_ref[:, cols(p)]).astype(BF16), state.astype(BF16),
                preferred_element_type=F32)
            upd = lax.dot_general((k * wk_ref[:, cols(p)]).astype(BF16), vb,
                                  (((0,), (0,)), ((), ())), preferred_element_type=F32)
            decay = jnp.exp(lg_ref[:, cols(p)] * float(t))
            state_ref[p] = jnp.where(same_head, state * decay + upd, 0.0)
            d = o - head_mean(o)
            var = head_mean(d * d)
            o_ref[0, rows, cols(p)] = (d * lax.rsqrt(var + EPS) * gn_ref[0, :, cols(p)]).astype(
                o_ref.dtype)


def _retention(rest, cos_t, sin_t, lg, gn_w, layer):
    batch, seq, _ = rest.shape
    n_pairs = D_RET // LANES
    tok = lambda blk: pl.BlockSpec((1, RET_BLK, D_RET), lambda b, i, blk=blk: (b, i, blk))
    fixed = pl.BlockSpec((1, D_RET), lambda b, i: (0, 0))
    table = pl.BlockSpec((RET_BLK, LANES), lambda b, i: (i, 0))
    return pl.pallas_call(
        _ret_kernel,
        grid=(batch, seq // RET_BLK),
        in_specs=[tok(4), tok(5), tok(6), table, table, fixed,
                  pl.BlockSpec((1, 1, D_RET), lambda b, i: (layer, 0, 0))],
        out_specs=pl.BlockSpec((1, RET_BLK, D_RET), lambda b, i: (b, i, 0)),
        out_shape=jax.ShapeDtypeStruct((batch, seq, D_RET), BF16),
        scratch_shapes=[
            pltpu.VMEM((n_pairs, LANES, LANES), F32),
            pltpu.VMEM((RET_HEADS, T_RET, T_RET), F32),
            pltpu.VMEM((T_RET, D_RET), F32),
            pltpu.VMEM((T_RET, D_RET), F32),
        ],
        compiler_params=_cparams(("arbitrary", "arbitrary")),
        name="retention",
    )(rest, rest, rest, cos_t, sin_t, lg, gn_w)


def _s5_kernel(u_ref, are_ref, aim_ref, ldt_ref, br_ref, bi_ref, cr_ref, ci_ref, d_ref, wg_ref,
               o_ref, wb_ref, wc_ref, lam_ref, lhs_ref, x_ref, z_ref, state_ref):
    nb = u_ref.shape[0]
    t5 = S5_T
    win = D_S5 // SUBLANES

    @pl.when(pl.program_id(0) == 0)
    def _():
        dt = jnp.exp(ldt_ref[0])
        ar = are_ref[0]
        ai = aim_ref[0]
        mag = jnp.exp(ar * dt)
        lr = mag * jnp.cos(ai * dt)
        li = mag * jnp.sin(ai * dt)
        den = ar * ar + ai * ai
        fr = ((lr - 1.0) * ar + li * ai) / den
        fi = (li * ar - (lr - 1.0) * ai) / den
        lam_ref[0] = lr
        lam_ref[1] = li
        for j in range(SUBLANES):
            rows = slice(win * j, win * (j + 1))
            f_r = fr[j:j + 1, :]
            f_i = fi[j:j + 1, :]
            b_r = br_ref[0, rows, :]
            b_i = bi_ref[0, rows, :]
            wb_ref[rows, :LANES] = (f_r * b_r - f_i * b_i).astype(BF16)
            wb_ref[rows, LANES:] = (f_r * b_i + f_i * b_r).astype(BF16)
        wc_ref[:LANES, :] = cr_ref[0].astype(BF16)
        wc_ref[LANES:, :] = (-ci_ref[0]).astype(BF16)
        state_ref[...] = jnp.zeros_like(state_ref)

    sub = lax.broadcasted_iota(jnp.int32, (2 * SUBLANES, D_S5), 0)
    keep = sub % SUBLANES == lax.broadcasted_iota(jnp.int32, (2 * SUBLANES, D_S5), 1) // win
    first = sub < SUBLANES

    def expand(tt, _):
        for b in range(nb):
            tile = u_ref[b, pl.ds(pl.multiple_of(tt * SUBLANES, SUBLANES), SUBLANES), :]
            for s in range(0, SUBLANES, 2):
                two = jnp.where(first,
                                jnp.broadcast_to(tile[s:s + 1, :], (2 * SUBLANES, D_S5)),
                                jnp.broadcast_to(tile[s + 1:s + 2, :], (2 * SUBLANES, D_S5)))
                dst = pl.ds(pl.multiple_of((tt * SUBLANES + s) * SUBLANES, 2 * SUBLANES),
                            2 * SUBLANES)
                lhs_ref[b, dst, :] = jnp.where(keep, two, 0.0).astype(BF16)
        return 0

    lax.fori_loop(0, t5 // SUBLANES, expand, 0)
    for b in range(nb):
        x_ref[b] = jnp.dot(lhs_ref[b], wb_ref[...], preferred_element_type=F32)

    lr = lam_ref[0]
    li = lam_ref[1]

    def scan(t, carry):
        rows = pl.ds(pl.multiple_of(t * SUBLANES, SUBLANES), SUBLANES)
        new = []
        for b in range(nb):
            xr, xi = carry[b]
            nr = lr * xr - li * xi + x_ref[b, rows, :LANES]
            ni = lr * xi + li * xr + x_ref[b, rows, LANES:]
            x_ref[b, rows, :LANES] = nr
            x_ref[b, rows, LANES:] = ni
            new.append((nr, ni))
        return tuple(new)

    init = tuple((state_ref[b, :, :LANES], state_ref[b, :, LANES:]) for b in range(nb))
    final = lax.fori_loop(0, t5, scan, init, unroll=2)
    for b in range(nb):
        state_ref[b, :, :LANES] = final[b][0]
        state_ref[b, :, LANES:] = final[b][1]

    window = lax.broadcasted_iota(jnp.int32, (t5, LANES), 1) // win
    for b in range(nb):
        z = jnp.dot(x_ref[b].astype(BF16), wc_ref[...], preferred_element_type=F32)
        z_ref[b, 0] = z[:, :LANES]
        z_ref[b, 1] = z[:, LANES:]
    for b in range(nb):
        halves = []
        for hh in range(2):
            per = SUBLANES // 2
            y = z_ref[b, hh, pl.ds(per * hh + per - 1, t5, stride=SUBLANES), :]
            for jj in range(per - 2, -1, -1):
                y = jnp.where(window == jj,
                              z_ref[b, hh, pl.ds(per * hh + jj, t5, stride=SUBLANES), :], y)
            halves.append(y)
        u = u_ref[b]
        y = jax.nn.gelu(jnp.concatenate(halves, axis=1) + d_ref[0] * u)
        gate = jnp.dot(y.astype(BF16), wg_ref[0], preferred_element_type=F32)
        o_ref[b] = (y * jax.nn.sigmoid(gate)).astype(o_ref.dtype)


def _s5(su, prm, layer):
    batch, seq, _ = su.shape
    fixed = lambda i: (layer, 0, 0)
    lam = pl.BlockSpec((1, SUBLANES, LANES), fixed)
    rows8 = SUBLANES * S5_T
    return pl.pallas_call(
        _s5_kernel,
        grid=(seq // S5_T,),
        in_specs=[
            pl.BlockSpec((batch, S5_T, D_S5), lambda i: (0, i, 0)),
            lam, lam, lam,
            pl.BlockSpec((1, D_S5, LANES), fixed), pl.BlockSpec((1, D_S5, LANES), fixed),
            pl.BlockSpec((1, LANES, D_S5), fixed), pl.BlockSpec((1, LANES, D_S5), fixed),
            pl.BlockSpec((1, 1, D_S5), fixed),
            pl.BlockSpec((1, D_S5, D_S5), fixed),
        ],
        out_specs=pl.BlockSpec((batch, S5_T, D_S5), lambda i: (0, i, 0)),
        out_shape=jax.ShapeDtypeStruct((batch, seq, D_S5), BF16),
        scratch_shapes=[
            pltpu.VMEM((D_S5, 2 * LANES), BF16),
            pltpu.VMEM((2 * LANES, D_S5), BF16),
            pltpu.VMEM((2, SUBLANES, LANES), F32),
            pltpu.VMEM((batch, rows8, D_S5), BF16),
            pltpu.VMEM((batch, rows8, 2 * LANES), F32),
            pltpu.VMEM((batch, 2, rows8, LANES), F32),
            pltpu.VMEM((batch, SUBLANES, 2 * LANES), F32),
        ],
        compiler_params=_cparams(("arbitrary",)),
        name="s5",
    )(su, prm["a_re"], prm["a_im"], prm["log_dt"], prm["b_re"], prm["b_im"],
      prm["c_re"], prm["c_im"], prm["d"], prm["w_glu"])


def _outproj_kernel(yf_ref, ys_ref, yr_ref, gate_ref, x_ref, wout_ref, g_ref, *refs, final):
    o_refs, w_ref = refs[:-1], refs[-1]

    @pl.when(pl.program_id(0) == 0)
    def _():
        w_ref[...] = wout_ref[0].astype(BF16)

    g = gate_ref[...].astype(F32)
    g = g * jax.nn.sigmoid(g)
    acc = x_ref[...]
    off = 0
    for y_ref in (yf_ref, ys_ref, yr_ref):
        width = y_ref.shape[-1]
        y = (y_ref[...] * g[:, off:off + width]).astype(BF16)
        acc = acc + jnp.dot(y, w_ref[off:off + width, :], preferred_element_type=F32)
        off += width
    normed = _rms_norm(acc, g_ref[0])
    if final:
        o_refs[0][...] = normed
    else:
        o_refs[0][...] = acc
        o_refs[1][...] = normed.astype(o_refs[1].dtype)


def _outproj(y_fox, y_s5, y_ret, rest, x2, w_out, norm_g, layer, final):
    m = x2.shape[0]
    row = lambda i: (i, 0)
    tile = pl.BlockSpec((TM_PROJ, D_MODEL), row)
    x_shape = jax.ShapeDtypeStruct((m, D_MODEL), F32)
    return pl.pallas_call(
        functools.partial(_outproj_kernel, final=final),
        grid=(m // TM_PROJ,),
        in_specs=[
            pl.BlockSpec((TM_PROJ, D_FOX), row),
            pl.BlockSpec((TM_PROJ, D_S5), row),
            pl.BlockSpec((TM_PROJ, D_RET), row),
            tile,
            tile,
            pl.BlockSpec((1, D_MODEL, D_MODEL), lambda i: (layer, 0, 0),
                         pipeline_mode=pl.Buffered(1)),
            pl.BlockSpec((1, 1, D_MODEL), lambda i: (layer + 1, 0, 0)),
        ],
        out_specs=tile if final else [tile, tile],
        out_shape=x_shape if final else [x_shape, jax.ShapeDtypeStruct((m, D_MODEL), BF16)],
        scratch_shapes=[pltpu.VMEM((D_MODEL, D_MODEL), BF16)],
        compiler_params=_cparams(("arbitrary",)),
        name="outproj_final" if final else "outproj",
    )(y_fox, y_s5, y_ret, rest, x2, w_out, norm_g)


def _rotary_tables(seq):
    half = HEAD_DIM // 2
    freqs = ROPE_BASE ** (-jnp.arange(half, dtype=F32) / half)
    ang = jnp.arange(seq, dtype=F32)[:, None] * freqs[None, :]
    cos = jnp.cos(ang)
    sin = jnp.sin(ang)
    cos_t = jnp.tile(cos, (1, LANES // half))
    sin_t = jnp.tile(jnp.concatenate([-sin, sin], axis=-1), (1, LANES // HEAD_DIM))
    return cos_t, sin_t


def _s5_params(a_re, a_im, b_re, b_im, c_re, c_im, d, log_dt, w_glu):
    depth = a_re.shape[0]
    pair = jnp.eye(2, dtype=F32)
    half = S5_GROUPS // 2
    pack_b = lambda b: jnp.einsum(
        "ljnph,kn->ljkhnp", b.reshape(depth, half, 2, S5_STATE, S5_GROUP_CH), pair
    ).reshape(depth, D_S5, LANES)
    pack_c = lambda c: jnp.einsum(
        "ljnhp,kn->lnpjkh", c.reshape(depth, half, 2, S5_GROUP_CH, S5_STATE), pair
    ).reshape(depth, LANES, D_S5)
    tile = lambda v: v.reshape(depth, SUBLANES, LANES)
    return {
        "a_re": tile(a_re), "a_im": tile(a_im),
        "log_dt": tile(jnp.repeat(log_dt, S5_STATE, axis=-1)),
        "b_re": pack_b(b_re), "b_im": pack_b(b_im), "c_re": pack_c(c_re), "c_im": pack_c(c_im),
        "d": d.reshape(depth, 1, D_S5), "w_glu": w_glu.astype(BF16),
    }


def kernel(x, norm_w, w_in, fox_b_f, s5_a_re, s5_a_im, s5_b_re, s5_b_im, s5_c_re, s5_c_im,
           s5_d, s5_log_dt, s5_w_glu, ret_gn_w, w_out, final_norm_w):
    batch, seq, _ = x.shape
    depth = w_in.shape[0]
    m = batch * seq
    cos_t, sin_t = _rotary_tables(seq)
    log_gamma = jnp.log1p(-(2.0 ** (-5.0 - jnp.arange(RET_HEADS, dtype=F32))))
    lg = jnp.repeat(log_gamma, HEAD_DIM).reshape(1, D_RET)
    norm_g = jnp.concatenate([norm_w, final_norm_w[None]], axis=0).reshape(depth + 1, 1, D_MODEL)

    bf_pad = jnp.pad(fox_b_f, ((0, 0), (0, LANES - FOX_HEADS))).reshape(depth, 1, LANES)
    gn_w = ret_gn_w.reshape(depth, 1, D_RET)
    s5_prm = _s5_params(s5_a_re, s5_a_im, s5_b_re, s5_b_im, s5_c_re, s5_c_im, s5_d, s5_log_dt,
                        s5_w_glu)

    x2 = x.reshape(m, D_MODEL)
    h = _first_norm(x2, norm_g)
    for l in range(depth):
        qkv, su, rest, flog = _inproj(h, w_in, l)
        qkv = qkv.reshape(batch, seq, N_QKV)
        rest3 = rest.reshape(batch, seq, N_REST - D_S5)
        y_fox = _fox_attention(qkv, flog.reshape(batch, seq, LANES), bf_pad, l)
        y_s5 = _s5(su.reshape(batch, seq, D_S5), s5_prm, l)
        y_ret = _retention(rest3, cos_t, sin_t, lg, gn_w, l)
        out = _outproj(y_fox.reshape(m, D_FOX), y_s5.reshape(m, D_S5), y_ret.reshape(m, D_RET),
                       rest, x2, w_out, norm_g, l, final=(l == depth - 1))
        if l < depth - 1:
            x2, h = out
    return out.reshape(batch, seq, D_MODEL)
```

```python
import functools
import math

import jax
import jax.numpy as jnp
from jax import lax
from jax.experimental import pallas as pl
from jax.experimental.pallas import tpu as pltpu

F32 = jnp.float32
BF16 = jnp.bfloat16

D_MODEL = 1024
HEAD_DIM = 64
CHUNK = 64
D_FOX = 512
FOX_HEADS = 8
D_S5 = 256
S5_GROUPS = 16
S5_GROUP_CH = 16
S5_STATE = 64
D_RET = 256
RET_HEADS = 4
ROPE_BASE = 10000.0
EPS = 1e-6

LANES = 128
SUBLANES = 8
N_QKV = 3 * D_FOX
N_REST = D_S5 + D_MODEL + 3 * D_RET
N_ALL = N_QKV + N_REST + 128
D_IN_PROJ = N_QKV + FOX_HEADS + N_REST
VMEM_LIMIT = 56 * 1024 * 1024

TM_PROJ = 512
T_FOX = 512
FOX_GROUP = 2
T_RET = 256
RET_BLK = 512
S5_T = 256
NEG_BIG = -1e30
FOX_SKIP_LOG2 = 40.0
FOX_NORM_SLACK = 1.02
LOG2E = math.log2(math.e)
Q_SCALE = LOG2E / math.sqrt(HEAD_DIM)


def _cparams(sem, flags=None):
    return pltpu.CompilerParams(dimension_semantics=sem, vmem_limit_bytes=VMEM_LIMIT, flags=flags)


def _rms_norm(x, g):
    ms = jnp.mean(x * x, axis=-1, keepdims=True)
    return x * lax.rsqrt(ms + EPS) * g


def _inproj_kernel(h_ref, g_ref, wt_ref, oa_ref, su_ref, ob_ref, fl_ref, w_ref, *, normalize):
    @pl.when(pl.program_id(0) == 0)
    def _():
        su_lo = N_QKV + FOX_HEADS
        rqkv_lo = su_lo + D_S5
        gate_lo = rqkv_lo + 3 * D_RET
        rows = 0
        for lo, hi in ((0, N_QKV), (su_lo, rqkv_lo), (gate_lo, D_IN_PROJ), (rqkv_lo, gate_lo)):
            for start in range(lo, hi, 256):
                stop = min(start + 256, hi)
                w_ref[rows:rows + stop - start, :] = wt_ref[0, start:stop, :].astype(BF16)
                rows += stop - start
        flog = jnp.concatenate([wt_ref[0, N_QKV:su_lo, :],
                                jnp.zeros((LANES - FOX_HEADS, D_MODEL), F32)], axis=0)
        w_ref[rows:, :] = flog.astype(BF16)

    h = h_ref[...]
    if normalize:
        h = _rms_norm(h, g_ref[0]).astype(BF16)
    nt = (((1,), (1,)), ((), ()))
    qkv = lax.dot_general(h, w_ref[:N_QKV, :], nt, preferred_element_type=F32)
    oa_ref[:, :D_FOX] = (qkv[:, :D_FOX] * Q_SCALE).astype(oa_ref.dtype)
    oa_ref[:, D_FOX:] = qkv[:, D_FOX:].astype(oa_ref.dtype)
    rest = lax.dot_general(h, w_ref[N_QKV:N_QKV + N_REST, :], nt, preferred_element_type=F32)
    su_ref[...] = rest[:, :D_S5]
    ob_ref[...] = rest[:, D_S5:].astype(ob_ref.dtype)
    fl_ref[...] = lax.dot_general(h, w_ref[N_QKV + N_REST:, :], nt, preferred_element_type=F32)


def _inproj(h, norm_g, w_in_t, layer):
    m = h.shape[0]
    row = lambda i: (i, 0)
    return pl.pallas_call(
        functools.partial(_inproj_kernel, normalize=(layer == 0)),
        grid=(m // TM_PROJ,),
        in_specs=[
            pl.BlockSpec((TM_PROJ, D_MODEL), row),
            pl.BlockSpec((1, 1, D_MODEL), lambda i: (layer, 0, 0)),
            pl.BlockSpec((1, D_IN_PROJ, D_MODEL), lambda i: (layer, 0, 0),
                         pipeline_mode=pl.Buffered(1)),
        ],
        out_specs=[
            pl.BlockSpec((TM_PROJ, N_QKV), row),
            pl.BlockSpec((TM_PROJ, D_S5), row),
            pl.BlockSpec((TM_PROJ, N_REST - D_S5), row),
            pl.BlockSpec((TM_PROJ, LANES), row),
        ],
        out_shape=[
            jax.ShapeDtypeStruct((m, N_QKV), BF16),
            jax.ShapeDtypeStruct((m, D_S5), F32),
            jax.ShapeDtypeStruct((m, N_REST - D_S5), BF16),
            jax.ShapeDtypeStruct((m, LANES), F32),
        ],
        scratch_shapes=[pltpu.VMEM((N_ALL, D_MODEL), BF16)],
        compiler_params=_cparams(("arbitrary",)),
        name="inproj",
    )(h, norm_g, w_in_t)


def _fox_kernel(q_ref, k_ref, v_ref, fl_ref, bf_ref, o_ref, kaug_ref, vaug_ref, c_ref, clast_ref,
                knorm_ref):
    t = T_FOX
    i = pl.program_id(1)
    seq = k_ref.shape[1]
    lane = lax.broadcasted_iota(jnp.int32, (t, LANES), 1)
    lane1 = lane[0:1, :]
    own = (lane < HEAD_DIM, lane >= HEAD_DIM)
    n_pairs = D_FOX // LANES
    heads = [(g, e) for g in range(n_pairs) for e in range(2)]
    cols = lambda g: slice(g * LANES, (g + 1) * LANES)
    x0 = lambda e: HEAD_DIM if e == 0 else 0
    shift = lambda g, e: (x0(e) - (2 * g + e)) % LANES

    half_id = lax.broadcasted_iota(jnp.int32, (LANES, LANES), 0) // HEAD_DIM
    same_half = (half_id == lax.broadcasted_iota(jnp.int32, (LANES, LANES), 1) // HEAD_DIM)
    head_sum = jnp.where(same_half, 1.0, 0.0).astype(BF16)

    def max_sq_norms(ref, rows):
        out = jnp.zeros((1, LANES), F32)
        for g in range(n_pairs):
            x = ref[0, rows, cols(g)]
            sums = jnp.dot(x * x, head_sum, preferred_element_type=F32)
            top = jnp.max(sums, axis=0, keepdims=True) * FOX_NORM_SLACK
            out = jnp.where(lane1 == 2 * g, top,
                            jnp.where(lane1 == 2 * g + 1, pltpu.roll(top, HEAD_DIM, 1), out))
        return out

    @pl.when(i == 0)
    def _():
        r = lax.broadcasted_iota(jnp.int32, (t, t), 0)
        s = lax.broadcasted_iota(jnp.int32, (t, t), 1)
        tri = jnp.where(s <= r, 1.0, 0.0).astype(BF16)

        def build(j, carried):
            carry, k_max = carried
            rows = pl.ds(pl.multiple_of(j * t, t), t)
            z = fl_ref[0, rows, :] + bf_ref[0]
            logf = jnp.minimum(z, 0.0) - jnp.log1p(jnp.exp(-jnp.abs(z)))
            top = logf.astype(BF16)
            rem = logf - top.astype(F32)
            middle = rem.astype(BF16)
            bottom = (rem - middle.astype(F32)).astype(BF16)
            c = carry + sum(jnp.dot(tri, piece, preferred_element_type=F32)
                            for piece in (bottom, middle, top))
            c_ref[rows, :] = c
            clast_ref[pl.ds(j, 1), :] = c[t - 1:t, :]
            k_max = jnp.maximum(k_max, max_sq_norms(k_ref, rows))
            knorm_ref[pl.ds(j, 1), :] = k_max
            b = -LOG2E * c
            hi = b.astype(BF16).astype(F32)
            mid = (b - hi).astype(BF16).astype(F32)
            lo = b - hi - mid
            pieces = jnp.where(lane < 8, hi, jnp.where(lane < 16, pltpu.roll(mid, 8, 1), jnp.where(
                lane < 24, pltpu.roll(lo, 16, 1), jnp.where(lane < 32, 1.0, 0.0))))
            for n, (g, e) in enumerate(heads):
                extra = pltpu.roll(pieces, shift(g, e), 1).astype(BF16)
                kaug_ref[n, rows, :] = jnp.where(own[e], k_ref[0, rows, cols(g)], extra)
                v = v_ref[0, rows, cols(g)]
                vaug_ref[n, rows, :] = jnp.where(own[e], v, jnp.ones_like(v))
            return c[t - 1:t, :], k_max

        zero_row = jnp.zeros((1, LANES), F32)
        lax.fori_loop(0, seq // t, build, (zero_row, zero_row))

    cq0 = LOG2E * c_ref[pl.ds(pl.multiple_of(i * t, t), SUBLANES), :][0:1, :]
    qa = []
    for g, e in heads:
        ones = (lane1 == x0(e)) | (lane1 == x0(e) + 8) | (lane1 == x0(e) + 16)
        extra = jnp.where(ones, 1.0, jnp.where(
            lane1 == x0(e) + 24, pltpu.roll(cq0, (shift(g, e) + 24) % LANES, 1), 0.0))
        qa.append(jnp.where(own[e], q_ref[0, :, cols(g)], extra.astype(BF16)))

    def tile(j, carry, masked, ids):
        m, acc = carry
        rows = pl.ds(pl.multiple_of(j * t, t), t)
        new_m, new_acc = [], []
        for pos, n in enumerate(ids):
            s = lax.dot_general(qa[n], kaug_ref[n, rows, :], (((1,), (1,)), ((), ())),
                                preferred_element_type=F32)
            if masked:
                r = lax.broadcasted_iota(jnp.int32, (t, t), 0)
                cidx = lax.broadcasted_iota(jnp.int32, (t, t), 1)
                s = jnp.where(cidx <= r, s, NEG_BIG)
            m_new = jnp.maximum(m[pos], jnp.max(s, axis=-1, keepdims=True))
            alpha = jnp.exp2(m[pos] - m_new)
            p = jnp.exp2(s - m_new).astype(BF16)
            new_m.append(m_new)
            new_acc.append(acc[pos] * alpha + jnp.dot(p, vaug_ref[n, rows, :],
                                                      preferred_element_type=F32))
        return tuple(new_m), tuple(new_acc)

    n_tiles = seq // t
    tile_id = lax.broadcasted_iota(jnp.int32, (n_tiles, LANES), 0)
    cq0_used = cq0.astype(BF16).astype(F32)
    upper = (jnp.sqrt(max_sq_norms(q_ref, slice(None)) * knorm_ref[...])
             + (cq0_used - LOG2E * clast_ref[...]))

    def needed(j, m, ids):
        row_min = jnp.zeros((1, LANES), F32)
        for pos, n in enumerate(ids):
            row_min = jnp.where(lane1 == n, jnp.min(m[pos], axis=0, keepdims=True), row_min)
        upper_j = jnp.sum(jnp.where(tile_id == j, upper, 0.0), axis=0, keepdims=True)
        hit = (upper_j >= row_min - FOX_SKIP_LOG2) & (lane1 >= ids[0]) & (lane1 <= ids[-1])
        return (j >= 0) & (jnp.max(jnp.where(hit, 1.0, 0.0)) > 0.0)

    acc = []
    for first in range(0, len(heads), FOX_GROUP):
        ids = list(range(first, first + FOX_GROUP))
        init = ((jnp.full((t, 1), NEG_BIG, F32),) * FOX_GROUP,
                (jnp.zeros((t, LANES), F32),) * FOX_GROUP)
        m, group_acc = tile(i, init, True, ids)

        def walk(state, ids=ids):
            j, _, m, group_acc = state
            m, group_acc = tile(j, (m, group_acc), False, ids)
            return j - 1, needed(j - 1, m, ids), m, group_acc

        state = lax.while_loop(lambda state: state[1], walk,
                               (i - 1, needed(i - 1, m, ids), m, group_acc))
        acc.extend(state[3])
    outs = [a / pltpu.roll(a, HEAD_DIM, 1) for a in acc]
    for g in range(n_pairs):
        o_ref[0, :, cols(g)] = jnp.where(own[0], outs[2 * g], outs[2 * g + 1]).astype(o_ref.dtype)


def _fox_attention(qkv, flog, bf_pad, layer):
    batch, seq, _ = qkv.shape
    n_heads = D_FOX // HEAD_DIM
    once = pl.Buffered(1)
    return pl.pallas_call(
        _fox_kernel,
        grid=(batch, seq // T_FOX),
        in_specs=[
            pl.BlockSpec((1, T_FOX, D_FOX), lambda b, i: (b, i, 0)),
            pl.BlockSpec((1, seq, D_FOX), lambda b, i: (b, 0, 1), pipeline_mode=once),
            pl.BlockSpec((1, seq, D_FOX), lambda b, i: (b, 0, 2), pipeline_mode=once),
            pl.BlockSpec((1, seq, LANES), lambda b, i: (b, 0, 0), pipeline_mode=once),
            pl.BlockSpec((1, 1, LANES), lambda b, i: (layer, 0, 0)),
        ],
        out_specs=pl.BlockSpec((1, T_FOX, D_FOX), lambda b, i: (b, i, 0)),
        out_shape=jax.ShapeDtypeStruct((batch, seq, D_FOX), BF16),
        scratch_shapes=[pltpu.VMEM((n_heads, seq, LANES), BF16),
                        pltpu.VMEM((n_heads, seq, LANES), BF16),
                        pltpu.VMEM((seq, LANES), F32),
                        pltpu.VMEM((seq // T_FOX, LANES), F32),
                        pltpu.VMEM((seq // T_FOX, LANES), F32)],
        compiler_params=_cparams(("parallel", "arbitrary")),
        name="fox_attention",
    )(qkv, qkv, qkv, flog, bf_pad)


def _ret_kernel(q_ref, k_ref, v_ref, cos_ref, sin_ref, lg_ref, gn_ref, o_ref,
                state_ref, dmat_ref, wq_ref, wk_ref):
    t = T_RET
    n_pairs = D_RET // LANES
    scale = 1.0 / math.sqrt(HEAD_DIM)
    lane = lax.broadcasted_iota(jnp.int32, (t, LANES), 1)
    first = lane < HEAD_DIM
    low_half = (lane % HEAD_DIM) < (HEAD_DIM // 2)
    cols = lambda p: slice(p * LANES, (p + 1) * LANES)

    @pl.when((pl.program_id(0) == 0) & (pl.program_id(1) == 0))
    def _():
        pos = lax.broadcasted_iota(jnp.int32, (t, 1), 0).astype(F32)
        r = lax.broadcasted_iota(jnp.int32, (t, t), 0)
        s = lax.broadcasted_iota(jnp.int32, (t, t), 1)
        dist = jnp.abs(r - s).astype(F32)
        visible = (s // CHUNK) <= (r // CHUNK)
        for h in range(RET_HEADS):
            lg = lg_ref[:, h * HEAD_DIM:h * HEAD_DIM + 1]
            dmat_ref[h] = jnp.where(visible, jnp.exp(lg * dist) * scale, 0.0)
        wq_ref[...] = jnp.exp(lg_ref[...] * (pos + 1.0)) * scale
        wk_ref[...] = jnp.exp(lg_ref[...] * (t - 1.0 - pos))

    @pl.when(pl.program_id(1) == 0)
    def _():
        state_ref[...] = jnp.zeros_like(state_ref)

    rb = lax.broadcasted_iota(jnp.int32, (LANES, LANES), 0)
    cb = lax.broadcasted_iota(jnp.int32, (LANES, LANES), 1)
    same_head = (rb // HEAD_DIM) == (cb // HEAD_DIM)

    def head_mean(x):
        s0 = jnp.sum(jnp.where(first, x, 0.0), axis=-1, keepdims=True)
        s1 = jnp.sum(jnp.where(first, 0.0, x), axis=-1, keepdims=True)
        return jnp.where(first, s0, s1) * (1.0 / HEAD_DIM)

    for sub in range(RET_BLK // t):
        rows = slice(sub * t, (sub + 1) * t)
        cos = cos_ref[rows, :]
        sin = sin_ref[rows, :]

        def rotary(x):
            swapped = jnp.where(low_half, pltpu.roll(x, LANES - HEAD_DIM // 2, 1),
                                pltpu.roll(x, HEAD_DIM // 2, 1))
            return x * cos + swapped * sin

        for p in range(n_pairs):
            q = rotary(q_ref[0, rows, cols(p)].astype(F32))
            k = rotary(k_ref[0, rows, cols(p)].astype(F32))
            vb = v_ref[0, rows, cols(p)]
            kb = k.astype(BF16)
            zero = jnp.zeros_like(q)
            inner = []
            for e in range(2):
                qe = jnp.where(first, q, zero) if e == 0 else jnp.where(first, zero, q)
                sc = lax.dot_general(qe.astype(BF16), kb, (((1,), (1,)), ((), ())),
                                     preferred_element_type=F32) * dmat_ref[2 * p + e]
                inner.append(jnp.dot(sc.astype(BF16), vb, preferred_element_type=F32))
            state = state_ref[p]
            o = jnp.where(first, inner[0], inner[1]) + jnp.dot(
                (q * wq_ref[:, cols(p)]).astype(BF16), state.astype(BF16),
                preferred_element_type=F32)
            upd = lax.dot_general((k * wk_ref[:, cols(p)]).astype(BF16), vb,
                                  (((0,), (0,)), ((), ())), preferred_element_type=F32)
            decay = jnp.exp(lg_ref[:, cols(p)] * float(t))
            state_ref[p] = jnp.where(same_head, state * decay + upd, 0.0)
            d = o - head_mean(o)
            var = head_mean(d * d)
            o_ref[0, rows, cols(p)] = (d * lax.rsqrt(var + EPS) * gn_ref[0, :, cols(p)]).astype(
                o_ref.dtype)


def _retention(rest, cos_t, sin_t, lg, gn_w, layer):
    batch, seq, _ = rest.shape
    n_pairs = D_RET // LANES
    tok = lambda blk: pl.BlockSpec((1, RET_BLK, D_RET), lambda b, i, blk=blk: (b, i, blk))
    fixed = pl.BlockSpec((1, D_RET), lambda b, i: (0, 0))
    table = pl.BlockSpec((RET_BLK, LANES), lambda b, i: (i, 0))
    return pl.pallas_call(
        _ret_kernel,
        grid=(batch, seq // RET_BLK),
        in_specs=[tok(4), tok(5), tok(6), table, table, fixed,
                  pl.BlockSpec((1, 1, D_RET), lambda b, i: (layer, 0, 0))],
        out_specs=pl.BlockSpec((1, RET_BLK, D_RET), lambda b, i: (b, i, 0)),
        out_shape=jax.ShapeDtypeStruct((batch, seq, D_RET), BF16),
        scratch_shapes=[
            pltpu.VMEM((n_pairs, LANES, LANES), F32),
            pltpu.VMEM((RET_HEADS, T_RET, T_RET), F32),
            pltpu.VMEM((T_RET, D_RET), F32),
            pltpu.VMEM((T_RET, D_RET), F32),
        ],
        compiler_params=_cparams(("arbitrary", "arbitrary")),
        name="retention",
    )(rest, rest, rest, cos_t, sin_t, lg, gn_w)


def _s5_kernel(u_ref, are_ref, aim_ref, ldt_ref, br_ref, bi_ref, cr_ref, ci_ref, d_ref, wg_ref,
               o_ref, wb_ref, wc_ref, lam_ref, lhs_ref, x_ref, z_ref, state_ref):
    nb = u_ref.shape[0]
    t5 = S5_T
    win = D_S5 // SUBLANES

    @pl.when(pl.program_id(0) == 0)
    def _():
        dt = jnp.exp(ldt_ref[0])
        ar = are_ref[0]
        ai = aim_ref[0]
        mag = jnp.exp(ar * dt)
        lr = mag * jnp.cos(ai * dt)
        li = mag * jnp.sin(ai * dt)
        den = ar * ar + ai * ai
        fr = ((lr - 1.0) * ar + li * ai) / den
        fi = (li * ar - (lr - 1.0) * ai) / den
        lam_ref[0] = lr
        lam_ref[1] = li
        for j in range(SUBLANES):
            rows = slice(win * j, win * (j + 1))
            f_r = fr[j:j + 1, :]
            f_i = fi[j:j + 1, :]
            b_r = br_ref[0, rows, :]
            b_i = bi_ref[0, rows, :]
            wb_ref[rows, :LANES] = (f_r * b_r - f_i * b_i).astype(BF16)
            wb_ref[rows, LANES:] = (f_r * b_i + f_i * b_r).astype(BF16)
        wc_ref[:LANES, :] = cr_ref[0].astype(BF16)
        wc_ref[LANES:, :] = (-ci_ref[0]).astype(BF16)
        state_ref[...] = jnp.zeros_like(state_ref)

    sub = lax.broadcasted_iota(jnp.int32, (2 * SUBLANES, D_S5), 0)
    keep = sub % SUBLANES == lax.broadcasted_iota(jnp.int32, (2 * SUBLANES, D_S5), 1) // win
    first = sub < SUBLANES

    def expand(tt, _):
        for b in range(nb):
            tile = u_ref[b, pl.ds(pl.multiple_of(tt * SUBLANES, SUBLANES), SUBLANES), :]
            for s in range(0, SUBLANES, 2):
                two = jnp.where(first,
                                jnp.broadcast_to(tile[s:s + 1, :], (2 * SUBLANES, D_S5)),
                                jnp.broadcast_to(tile[s + 1:s + 2, :], (2 * SUBLANES, D_S5)))
                dst = pl.ds(pl.multiple_of((tt * SUBLANES + s) * SUBLANES, 2 * SUBLANES),
                            2 * SUBLANES)
                lhs_ref[b, dst, :] = jnp.where(keep, two, 0.0).astype(BF16)
        return 0

    lax.fori_loop(0, t5 // SUBLANES, expand, 0)
    for b in range(nb):
        x_ref[b] = jnp.dot(lhs_ref[b], wb_ref[...], preferred_element_type=F32)

    lr = lam_ref[0]
    li = lam_ref[1]

    def scan(t, carry):
        rows = pl.ds(pl.multiple_of(t * SUBLANES, SUBLANES), SUBLANES)
        new = []
        for b in range(nb):
            xr, xi = carry[b]
            nr = lr * xr - li * xi + x_ref[b, rows, :LANES]
            ni = lr * xi + li * xr + x_ref[b, rows, LANES:]
            x_ref[b, rows, :LANES] = nr
            x_ref[b, rows, LANES:] = ni
            new.append((nr, ni))
        return tuple(new)

    init = tuple((state_ref[b, :, :LANES], state_ref[b, :, LANES:]) for b in range(nb))
    final = lax.fori_loop(0, t5, scan, init, unroll=2)
    for b in range(nb):
        state_ref[b, :, :LANES] = final[b][0]
        state_ref[b, :, LANES:] = final[b][1]

    window = lax.broadcasted_iota(jnp.int32, (t5, LANES), 1) // win
    for b in range(nb):
        z = jnp.dot(x_ref[b].astype(BF16), wc_ref[...], preferred_element_type=F32)
        z_ref[b, 0] = z[:, :LANES]
        z_ref[b, 1] = z[:, LANES:]
    for b in range(nb):
        halves = []
        for hh in range(2):
            per = SUBLANES // 2
            y = z_ref[b, hh, pl.ds(per * hh + per - 1, t5, stride=SUBLANES), :]
            for jj in range(per - 2, -1, -1):
                y = jnp.where(window == jj,
                              z_ref[b, hh, pl.ds(per * hh + jj, t5, stride=SUBLANES), :], y)
            halves.append(y)
        u = u_ref[b]
        y = jax.nn.gelu(jnp.concatenate(halves, axis=1) + d_ref[0] * u)
        gate = jnp.dot(y.astype(BF16), wg_ref[0], preferred_element_type=F32)
        o_ref[b] = (y * jax.nn.sigmoid(gate)).astype(o_ref.dtype)


def _s5(su, prm, layer):
    batch, seq, _ = su.shape
    fixed = lambda i: (layer, 0, 0)
    lam = pl.BlockSpec((1, SUBLANES, LANES), fixed)
    rows8 = SUBLANES * S5_T
    return pl.pallas_call(
        _s5_kernel,
        grid=(seq // S5_T,),
        in_specs=[
            pl.BlockSpec((batch, S5_T, D_S5), lambda i: (0, i, 0)),
            lam, lam, lam,
            pl.BlockSpec((1, D_S5, LANES), fixed), pl.BlockSpec((1, D_S5, LANES), fixed),
            pl.BlockSpec((1, LANES, D_S5), fixed), pl.BlockSpec((1, LANES, D_S5), fixed),
            pl.BlockSpec((1, 1, D_S5), fixed),
            pl.BlockSpec((1, D_S5, D_S5), fixed),
        ],
        out_specs=pl.BlockSpec((batch, S5_T, D_S5), lambda i: (0, i, 0)),
        out_shape=jax.ShapeDtypeStruct((batch, seq, D_S5), BF16),
        scratch_shapes=[
            pltpu.VMEM((D_S5, 2 * LANES), BF16),
            pltpu.VMEM((2 * LANES, D_S5), BF16),
            pltpu.VMEM((2, SUBLANES, LANES), F32),
            pltpu.VMEM((batch, rows8, D_S5), BF16),
            pltpu.VMEM((batch, rows8, 2 * LANES), F32),
            pltpu.VMEM((batch, 2, rows8, LANES), F32),
            pltpu.VMEM((batch, SUBLANES, 2 * LANES), F32),
        ],
        compiler_params=_cparams(("arbitrary",)),
        name="s5",
    )(su, prm["a_re"], prm["a_im"], prm["log_dt"], prm["b_re"], prm["b_im"],
      prm["c_re"], prm["c_im"], prm["d"], prm["w_glu"])


def _outproj_kernel(yf_ref, ys_ref, yr_ref, gate_ref, x_ref, wout_ref, g_ref, *refs, final):
    o_refs, w_ref = refs[:-1], refs[-1]

    @pl.when(pl.program_id(0) == 0)
    def _():
        w_ref[...] = wout_ref[0].astype(BF16)

    g = gate_ref[...].astype(F32)
    g = g * jax.nn.sigmoid(g)
    acc = x_ref[...]
    off = 0
    for y_ref in (yf_ref, ys_ref, yr_ref):
        width = y_ref.shape[-1]
        y = (y_ref[...] * g[:, off:off + width]).astype(BF16)
        acc = acc + jnp.dot(y, w_ref[off:off + width, :], preferred_element_type=F32)
        off += width
    normed = _rms_norm(acc, g_ref[0])
    if final:
        o_refs[0][...] = normed
    else:
        o_refs[0][...] = acc
        o_refs[1][...] = normed.astype(o_refs[1].dtype)


def _outproj(y_fox, y_s5, y_ret, rest, x2, w_out, norm_g, layer, final):
    m = x2.shape[0]
    row = lambda i: (i, 0)
    tile = pl.BlockSpec((TM_PROJ, D_MODEL), row)
    x_shape = jax.ShapeDtypeStruct((m, D_MODEL), F32)
    return pl.pallas_call(
        functools.partial(_outproj_kernel, final=final),
        grid=(m // TM_PROJ,),
        in_specs=[
            pl.BlockSpec((TM_PROJ, D_FOX), row),
            pl.BlockSpec((TM_PROJ, D_S5), row),
            pl.BlockSpec((TM_PROJ, D_RET), row),
            tile,
            tile,
            pl.BlockSpec((1, D_MODEL, D_MODEL), lambda i: (layer, 0, 0),
                         pipeline_mode=pl.Buffered(1)),
            pl.BlockSpec((1, 1, D_MODEL), lambda i: (layer + 1, 0, 0)),
        ],
        out_specs=tile if final else [tile, tile],
        out_shape=x_shape if final else [x_shape, jax.ShapeDtypeStruct((m, D_MODEL), BF16)],
        scratch_shapes=[pltpu.VMEM((D_MODEL, D_MODEL), BF16)],
        compiler_params=_cparams(("arbitrary",)),
        name="outproj_final" if final else "outproj",
    )(y_fox, y_s5, y_ret, rest, x2, w_out, norm_g)


def _rotary_tables(seq):
    half = HEAD_DIM // 2
    freqs = ROPE_BASE ** (-jnp.arange(half, dtype=F32) / half)
    ang = jnp.arange(seq, dtype=F32)[:, None] * freqs[None, :]
    cos = jnp.cos(ang)
    sin = jnp.sin(ang)
    cos_t = jnp.tile(cos, (1, LANES // half))
    sin_t = jnp.tile(jnp.concatenate([-sin, sin], axis=-1), (1, LANES // HEAD_DIM))
    return cos_t, sin_t


def _s5_params(a_re, a_im, b_re, b_im, c_re, c_im, d, log_dt, w_glu):
    depth = a_re.shape[0]
    pair = jnp.eye(2, dtype=F32)
    half = S5_GROUPS // 2
    pack_b = lambda b: jnp.einsum(
        "ljnph,kn->ljkhnp", b.reshape(depth, half, 2, S5_STATE, S5_GROUP_CH), pair
    ).reshape(depth, D_S5, LANES)
    pack_c = lambda c: jnp.einsum(
        "ljnhp,kn->lnpjkh", c.reshape(depth, half, 2, S5_GROUP_CH, S5_STATE), pair
    ).reshape(depth, LANES, D_S5)
    tile = lambda v: v.reshape(depth, SUBLANES, LANES)
    return {
        "a_re": tile(a_re), "a_im": tile(a_im),
        "log_dt": tile(jnp.repeat(log_dt, S5_STATE, axis=-1)),
        "b_re": pack_b(b_re), "b_im": pack_b(b_im), "c_re": pack_c(c_re), "c_im": pack_c(c_im),
        "d": d.reshape(depth, 1, D_S5), "w_glu": w_glu.astype(BF16),
    }


def kernel(x, norm_w, w_in, fox_b_f, s5_a_re, s5_a_im, s5_b_re, s5_b_im, s5_c_re, s5_c_im,
           s5_d, s5_log_dt, s5_w_glu, ret_gn_w, w_out, final_norm_w):
    batch, seq, _ = x.shape
    depth = w_in.shape[0]
    m = batch * seq
    cos_t, sin_t = _rotary_tables(seq)
    log_gamma = jnp.log1p(-(2.0 ** (-5.0 - jnp.arange(RET_HEADS, dtype=F32))))
    lg = jnp.repeat(log_gamma, HEAD_DIM).reshape(1, D_RET)
    norm_g = jnp.concatenate([norm_w, final_norm_w[None]], axis=0).reshape(depth + 1, 1, D_MODEL)

    bf_pad = jnp.pad(fox_b_f, ((0, 0), (0, LANES - FOX_HEADS))).reshape(depth, 1, LANES)
    gn_w = ret_gn_w.reshape(depth, 1, D_RET)
    s5_prm = _s5_params(s5_a_re, s5_a_im, s5_b_re, s5_b_im, s5_c_re, s5_c_im, s5_d, s5_log_dt,
                        s5_w_glu)

    w_in_t = jnp.swapaxes(w_in, 1, 2)
    x2 = x.reshape(m, D_MODEL)
    h = x2
    for l in range(depth):
        qkv, su, rest, flog = _inproj(h, norm_g, w_in_t, l)
        qkv = qkv.reshape(batch, seq, N_QKV)
        rest3 = rest.reshape(batch, seq, N_REST - D_S5)
        y_fox = _fox_attention(qkv, flog.reshape(batch, seq, LANES), bf_pad, l)
        y_s5 = _s5(su.reshape(batch, seq, D_S5), s5_prm, l)
        y_ret = _retention(rest3, cos_t, sin_t, lg, gn_w, l)
        out = _outproj(y_fox.reshape(m, D_FOX), y_s5.reshape(m, D_S5), y_ret.reshape(m, D_RET),
                       rest, x2, w_out, norm_g, l, final=(l == depth - 1))
        if l < depth - 1:
            x2, h = out
    return out.reshape(batch, seq, D_MODEL)
```

```python
import functools
import math

import jax
import jax.numpy as jnp
from jax import lax
from jax.experimental import pallas as pl
from jax.experimental.pallas import tpu as pltpu

F32 = jnp.float32
BF16 = jnp.bfloat16

D_MODEL = 1024
HEAD_DIM = 64
CHUNK = 64
D_FOX = 512
FOX_HEADS = 8
D_S5 = 256
S5_GROUPS = 16
S5_GROUP_CH = 16
S5_STATE = 64
D_RET = 256
RET_HEADS = 4
ROPE_BASE = 10000.0
EPS = 1e-6

LANES = 128
SUBLANES = 8
N_QKV = 3 * D_FOX
N_REST = D_S5 + D_MODEL + 3 * D_RET
N_ALL = N_QKV + N_REST + 128
D_IN_PROJ = N_QKV + FOX_HEADS + N_REST
VMEM_LIMIT = 56 * 1024 * 1024

TM_PROJ = 512
T_FOX = 512
FOX_GROUP = 2
T_RET = 256
RET_BLK = 512
S5_T = 256
NEG_BIG = -1e30
FOX_SKIP_LOG2 = 40.0
FOX_NORM_SLACK = 1.02
LOG2E = math.log2(math.e)
Q_SCALE = LOG2E / math.sqrt(HEAD_DIM)


def _cparams(sem, flags=None):
    return pltpu.CompilerParams(dimension_semantics=sem, vmem_limit_bytes=VMEM_LIMIT, flags=flags)


def _rms_norm(x, g):
    ms = jnp.mean(x * x, axis=-1, keepdims=True)
    return x * lax.rsqrt(ms + EPS) * g


def _inproj_kernel(order_ref, h_ref, g_ref, wt_ref, oa_ref, su_ref, ob_ref, fl_ref, w_ref, *,
                   normalize):
    @pl.when(pl.program_id(0) == 0)
    def _():
        su_lo = N_QKV + FOX_HEADS
        rqkv_lo = su_lo + D_S5
        gate_lo = rqkv_lo + 3 * D_RET

        def copy_rows(dst, src, size):
            w_ref[dst:dst + size, :] = wt_ref[0, src, :].astype(BF16)

        def copy_heads(dst, src):
            for n in range(FOX_HEADS):
                start = pl.multiple_of(src + order_ref[n] * HEAD_DIM, HEAD_DIM)
                copy_rows(dst + n * HEAD_DIM, pl.ds(start, HEAD_DIM), HEAD_DIM)

        def copy_plain(dst, lo, hi):
            for start in range(lo, hi, 256):
                size = min(256, hi - start)
                copy_rows(dst + start - lo, slice(start, start + size), size)

        for part in range(3):
            copy_heads(part * D_FOX, part * D_FOX)
        copy_plain(N_QKV, su_lo, rqkv_lo)
        copy_heads(N_QKV + D_S5, gate_lo)
        copy_plain(N_QKV + D_S5 + D_FOX, gate_lo + D_FOX, D_IN_PROJ)
        copy_plain(N_QKV + D_S5 + D_MODEL, rqkv_lo, gate_lo)
        flog = [wt_ref[0, pl.ds(N_QKV + order_ref[n], 1), :] for n in range(FOX_HEADS)]
        flog.append(jnp.zeros((LANES - FOX_HEADS, D_MODEL), F32))
        w_ref[N_QKV + N_REST:, :] = jnp.concatenate(flog, axis=0).astype(BF16)

    h = h_ref[...]
    if normalize:
        h = _rms_norm(h, g_ref[0]).astype(BF16)
    nt = (((1,), (1,)), ((), ()))
    qkv = lax.dot_general(h, w_ref[:N_QKV, :], nt, preferred_element_type=F32)
    oa_ref[:, :D_FOX] = (qkv[:, :D_FOX] * Q_SCALE).astype(oa_ref.dtype)
    oa_ref[:, D_FOX:] = qkv[:, D_FOX:].astype(oa_ref.dtype)
    rest = lax.dot_general(h, w_ref[N_QKV:N_QKV + N_REST, :], nt, preferred_element_type=F32)
    su_ref[...] = rest[:, :D_S5]
    ob_ref[...] = rest[:, D_S5:].astype(ob_ref.dtype)
    fl_ref[...] = lax.dot_general(h, w_ref[N_QKV + N_REST:, :], nt, preferred_element_type=F32)


def _inproj(order, h, norm_g, w_in_t, layer):
    m = h.shape[0]
    row = lambda i, order: (i, 0)
    grid_spec = pltpu.PrefetchScalarGridSpec(
        num_scalar_prefetch=1,
        grid=(m // TM_PROJ,),
        in_specs=[
            pl.BlockSpec((TM_PROJ, D_MODEL), row),
            pl.BlockSpec((1, 1, D_MODEL), lambda i, order: (layer, 0, 0)),
            pl.BlockSpec((1, D_IN_PROJ, D_MODEL), lambda i, order: (layer, 0, 0),
                         pipeline_mode=pl.Buffered(1)),
        ],
        out_specs=[
            pl.BlockSpec((TM_PROJ, N_QKV), row),
            pl.BlockSpec((TM_PROJ, D_S5), row),
            pl.BlockSpec((TM_PROJ, N_REST - D_S5), row),
            pl.BlockSpec((TM_PROJ, LANES), row),
        ],
        scratch_shapes=[pltpu.VMEM((N_ALL, D_MODEL), BF16)],
    )
    return pl.pallas_call(
        functools.partial(_inproj_kernel, normalize=(layer == 0)),
        grid_spec=grid_spec,
        out_shape=[
            jax.ShapeDtypeStruct((m, N_QKV), BF16),
            jax.ShapeDtypeStruct((m, D_S5), F32),
            jax.ShapeDtypeStruct((m, N_REST - D_S5), BF16),
            jax.ShapeDtypeStruct((m, LANES), F32),
        ],
        compiler_params=_cparams(("arbitrary",)),
        name="inproj",
    )(order, h, norm_g, w_in_t)


def _fox_kernel(q_ref, k_ref, v_ref, fl_ref, bf_ref, o_ref, kaug_ref, vaug_ref, c_ref, clast_ref,
                knorm_ref):
    t = T_FOX
    i = pl.program_id(1)
    seq = k_ref.shape[1]
    lane = lax.broadcasted_iota(jnp.int32, (t, LANES), 1)
    lane1 = lane[0:1, :]
    own = (lane < HEAD_DIM, lane >= HEAD_DIM)
    n_pairs = D_FOX // LANES
    heads = [(g, e) for g in range(n_pairs) for e in range(2)]
    cols = lambda g: slice(g * LANES, (g + 1) * LANES)
    x0 = lambda e: HEAD_DIM if e == 0 else 0
    shift = lambda g, e: (x0(e) - (2 * g + e)) % LANES

    half_id = lax.broadcasted_iota(jnp.int32, (LANES, LANES), 0) // HEAD_DIM
    same_half = (half_id == lax.broadcasted_iota(jnp.int32, (LANES, LANES), 1) // HEAD_DIM)
    head_sum = jnp.where(same_half, 1.0, 0.0).astype(BF16)

    def max_sq_norms(ref, rows):
        out = jnp.zeros((1, LANES), F32)
        for g in range(n_pairs):
            x = ref[0, rows, cols(g)]
            sums = jnp.dot(x * x, head_sum, preferred_element_type=F32)
            top = jnp.max(sums, axis=0, keepdims=True) * FOX_NORM_SLACK
            out = jnp.where(lane1 == 2 * g, top,
                            jnp.where(lane1 == 2 * g + 1, pltpu.roll(top, HEAD_DIM, 1), out))
        return out

    @pl.when(i == 0)
    def _():
        r = lax.broadcasted_iota(jnp.int32, (t, t), 0)
        s = lax.broadcasted_iota(jnp.int32, (t, t), 1)
        tri = jnp.where(s <= r, 1.0, 0.0).astype(BF16)

        def build(j, carried):
            carry, k_max = carried
            rows = pl.ds(pl.multiple_of(j * t, t), t)
            z = fl_ref[0, rows, :] + bf_ref[0]
            logf = jnp.minimum(z, 0.0) - jnp.log1p(jnp.exp(-jnp.abs(z)))
            top = logf.astype(BF16)
            rem = logf - top.astype(F32)
            middle = rem.astype(BF16)
            bottom = (rem - middle.astype(F32)).astype(BF16)
            c = carry + sum(jnp.dot(tri, piece, preferred_element_type=F32)
                            for piece in (bottom, middle, top))
            c_ref[rows, :] = c
            clast_ref[pl.ds(j, 1), :] = c[t - 1:t, :]
            k_max = jnp.maximum(k_max, max_sq_norms(k_ref, rows))
            knorm_ref[pl.ds(j, 1), :] = k_max
            b = -LOG2E * c
            hi = b.astype(BF16).astype(F32)
            mid = (b - hi).astype(BF16).astype(F32)
            lo = b - hi - mid
            pieces = jnp.where(lane < 8, hi, jnp.where(lane < 16, pltpu.roll(mid, 8, 1), jnp.where(
                lane < 24, pltpu.roll(lo, 16, 1), jnp.where(lane < 32, 1.0, 0.0))))
            for n, (g, e) in enumerate(heads):
                extra = pltpu.roll(pieces, shift(g, e), 1).astype(BF16)
                kaug_ref[n, rows, :] = jnp.where(own[e], k_ref[0, rows, cols(g)], extra)
                v = v_ref[0, rows, cols(g)]
                vaug_ref[n, rows, :] = jnp.where(own[e], v, jnp.ones_like(v))
            return c[t - 1:t, :], k_max

        zero_row = jnp.zeros((1, LANES), F32)
        lax.fori_loop(0, seq // t, build, (zero_row, zero_row))

    cq0 = LOG2E * c_ref[pl.ds(pl.multiple_of(i * t, t), SUBLANES), :][0:1, :]
    qa = []
    for g, e in heads:
        ones = (lane1 == x0(e)) | (lane1 == x0(e) + 8) | (lane1 == x0(e) + 16)
        extra = jnp.where(ones, 1.0, jnp.where(
            lane1 == x0(e) + 24, pltpu.roll(cq0, (shift(g, e) + 24) % LANES, 1), 0.0))
        qa.append(jnp.where(own[e], q_ref[0, :, cols(g)], extra.astype(BF16)))

    def tile(j, carry, masked, ids):
        m, acc = carry
        rows = pl.ds(pl.multiple_of(j * t, t), t)
        new_m, new_acc = [], []
        for pos, n in enumerate(ids):
            s = lax.dot_general(qa[n], kaug_ref[n, rows, :], (((1,), (1,)), ((), ())),
                                preferred_element_type=F32)
            if masked:
                r = lax.broadcasted_iota(jnp.int32, (t, t), 0)
                cidx = lax.broadcasted_iota(jnp.int32, (t, t), 1)
                s = jnp.where(cidx <= r, s, NEG_BIG)
            m_new = jnp.maximum(m[pos], jnp.max(s, axis=-1, keepdims=True))
            alpha = jnp.exp2(m[pos] - m_new)
            p = jnp.exp2(s - m_new).astype(BF16)
            new_m.append(m_new)
            new_acc.append(acc[pos] * alpha + jnp.dot(p, vaug_ref[n, rows, :],
                                                      preferred_element_type=F32))
        return tuple(new_m), tuple(new_acc)

    n_tiles = seq // t
    tile_id = lax.broadcasted_iota(jnp.int32, (n_tiles, LANES), 0)
    cq0_used = cq0.astype(BF16).astype(F32)
    upper = (jnp.sqrt(max_sq_norms(q_ref, slice(None)) * knorm_ref[...])
             + (cq0_used - LOG2E * clast_ref[...]))

    def needed(j, m, ids):
        row_min = jnp.zeros((1, LANES), F32)
        for pos, n in enumerate(ids):
            row_min = jnp.where(lane1 == n, jnp.min(m[pos], axis=0, keepdims=True), row_min)
        upper_j = jnp.sum(jnp.where(tile_id == j, upper, 0.0), axis=0, keepdims=True)
        hit = (upper_j >= row_min - FOX_SKIP_LOG2) & (lane1 >= ids[0]) & (lane1 <= ids[-1])
        return (j >= 0) & (jnp.max(jnp.where(hit, 1.0, 0.0)) > 0.0)

    acc = []
    for first in range(0, len(heads), FOX_GROUP):
        ids = list(range(first, first + FOX_GROUP))
        init = ((jnp.full((t, 1), NEG_BIG, F32),) * FOX_GROUP,
                (jnp.zeros((t, LANES), F32),) * FOX_GROUP)
        m, group_acc = tile(i, init, True, ids)

        def walk(state, ids=ids):
            j, _, m, group_acc = state
            m, group_acc = tile(j, (m, group_acc), False, ids)
            return j - 1, needed(j - 1, m, ids), m, group_acc

        state = lax.while_loop(lambda state: state[1], walk,
                               (i - 1, needed(i - 1, m, ids), m, group_acc))
        acc.extend(state[3])
    outs = [a / pltpu.roll(a, HEAD_DIM, 1) for a in acc]
    for g in range(n_pairs):
        o_ref[0, :, cols(g)] = jnp.where(own[0], outs[2 * g], outs[2 * g + 1]).astype(o_ref.dtype)


def _fox_attention(qkv, flog, bf_pad, layer):
    batch, seq, _ = qkv.shape
    n_heads = D_FOX // HEAD_DIM
    once = pl.Buffered(1)
    return pl.pallas_call(
        _fox_kernel,
        grid=(batch, seq // T_FOX),
        in_specs=[
            pl.BlockSpec((1, T_FOX, D_FOX), lambda b, i: (b, i, 0)),
            pl.BlockSpec((1, seq, D_FOX), lambda b, i: (b, 0, 1), pipeline_mode=once),
            pl.BlockSpec((1, seq, D_FOX), lambda b, i: (b, 0, 2), pipeline_mode=once),
            pl.BlockSpec((1, seq, LANES), lambda b, i: (b, 0, 0), pipeline_mode=once),
            pl.BlockSpec((1, 1, LANES), lambda b, i: (layer, 0, 0)),
        ],
        out_specs=pl.BlockSpec((1, T_FOX, D_FOX), lambda b, i: (b, i, 0)),
        out_shape=jax.ShapeDtypeStruct((batch, seq, D_FOX), BF16),
        scratch_shapes=[pltpu.VMEM((n_heads, seq, LANES), BF16),
                        pltpu.VMEM((n_heads, seq, LANES), BF16),
                        pltpu.VMEM((seq, LANES), F32),
                        pltpu.VMEM((seq // T_FOX, LANES), F32),
                        pltpu.VMEM((seq // T_FOX, LANES), F32)],
        compiler_params=_cparams(("parallel", "arbitrary")),
        name="fox_attention",
    )(qkv, qkv, qkv, flog, bf_pad)


def _ret_kernel(q_ref, k_ref, v_ref, cos_ref, sin_ref, lg_ref, gn_ref, o_ref,
                state_ref, dmat_ref, wq_ref, wk_ref):
    t = T_RET
    n_pairs = D_RET // LANES
    scale = 1.0 / math.sqrt(HEAD_DIM)
    lane = lax.broadcasted_iota(jnp.int32, (t, LANES), 1)
    first = lane < HEAD_DIM
    low_half = (lane % HEAD_DIM) < (HEAD_DIM // 2)
    cols = lambda p: slice(p * LANES, (p + 1) * LANES)

    @pl.when((pl.program_id(0) == 0) & (pl.program_id(1) == 0))
    def _():
        pos = lax.broadcasted_iota(jnp.int32, (t, 1), 0).astype(F32)
        r = lax.broadcasted_iota(jnp.int32, (t, t), 0)
        s = lax.broadcasted_iota(jnp.int32, (t, t), 1)
        dist = jnp.abs(r - s).astype(F32)
        visible = (s // CHUNK) <= (r // CHUNK)
        for h in range(RET_HEADS):
            lg = lg_ref[:, h * HEAD_DIM:h * HEAD_DIM + 1]
            dmat_ref[h] = jnp.where(visible, jnp.exp(lg * dist) * scale, 0.0)
        wq_ref[...] = jnp.exp(lg_ref[...] * (pos + 1.0)) * scale
        wk_ref[...] = jnp.exp(lg_ref[...] * (t - 1.0 - pos))

    @pl.when(pl.program_id(1) == 0)
    def _():
        state_ref[...] = jnp.zeros_like(state_ref)

    rb = lax.broadcasted_iota(jnp.int32, (LANES, LANES), 0)
    cb = lax.broadcasted_iota(jnp.int32, (LANES, LANES), 1)
    same_head = (rb // HEAD_DIM) == (cb // HEAD_DIM)

    def head_mean(x):
        s0 = jnp.sum(jnp.where(first, x, 0.0), axis=-1, keepdims=True)
        s1 = jnp.sum(jnp.where(first, 0.0, x), axis=-1, keepdims=True)
        return jnp.where(first, s0, s1) * (1.0 / HEAD_DIM)

    for sub in range(RET_BLK // t):
        rows = slice(sub * t, (sub + 1) * t)
        cos = cos_ref[rows, :]
        sin = sin_ref[rows, :]

        def rotary(x):
            swapped = jnp.where(low_half, pltpu.roll(x, LANES - HEAD_DIM // 2, 1),
                                pltpu.roll(x, HEAD_DIM // 2, 1))
            return x * cos + swapped * sin

        for p in range(n_pairs):
            q = rotary(q_ref[0, rows, cols(p)].astype(F32))
            k = rotary(k_ref[0, rows, cols(p)].astype(F32))
            vb = v_ref[0, rows, cols(p)]
            kb = k.astype(BF16)
            zero = jnp.zeros_like(q)
            inner = []
            for e in range(2):
                qe = jnp.where(first, q, zero) if e == 0 else jnp.where(first, zero, q)
                sc = lax.dot_general(qe.astype(BF16), kb, (((1,), (1,)), ((), ())),
                                     preferred_element_type=F32) * dmat_ref[2 * p + e]
                inner.append(jnp.dot(sc.astype(BF16), vb, preferred_element_type=F32))
            state = state_ref[p]
            o = jnp.where(first, inner[0], inner[1]) + jnp.dot(
                (q * wq_ref[:, cols(p)]).astype(BF16), state.astype(BF16),
                preferred_element_type=F32)
            upd = lax.dot_general((k * wk_ref[:, cols(p)]).astype(BF16), vb,
                                  (((0,), (0,)), ((), ())), preferred_element_type=F32)
            decay = jnp.exp(lg_ref[:, cols(p)] * float(t))
            state_ref[p] = jnp.where(same_head, state * decay + upd, 0.0)
            d = o - head_mean(o)
            var = head_mean(d * d)
            o_ref[0, rows, cols(p)] = (d * lax.rsqrt(var + EPS) * gn_ref[0, :, cols(p)]).astype(
                o_ref.dtype)


def _retention(rest, cos_t, sin_t, lg, gn_w, layer):
    batch, seq, _ = rest.shape
    n_pairs = D_RET // LANES
    tok = lambda blk: pl.BlockSpec((1, RET_BLK, D_RET), lambda b, i, blk=blk: (b, i, blk))
    fixed = pl.BlockSpec((1, D_RET), lambda b, i: (0, 0))
    table = pl.BlockSpec((RET_BLK, LANES), lambda b, i: (i, 0))
    return pl.pallas_call(
        _ret_kernel,
        grid=(batch, seq // RET_BLK),
        in_specs=[tok(4), tok(5), tok(6), table, table, fixed,
                  pl.BlockSpec((1, 1, D_RET), lambda b, i: (layer, 0, 0))],
        out_specs=pl.BlockSpec((1, RET_BLK, D_RET), lambda b, i: (b, i, 0)),
        out_shape=jax.ShapeDtypeStruct((batch, seq, D_RET), BF16),
        scratch_shapes=[
            pltpu.VMEM((n_pairs, LANES, LANES), F32),
            pltpu.VMEM((RET_HEADS, T_RET, T_RET), F32),
            pltpu.VMEM((T_RET, D_RET), F32),
            pltpu.VMEM((T_RET, D_RET), F32),
        ],
        compiler_params=_cparams(("arbitrary", "arbitrary")),
        name="retention",
    )(rest, rest, rest, cos_t, sin_t, lg, gn_w)


def _s5_kernel(u_ref, are_ref, aim_ref, ldt_ref, br_ref, bi_ref, cr_ref, ci_ref, d_ref, wg_ref,
               o_ref, wb_ref, wc_ref, lam_ref, lhs_ref, x_ref, z_ref, state_ref):
    nb = u_ref.shape[0]
    t5 = S5_T
    win = D_S5 // SUBLANES

    @pl.when(pl.program_id(0) == 0)
    def _():
        dt = jnp.exp(ldt_ref[0])
        ar = are_ref[0]
        ai = aim_ref[0]
        mag = jnp.exp(ar * dt)
        lr = mag * jnp.cos(ai * dt)
        li = mag * jnp.sin(ai * dt)
        den = ar * ar + ai * ai
        fr = ((lr - 1.0) * ar + li * ai) / den
        fi = (li * ar - (lr - 1.0) * ai) / den
        lam_ref[0] = lr
        lam_ref[1] = li
        for j in range(SUBLANES):
            rows = slice(win * j, win * (j + 1))
            f_r = fr[j:j + 1, :]
            f_i = fi[j:j + 1, :]
            b_r = br_ref[0, rows, :]
            b_i = bi_ref[0, rows, :]
            wb_ref[rows, :LANES] = (f_r * b_r - f_i * b_i).astype(BF16)
            wb_ref[rows, LANES:] = (f_r * b_i + f_i * b_r).astype(BF16)
        wc_ref[:LANES, :] = cr_ref[0].astype(BF16)
        wc_ref[LANES:, :] = (-ci_ref[0]).astype(BF16)
        state_ref[...] = jnp.zeros_like(state_ref)

    sub = lax.broadcasted_iota(jnp.int32, (2 * SUBLANES, D_S5), 0)
    keep = sub % SUBLANES == lax.broadcasted_iota(jnp.int32, (2 * SUBLANES, D_S5), 1) // win
    first = sub < SUBLANES

    for tt in range(t5 // SUBLANES):
        for b in range(nb):
            tile = u_ref[b, tt * SUBLANES:(tt + 1) * SUBLANES, :]
            for s in range(0, SUBLANES, 2):
                two = jnp.where(first,
                                jnp.broadcast_to(tile[s:s + 1, :], (2 * SUBLANES, D_S5)),
                                jnp.broadcast_to(tile[s + 1:s + 2, :], (2 * SUBLANES, D_S5)))
                row0 = (tt * SUBLANES + s) * SUBLANES
                lhs_ref[b, row0:row0 + 2 * SUBLANES, :] = jnp.where(keep, two, 0.0).astype(BF16)
    for b in range(nb):
        x_ref[b] = jnp.dot(lhs_ref[b], wb_ref[...], preferred_element_type=F32)

    lr = lam_ref[0]
    li = lam_ref[1]

    def scan(t, carry):
        rows = slice(t * SUBLANES, (t + 1) * SUBLANES)
        new = []
        for b in range(nb):
            xr, xi = carry[b]
            nr = lr * xr - li * xi + x_ref[b, rows, :LANES]
            ni = lr * xi + li * xr + x_ref[b, rows, LANES:]
            x_ref[b, rows, :LANES] = nr
            x_ref[b, rows, LANES:] = ni
            new.append((nr, ni))
        return tuple(new)

    final = tuple((state_ref[b, :, :LANES], state_ref[b, :, LANES:]) for b in range(nb))
    for t in range(t5):
        final = scan(t, final)
    for b in range(nb):
        state_ref[b, :, :LANES] = final[b][0]
        state_ref[b, :, LANES:] = final[b][1]

    window = lax.broadcasted_iota(jnp.int32, (t5, LANES), 1) // win
    for b in range(nb):
        z = jnp.dot(x_ref[b].astype(BF16), wc_ref[...], preferred_element_type=F32)
        z_ref[b, 0] = z[:, :LANES]
        z_ref[b, 1] = z[:, LANES:]
    for b in range(nb):
        halves = []
        for hh in range(2):
            per = SUBLANES // 2
            y = z_ref[b, hh, pl.ds(per * hh + per - 1, t5, stride=SUBLANES), :]
            for jj in range(per - 2, -1, -1):
                y = jnp.where(window == jj,
                              z_ref[b, hh, pl.ds(per * hh + jj, t5, stride=SUBLANES), :], y)
            halves.append(y)
        u = u_ref[b]
        y = jax.nn.gelu(jnp.concatenate(halves, axis=1) + d_ref[0] * u)
        gate = jnp.dot(y.astype(BF16), wg_ref[0], preferred_element_type=F32)
        o_ref[b] = (y * jax.nn.sigmoid(gate)).astype(o_ref.dtype)


def _s5(su, prm, layer):
    batch, seq, _ = su.shape
    fixed = lambda i: (layer, 0, 0)
    lam = pl.BlockSpec((1, SUBLANES, LANES), fixed)
    rows8 = SUBLANES * S5_T
    return pl.pallas_call(
        _s5_kernel,
        grid=(seq // S5_T,),
        in_specs=[
            pl.BlockSpec((batch, S5_T, D_S5), lambda i: (0, i, 0)),
            lam, lam, lam,
            pl.BlockSpec((1, D_S5, LANES), fixed), pl.BlockSpec((1, D_S5, LANES), fixed),
            pl.BlockSpec((1, LANES, D_S5), fixed), pl.BlockSpec((1, LANES, D_S5), fixed),
            pl.BlockSpec((1, 1, D_S5), fixed),
            pl.BlockSpec((1, D_S5, D_S5), fixed),
        ],
        out_specs=pl.BlockSpec((batch, S5_T, D_S5), lambda i: (0, i, 0)),
        out_shape=jax.ShapeDtypeStruct((batch, seq, D_S5), BF16),
        scratch_shapes=[
            pltpu.VMEM((D_S5, 2 * LANES), BF16),
            pltpu.VMEM((2 * LANES, D_S5), BF16),
            pltpu.VMEM((2, SUBLANES, LANES), F32),
            pltpu.VMEM((batch, rows8, D_S5), BF16),
            pltpu.VMEM((batch, rows8, 2 * LANES), F32),
            pltpu.VMEM((batch, 2, rows8, LANES), F32),
            pltpu.VMEM((batch, SUBLANES, 2 * LANES), F32),
        ],
        compiler_params=_cparams(("arbitrary",)),
        name="s5",
    )(su, prm["a_re"], prm["a_im"], prm["log_dt"], prm["b_re"], prm["b_im"],
      prm["c_re"], prm["c_im"], prm["d"], prm["w_glu"])


def _outproj_kernel(order_ref, yf_ref, ys_ref, yr_ref, gate_ref, x_ref, wout_ref, g_ref, *refs,
                    final):
    o_refs, w_ref = refs[:-1], refs[-1]

    @pl.when(pl.program_id(0) == 0)
    def _():
        for n in range(FOX_HEADS):
            src = pl.ds(pl.multiple_of(order_ref[n] * HEAD_DIM, HEAD_DIM), HEAD_DIM)
            w_ref[n * HEAD_DIM:(n + 1) * HEAD_DIM, :] = wout_ref[0, src, :].astype(BF16)
        w_ref[D_FOX:, :] = wout_ref[0, D_FOX:, :].astype(BF16)

    g = gate_ref[...].astype(F32)
    g = g * jax.nn.sigmoid(g)
    acc = x_ref[...]
    off = 0
    for y_ref in (yf_ref, ys_ref, yr_ref):
        width = y_ref.shape[-1]
        y = (y_ref[...] * g[:, off:off + width]).astype(BF16)
        acc = acc + jnp.dot(y, w_ref[off:off + width, :], preferred_element_type=F32)
        off += width
    normed = _rms_norm(acc, g_ref[0])
    if final:
        o_refs[0][...] = normed
    else:
        o_refs[0][...] = acc
        o_refs[1][...] = normed.astype(o_refs[1].dtype)


def _outproj(order, y_fox, y_s5, y_ret, rest, x2, w_out, norm_g, layer, final):
    m = x2.shape[0]
    row = lambda i, order: (i, 0)
    tile = pl.BlockSpec((TM_PROJ, D_MODEL), row)
    x_shape = jax.ShapeDtypeStruct((m, D_MODEL), F32)
    grid_spec = pltpu.PrefetchScalarGridSpec(
        num_scalar_prefetch=1,
        grid=(m // TM_PROJ,),
        in_specs=[
            pl.BlockSpec((TM_PROJ, D_FOX), row),
            pl.BlockSpec((TM_PROJ, D_S5), row),
            pl.BlockSpec((TM_PROJ, D_RET), row),
            tile,
            tile,
            pl.BlockSpec((1, D_MODEL, D_MODEL), lambda i, order: (layer, 0, 0),
                         pipeline_mode=pl.Buffered(1)),
            pl.BlockSpec((1, 1, D_MODEL), lambda i, order: (layer + 1, 0, 0)),
        ],
        out_specs=tile if final else [tile, tile],
        scratch_shapes=[pltpu.VMEM((D_MODEL, D_MODEL), BF16)],
    )
    return pl.pallas_call(
        functools.partial(_outproj_kernel, final=final),
        grid_spec=grid_spec,
        out_shape=x_shape if final else [x_shape, jax.ShapeDtypeStruct((m, D_MODEL), BF16)],
        compiler_params=_cparams(("arbitrary",)),
        name="outproj_final" if final else "outproj",
    )(order, y_fox, y_s5, y_ret, rest, x2, w_out, norm_g)


def _rotary_tables(seq):
    half = HEAD_DIM // 2
    freqs = ROPE_BASE ** (-jnp.arange(half, dtype=F32) / half)
    ang = jnp.arange(seq, dtype=F32)[:, None] * freqs[None, :]
    cos = jnp.cos(ang)
    sin = jnp.sin(ang)
    cos_t = jnp.tile(cos, (1, LANES // half))
    sin_t = jnp.tile(jnp.concatenate([-sin, sin], axis=-1), (1, LANES // HEAD_DIM))
    return cos_t, sin_t


def _s5_params(a_re, a_im, b_re, b_im, c_re, c_im, d, log_dt, w_glu):
    depth = a_re.shape[0]
    pair = jnp.eye(2, dtype=F32)
    half = S5_GROUPS // 2
    pack_b = lambda b: jnp.einsum(
        "ljnph,kn->ljkhnp", b.reshape(depth, half, 2, S5_STATE, S5_GROUP_CH), pair
    ).reshape(depth, D_S5, LANES)
    pack_c = lambda c: jnp.einsum(
        "ljnhp,kn->lnpjkh", c.reshape(depth, half, 2, S5_GROUP_CH, S5_STATE), pair
    ).reshape(depth, LANES, D_S5)
    tile = lambda v: v.reshape(depth, SUBLANES, LANES)
    return {
        "a_re": tile(a_re), "a_im": tile(a_im),
        "log_dt": tile(jnp.repeat(log_dt, S5_STATE, axis=-1)),
        "b_re": pack_b(b_re), "b_im": pack_b(b_im), "c_re": pack_c(c_re), "c_im": pack_c(c_im),
        "d": d.reshape(depth, 1, D_S5), "w_glu": w_glu.astype(BF16),
    }


def kernel(x, norm_w, w_in, fox_b_f, s5_a_re, s5_a_im, s5_b_re, s5_b_im, s5_c_re, s5_c_im,
           s5_d, s5_log_dt, s5_w_glu, ret_gn_w, w_out, final_norm_w):
    batch, seq, _ = x.shape
    depth = w_in.shape[0]
    m = batch * seq
    cos_t, sin_t = _rotary_tables(seq)
    log_gamma = jnp.log1p(-(2.0 ** (-5.0 - jnp.arange(RET_HEADS, dtype=F32))))
    lg = jnp.repeat(log_gamma, HEAD_DIM).reshape(1, D_RET)
    norm_g = jnp.concatenate([norm_w, final_norm_w[None]], axis=0).reshape(depth + 1, 1, D_MODEL)

    order = jnp.argsort(fox_b_f, axis=1).astype(jnp.int32)
    bf_sorted = jnp.take_along_axis(fox_b_f, order, axis=1)
    bf_pad = jnp.pad(bf_sorted, ((0, 0), (0, LANES - FOX_HEADS))).reshape(depth, 1, LANES)
    gn_w = ret_gn_w.reshape(depth, 1, D_RET)
    s5_prm = _s5_params(s5_a_re, s5_a_im, s5_b_re, s5_b_im, s5_c_re, s5_c_im, s5_d, s5_log_dt,
                        s5_w_glu)

    w_in_t = jnp.swapaxes(w_in, 1, 2)
    x2 = x.reshape(m, D_MODEL)
    h = x2
    for l in range(depth):
        qkv, su, rest, flog = _inproj(order[l], h, norm_g, w_in_t, l)
        qkv = qkv.reshape(batch, seq, N_QKV)
        rest3 = rest.reshape(batch, seq, N_REST - D_S5)
        y_fox = _fox_attention(qkv, flog.reshape(batch, seq, LANES), bf_pad, l)
        y_s5 = _s5(su.reshape(batch, seq, D_S5), s5_prm, l)
        y_ret = _retention(rest3, cos_t, sin_t, lg, gn_w, l)
        out = _outproj(order[l], y_fox.reshape(m, D_FOX), y_s5.reshape(m, D_S5),
                       y_ret.reshape(m, D_RET), rest, x2, w_out, norm_g, l, final=(l == depth - 1))
        if l < depth - 1:
            x2, h = out
    return out.reshape(batch, seq, D_MODEL)
```

```python
import functools
import math

import jax
import jax.numpy as jnp
from jax import lax
from jax.experimental import pallas as pl
from jax.experimental.pallas import tpu as pltpu

F32 = jnp.float32
BF16 = jnp.bfloat16

D_MODEL = 1024
HEAD_DIM = 64
CHUNK = 64
D_FOX = 512
FOX_HEADS = 8
D_S5 = 256
S5_GROUPS = 16
S5_GROUP_CH = 16
S5_STATE = 64
D_RET = 256
RET_HEADS = 4
ROPE_BASE = 10000.0
EPS = 1e-6

LANES = 128
SUBLANES = 8
N_QKV = 3 * D_FOX
N_REST = D_S5 + D_MODEL + 3 * D_RET
N_ALL = N_QKV + N_REST + 128
D_IN_PROJ = N_QKV + FOX_HEADS + N_REST
VMEM_LIMIT = 56 * 1024 * 1024

TM_PROJ = 512
T_FOX = 512
FOX_GROUP = 2
T_RET = 256
RET_BLK = 512
S5_T = 256
NEG_BIG = -1e30
FOX_SKIP_LOG2 = 40.0
FOX_NORM_SLACK = 1.02
LOG2E = math.log2(math.e)
Q_SCALE = LOG2E / math.sqrt(HEAD_DIM)


def _cparams(sem, flags=None):
    return pltpu.CompilerParams(dimension_semantics=sem, vmem_limit_bytes=VMEM_LIMIT, flags=flags)


def _rms_norm(x, g):
    ms = jnp.mean(x * x, axis=-1, keepdims=True)
    return x * lax.rsqrt(ms + EPS) * g


def _pack_kernel(order_ref, wt_ref, wout_ref, win_ref, wo_ref):
    su_lo = N_QKV + FOX_HEADS
    rqkv_lo = su_lo + D_S5
    gate_lo = rqkv_lo + 3 * D_RET

    def copy_rows(dst, src, size):
        win_ref[dst:dst + size, :] = wt_ref[0, src, :].astype(BF16)

    def copy_heads(dst, src):
        for n in range(FOX_HEADS):
            start = pl.multiple_of(src + order_ref[n] * HEAD_DIM, HEAD_DIM)
            copy_rows(dst + n * HEAD_DIM, pl.ds(start, HEAD_DIM), HEAD_DIM)

    def copy_plain(dst, lo, hi):
        for start in range(lo, hi, 256):
            size = min(256, hi - start)
            copy_rows(dst + start - lo, slice(start, start + size), size)

    for part in range(3):
        copy_heads(part * D_FOX, part * D_FOX)
    copy_plain(N_QKV, su_lo, rqkv_lo)
    copy_heads(N_QKV + D_S5, gate_lo)
    copy_plain(N_QKV + D_S5 + D_FOX, gate_lo + D_FOX, D_IN_PROJ)
    copy_plain(N_QKV + D_S5 + D_MODEL, rqkv_lo, gate_lo)
    flog = [wt_ref[0, pl.ds(N_QKV + order_ref[n], 1), :] for n in range(FOX_HEADS)]
    flog.append(jnp.zeros((LANES - FOX_HEADS, D_MODEL), F32))
    win_ref[N_QKV + N_REST:, :] = jnp.concatenate(flog, axis=0).astype(BF16)

    for n in range(FOX_HEADS):
        src = pl.ds(pl.multiple_of(order_ref[n] * HEAD_DIM, HEAD_DIM), HEAD_DIM)
        wo_ref[n * HEAD_DIM:(n + 1) * HEAD_DIM, :] = wout_ref[0, src, :].astype(BF16)
    wo_ref[D_FOX:, :] = wout_ref[0, D_FOX:, :].astype(BF16)


def _pack_weights(order, w_in_t, w_out, layer):
    grid_spec = pltpu.PrefetchScalarGridSpec(
        num_scalar_prefetch=1,
        grid=(1,),
        in_specs=[
            pl.BlockSpec((1, D_IN_PROJ, D_MODEL), lambda i, order: (layer, 0, 0)),
            pl.BlockSpec((1, D_MODEL, D_MODEL), lambda i, order: (layer, 0, 0)),
        ],
        out_specs=[
            pl.BlockSpec((N_ALL, D_MODEL), lambda i, order: (0, 0)),
            pl.BlockSpec((D_MODEL, D_MODEL), lambda i, order: (0, 0)),
        ],
    )
    return pl.pallas_call(
        _pack_kernel,
        grid_spec=grid_spec,
        out_shape=[jax.ShapeDtypeStruct((N_ALL, D_MODEL), BF16),
                   jax.ShapeDtypeStruct((D_MODEL, D_MODEL), BF16)],
        compiler_params=_cparams(("arbitrary",)),
        name="pack_weights",
    )(order, w_in_t, w_out)


def _project_in(h, w_ref, oa_ref, su_ref, ob_ref, fl_ref):
    nt = (((1,), (1,)), ((), ()))
    qkv = lax.dot_general(h, w_ref[:N_QKV, :], nt, preferred_element_type=F32)
    oa_ref[:, :D_FOX] = (qkv[:, :D_FOX] * Q_SCALE).astype(oa_ref.dtype)
    oa_ref[:, D_FOX:] = qkv[:, D_FOX:].astype(oa_ref.dtype)
    rest = lax.dot_general(h, w_ref[N_QKV:N_QKV + N_REST, :], nt, preferred_element_type=F32)
    su_ref[...] = rest[:, :D_S5]
    ob_ref[...] = rest[:, D_S5:].astype(ob_ref.dtype)
    fl_ref[...] = lax.dot_general(h, w_ref[N_QKV + N_REST:, :], nt, preferred_element_type=F32)


def _project_out(yf_ref, ys_ref, yr_ref, gate_ref, x_ref, w_ref):
    g = gate_ref[...].astype(F32)
    g = g * jax.nn.sigmoid(g)
    acc = x_ref[...]
    off = 0
    for y_ref in (yf_ref, ys_ref, yr_ref):
        width = y_ref.shape[-1]
        y = (y_ref[...] * g[:, off:off + width]).astype(BF16)
        acc = acc + jnp.dot(y, w_ref[off:off + width, :], preferred_element_type=F32)
        off += width
    return acc


def _first_kernel(x_ref, g_ref, win_ref, oa_ref, su_ref, ob_ref, fl_ref):
    h = _rms_norm(x_ref[...], g_ref[0]).astype(BF16)
    _project_in(h, win_ref, oa_ref, su_ref, ob_ref, fl_ref)


def _mid_kernel(yf_ref, ys_ref, yr_ref, gate_ref, x_ref, wo_ref, g_ref, win_ref,
                xo_ref, oa_ref, su_ref, ob_ref, fl_ref):
    acc = _project_out(yf_ref, ys_ref, yr_ref, gate_ref, x_ref, wo_ref)
    xo_ref[...] = acc
    h = _rms_norm(acc, g_ref[0]).astype(BF16)
    _project_in(h, win_ref, oa_ref, su_ref, ob_ref, fl_ref)


def _last_kernel(yf_ref, ys_ref, yr_ref, gate_ref, x_ref, wo_ref, g_ref, o_ref):
    acc = _project_out(yf_ref, ys_ref, yr_ref, gate_ref, x_ref, wo_ref)
    o_ref[...] = _rms_norm(acc, g_ref[0])


def _projections(kind, m, norm_g, norm_layer, operands):
    row = lambda i: (i, 0)
    fixed = lambda i: (0, 0)
    tile = pl.BlockSpec((TM_PROJ, D_MODEL), row)
    once = pl.Buffered(1)
    norm_spec = pl.BlockSpec((1, 1, D_MODEL), lambda i: (norm_layer, 0, 0))
    win_spec = pl.BlockSpec((N_ALL, D_MODEL), fixed, pipeline_mode=once)
    wo_spec = pl.BlockSpec((D_MODEL, D_MODEL), fixed, pipeline_mode=once)
    out_in_specs = [pl.BlockSpec((TM_PROJ, D_FOX), row), pl.BlockSpec((TM_PROJ, D_S5), row),
                    pl.BlockSpec((TM_PROJ, D_RET), row), tile, tile]
    in_out_specs = [pl.BlockSpec((TM_PROJ, N_QKV), row), pl.BlockSpec((TM_PROJ, D_S5), row),
                    pl.BlockSpec((TM_PROJ, N_REST - D_S5), row),
                    pl.BlockSpec((TM_PROJ, LANES), row)]
    in_out_shapes = [jax.ShapeDtypeStruct((m, N_QKV), BF16), jax.ShapeDtypeStruct((m, D_S5), F32),
                     jax.ShapeDtypeStruct((m, N_REST - D_S5), BF16),
                     jax.ShapeDtypeStruct((m, LANES), F32)]
    x_shape = jax.ShapeDtypeStruct((m, D_MODEL), F32)
    if kind == "first":
        body, in_specs = _first_kernel, [tile, norm_spec, win_spec]
        out_specs, out_shape = in_out_specs, in_out_shapes
    elif kind == "mid":
        body, in_specs = _mid_kernel, out_in_specs + [wo_spec, norm_spec, win_spec]
        out_specs, out_shape = [tile] + in_out_specs, [x_shape] + in_out_shapes
    else:
        body, in_specs = _last_kernel, out_in_specs + [wo_spec, norm_spec]
        out_specs, out_shape = tile, x_shape
    return pl.pallas_call(
        body,
        grid=(m // TM_PROJ,),
        in_specs=in_specs,
        out_specs=out_specs,
        out_shape=out_shape,
        compiler_params=_cparams(("parallel",)),
        name="proj_" + kind,
    )(*operands)


def _fox_kernel(q_ref, k_ref, v_ref, fl_ref, bf_ref, o_ref, kaug_ref, vaug_ref, c_ref, clast_ref,
                knorm_ref):
    t = T_FOX
    i = pl.program_id(1)
    seq = k_ref.shape[1]
    lane = lax.broadcasted_iota(jnp.int32, (t, LANES), 1)
    lane1 = lane[0:1, :]
    own = (lane < HEAD_DIM, lane >= HEAD_DIM)
    n_pairs = D_FOX // LANES
    heads = [(g, e) for g in range(n_pairs) for e in range(2)]
    cols = lambda g: slice(g * LANES, (g + 1) * LANES)
    x0 = lambda e: HEAD_DIM if e == 0 else 0
    shift = lambda g, e: (x0(e) - (2 * g + e)) % LANES

    half_id = lax.broadcasted_iota(jnp.int32, (LANES, LANES), 0) // HEAD_DIM
    same_half = (half_id == lax.broadcasted_iota(jnp.int32, (LANES, LANES), 1) // HEAD_DIM)
    head_sum = jnp.where(same_half, 1.0, 0.0).astype(BF16)

    def max_sq_norms(ref, rows):
        out = jnp.zeros((1, LANES), F32)
        for g in range(n_pairs):
            x = ref[0, rows, cols(g)]
            sums = jnp.dot(x * x, head_sum, preferred_element_type=F32)
            top = jnp.max(sums, axis=0, keepdims=True) * FOX_NORM_SLACK
            out = jnp.where(lane1 == 2 * g, top,
                            jnp.where(lane1 == 2 * g + 1, pltpu.roll(top, HEAD_DIM, 1), out))
        return out

    @pl.when(i == 0)
    def _():
        r = lax.broadcasted_iota(jnp.int32, (t, t), 0)
        s = lax.broadcasted_iota(jnp.int32, (t, t), 1)
        tri = jnp.where(s <= r, 1.0, 0.0).astype(BF16)

        def build(j, carried):
            carry, k_max = carried
            rows = pl.ds(pl.multiple_of(j * t, t), t)
            z = fl_ref[0, rows, :] + bf_ref[0]
            logf = jnp.minimum(z, 0.0) - jnp.log1p(jnp.exp(-jnp.abs(z)))
            top = logf.astype(BF16)
            rem = logf - top.astype(F32)
            middle = rem.astype(BF16)
            bottom = (rem - middle.astype(F32)).astype(BF16)
            c = carry + sum(jnp.dot(tri, piece, preferred_element_type=F32)
                            for piece in (bottom, middle, top))
            c_ref[rows, :] = c
            clast_ref[pl.ds(j, 1), :] = c[t - 1:t, :]
            k_max = jnp.maximum(k_max, max_sq_norms(k_ref, rows))
            knorm_ref[pl.ds(j, 1), :] = k_max
            b = -LOG2E * c
            hi = b.astype(BF16).astype(F32)
            mid = (b - hi).astype(BF16).astype(F32)
            lo = b - hi - mid
            pieces = jnp.where(lane < 8, hi, jnp.where(lane < 16, pltpu.roll(mid, 8, 1), jnp.where(
                lane < 24, pltpu.roll(lo, 16, 1), jnp.where(lane < 32, 1.0, 0.0))))
            for n, (g, e) in enumerate(heads):
                extra = pltpu.roll(pieces, shift(g, e), 1).astype(BF16)
                kaug_ref[n, rows, :] = jnp.where(own[e], k_ref[0, rows, cols(g)], extra)
                v = v_ref[0, rows, cols(g)]
                vaug_ref[n, rows, :] = jnp.where(own[e], v, jnp.ones_like(v))
            return c[t - 1:t, :], k_max

        zero_row = jnp.zeros((1, LANES), F32)
        lax.fori_loop(0, seq // t, build, (zero_row, zero_row))

    cq0 = LOG2E * c_ref[pl.ds(pl.multiple_of(i * t, t), SUBLANES), :][0:1, :]
    qa = []
    for g, e in heads:
        ones = (lane1 == x0(e)) | (lane1 == x0(e) + 8) | (lane1 == x0(e) + 16)
        extra = jnp.where(ones, 1.0, jnp.where(
            lane1 == x0(e) + 24, pltpu.roll(cq0, (shift(g, e) + 24) % LANES, 1), 0.0))
        qa.append(jnp.where(own[e], q_ref[0, :, cols(g)], extra.astype(BF16)))

    def tile(j, carry, masked, ids):
        m, acc = carry
        rows = pl.ds(pl.multiple_of(j * t, t), t)
        new_m, new_acc = [], []
        for pos, n in enumerate(ids):
            s = lax.dot_general(qa[n], kaug_ref[n, rows, :], (((1,), (1,)), ((), ())),
                                preferred_element_type=F32)
            if masked:
                r = lax.broadcasted_iota(jnp.int32, (t, t), 0)
                cidx = lax.broadcasted_iota(jnp.int32, (t, t), 1)
                s = jnp.where(cidx <= r, s, NEG_BIG)
            m_new = jnp.maximum(m[pos], jnp.max(s, axis=-1, keepdims=True))
            alpha = jnp.exp2(m[pos] - m_new)
            p = jnp.exp2(s - m_new).astype(BF16)
            new_m.append(m_new)
            new_acc.append(acc[pos] * alpha + jnp.dot(p, vaug_ref[n, rows, :],
                                                      preferred_element_type=F32))
        return tuple(new_m), tuple(new_acc)

    n_tiles = seq // t
    tile_id = lax.broadcasted_iota(jnp.int32, (n_tiles, LANES), 0)
    cq0_used = cq0.astype(BF16).astype(F32)
    upper = (jnp.sqrt(max_sq_norms(q_ref, slice(None)) * knorm_ref[...])
             + (cq0_used - LOG2E * clast_ref[...]))

    def needed(j, m, ids):
        row_min = jnp.zeros((1, LANES), F32)
        for pos, n in enumerate(ids):
            row_min = jnp.where(lane1 == n, jnp.min(m[pos], axis=0, keepdims=True), row_min)
        upper_j = jnp.sum(jnp.where(tile_id == j, upper, 0.0), axis=0, keepdims=True)
        hit = (upper_j >= row_min - FOX_SKIP_LOG2) & (lane1 >= ids[0]) & (lane1 <= ids[-1])
        return (j >= 0) & (jnp.max(jnp.where(hit, 1.0, 0.0)) > 0.0)

    acc = []
    for first in range(0, len(heads), FOX_GROUP):
        ids = list(range(first, first + FOX_GROUP))
        init = ((jnp.full((t, 1), NEG_BIG, F32),) * FOX_GROUP,
                (jnp.zeros((t, LANES), F32),) * FOX_GROUP)
        m, group_acc = tile(i, init, True, ids)

        def walk(state, ids=ids):
            j, _, m, group_acc = state
            m, group_acc = tile(j, (m, group_acc), False, ids)
            return j - 1, needed(j - 1, m, ids), m, group_acc

        state = lax.while_loop(lambda state: state[1], walk,
                               (i - 1, needed(i - 1, m, ids), m, group_acc))
        acc.extend(state[3])
    outs = [a / pltpu.roll(a, HEAD_DIM, 1) for a in acc]
    for g in range(n_pairs):
        o_ref[0, :, cols(g)] = jnp.where(own[0], outs[2 * g], outs[2 * g + 1]).astype(o_ref.dtype)


def _fox_attention(qkv, flog, bf_pad, layer):
    batch, seq, _ = qkv.shape
    n_heads = D_FOX // HEAD_DIM
    once = pl.Buffered(1)
    return pl.pallas_call(
        _fox_kernel,
        grid=(batch, seq // T_FOX),
        in_specs=[
            pl.BlockSpec((1, T_FOX, D_FOX), lambda b, i: (b, i, 0)),
            pl.BlockSpec((1, seq, D_FOX), lambda b, i: (b, 0, 1), pipeline_mode=once),
            pl.BlockSpec((1, seq, D_FOX), lambda b, i: (b, 0, 2), pipeline_mode=once),
            pl.BlockSpec((1, seq, LANES), lambda b, i: (b, 0, 0), pipeline_mode=once),
            pl.BlockSpec((1, 1, LANES), lambda b, i: (layer, 0, 0)),
        ],
        out_specs=pl.BlockSpec((1, T_FOX, D_FOX), lambda b, i: (b, i, 0)),
        out_shape=jax.ShapeDtypeStruct((batch, seq, D_FOX), BF16),
        scratch_shapes=[pltpu.VMEM((n_heads, seq, LANES), BF16),
                        pltpu.VMEM((n_heads, seq, LANES), BF16),
                        pltpu.VMEM((seq, LANES), F32),
                        pltpu.VMEM((seq // T_FOX, LANES), F32),
                        pltpu.VMEM((seq // T_FOX, LANES), F32)],
        compiler_params=_cparams(("parallel", "arbitrary")),
        name="fox_attention",
    )(qkv, qkv, qkv, flog, bf_pad)


def _ret_kernel(q_ref, k_ref, v_ref, cos_ref, sin_ref, lg_ref, gn_ref, o_ref,
                state_ref, dmat_ref, wq_ref, wk_ref):
    t = T_RET
    n_pairs = D_RET // LANES
    scale = 1.0 / math.sqrt(HEAD_DIM)
    lane = lax.broadcasted_iota(jnp.int32, (t, LANES), 1)
    first = lane < HEAD_DIM
    low_half = (lane % HEAD_DIM) < (HEAD_DIM // 2)
    cols = lambda p: slice(p * LANES, (p + 1) * LANES)

    @pl.when((pl.program_id(0) == 0) & (pl.program_id(1) == 0))
    def _():
        pos = lax.broadcasted_iota(jnp.int32, (t, 1), 0).astype(F32)
        r = lax.broadcasted_iota(jnp.int32, (t, t), 0)
        s = lax.broadcasted_iota(jnp.int32, (t, t), 1)
        dist = jnp.abs(r - s).astype(F32)
        visible = (s // CHUNK) <= (r // CHUNK)
        for h in range(RET_HEADS):
            lg = lg_ref[:, h * HEAD_DIM:h * HEAD_DIM + 1]
            dmat_ref[h] = jnp.where(visible, jnp.exp(lg * dist) * scale, 0.0)
        wq_ref[...] = jnp.exp(lg_ref[...] * (pos + 1.0)) * scale
        wk_ref[...] = jnp.exp(lg_ref[...] * (t - 1.0 - pos))

    @pl.when(pl.program_id(1) == 0)
    def _():
        state_ref[...] = jnp.zeros_like(state_ref)

    rb = lax.broadcasted_iota(jnp.int32, (LANES, LANES), 0)
    cb = lax.broadcasted_iota(jnp.int32, (LANES, LANES), 1)
    same_head = (rb // HEAD_DIM) == (cb // HEAD_DIM)

    def head_mean(x):
        s0 = jnp.sum(jnp.where(first, x, 0.0), axis=-1, keepdims=True)
        s1 = jnp.sum(jnp.where(first, 0.0, x), axis=-1, keepdims=True)
        return jnp.where(first, s0, s1) * (1.0 / HEAD_DIM)

    for sub in range(RET_BLK // t):
        rows = slice(sub * t, (sub + 1) * t)
        cos = cos_ref[rows, :]
        sin = sin_ref[rows, :]

        def rotary(x):
            swapped = jnp.where(low_half, pltpu.roll(x, LANES - HEAD_DIM // 2, 1),
                                pltpu.roll(x, HEAD_DIM // 2, 1))
            return x * cos + swapped * sin

        for p in range(n_pairs):
            q = rotary(q_ref[0, rows, cols(p)].astype(F32))
            k = rotary(k_ref[0, rows, cols(p)].astype(F32))
            vb = v_ref[0, rows, cols(p)]
            kb = k.astype(BF16)
            zero = jnp.zeros_like(q)
            inner = []
            for e in range(2):
                qe = jnp.where(first, q, zero) if e == 0 else jnp.where(first, zero, q)
                sc = lax.dot_general(qe.astype(BF16), kb, (((1,), (1,)), ((), ())),
                                     preferred_element_type=F32) * dmat_ref[2 * p + e]
                inner.append(jnp.dot(sc.astype(BF16), vb, preferred_element_type=F32))
            state = state_ref[p]
            o = jnp.where(first, inner[0], inner[1]) + jnp.dot(
                (q * wq_ref[:, cols(p)]).astype(BF16), state.astype(BF16),
                preferred_element_type=F32)
            upd = lax.dot_general((k * wk_ref[:, cols(p)]).astype(BF16), vb,
                                  (((0,), (0,)), ((), ())), preferred_element_type=F32)
            decay = jnp.exp(lg_ref[:, cols(p)] * float(t))
            state_ref[p] = jnp.where(same_head, state * decay + upd, 0.0)
            d = o - head_mean(o)
            var = head_mean(d * d)
            o_ref[0, rows, cols(p)] = (d * lax.rsqrt(var + EPS) * gn_ref[0, :, cols(p)]).astype(
                o_ref.dtype)


def _retention(rest, cos_t, sin_t, lg, gn_w, layer):
    batch, seq, _ = rest.shape
    n_pairs = D_RET // LANES
    tok = lambda blk: pl.BlockSpec((1, RET_BLK, D_RET), lambda b, i, blk=blk: (b, i, blk))
    fixed = pl.BlockSpec((1, D_RET), lambda b, i: (0, 0))
    table = pl.BlockSpec((RET_BLK, LANES), lambda b, i: (i, 0))
    return pl.pallas_call(
        _ret_kernel,
        grid=(batch, seq // RET_BLK),
        in_specs=[tok(4), tok(5), tok(6), table, table, fixed,
                  pl.BlockSpec((1, 1, D_RET), lambda b, i: (layer, 0, 0))],
        out_specs=pl.BlockSpec((1, RET_BLK, D_RET), lambda b, i: (b, i, 0)),
        out_shape=jax.ShapeDtypeStruct((batch, seq, D_RET), BF16),
        scratch_shapes=[
            pltpu.VMEM((n_pairs, LANES, LANES), F32),
            pltpu.VMEM((RET_HEADS, T_RET, T_RET), F32),
            pltpu.VMEM((T_RET, D_RET), F32),
            pltpu.VMEM((T_RET, D_RET), F32),
        ],
        compiler_params=_cparams(("arbitrary", "arbitrary")),
        name="retention",
    )(rest, rest, rest, cos_t, sin_t, lg, gn_w)


def _s5_kernel(u_ref, are_ref, aim_ref, ldt_ref, br_ref, bi_ref, cr_ref, ci_ref, d_ref, wg_ref,
               o_ref, wb_ref, wc_ref, lam_ref, lhs_ref, x_ref, z_ref, state_ref):
    nb = u_ref.shape[0]
    t5 = S5_T
    win = D_S5 // SUBLANES

    @pl.when(pl.program_id(0) == 0)
    def _():
        dt = jnp.exp(ldt_ref[0])
        ar = are_ref[0]
        ai = aim_ref[0]
        mag = jnp.exp(ar * dt)
        lr = mag * jnp.cos(ai * dt)
        li = mag * jnp.sin(ai * dt)
        den = ar * ar + ai * ai
        fr = ((lr - 1.0) * ar + li * ai) / den
        fi = (li * ar - (lr - 1.0) * ai) / den
        lam_ref[0] = lr
        lam_ref[1] = li
        for j in range(SUBLANES):
            rows = slice(win * j, win * (j + 1))
            f_r = fr[j:j + 1, :]
            f_i = fi[j:j + 1, :]
            b_r = br_ref[0, rows, :]
            b_i = bi_ref[0, rows, :]
            wb_ref[rows, :LANES] = (f_r * b_r - f_i * b_i).astype(BF16)
            wb_ref[rows, LANES:] = (f_r * b_i + f_i * b_r).astype(BF16)
        wc_ref[:LANES, :] = cr_ref[0].astype(BF16)
        wc_ref[LANES:, :] = (-ci_ref[0]).astype(BF16)
        state_ref[...] = jnp.zeros_like(state_ref)

    sub = lax.broadcasted_iota(jnp.int32, (2 * SUBLANES, D_S5), 0)
    keep = sub % SUBLANES == lax.broadcasted_iota(jnp.int32, (2 * SUBLANES, D_S5), 1) // win
    first = sub < SUBLANES

    for tt in range(t5 // SUBLANES):
        for b in range(nb):
            tile = u_ref[b, tt * SUBLANES:(tt + 1) * SUBLANES, :]
            for s in range(0, SUBLANES, 2):
                two = jnp.where(first,
                                jnp.broadcast_to(tile[s:s + 1, :], (2 * SUBLANES, D_S5)),
                                jnp.broadcast_to(tile[s + 1:s + 2, :], (2 * SUBLANES, D_S5)))
                row0 = (tt * SUBLANES + s) * SUBLANES
                lhs_ref[b, row0:row0 + 2 * SUBLANES, :] = jnp.where(keep, two, 0.0).astype(BF16)
    for b in range(nb):
        x_ref[b] = jnp.dot(lhs_ref[b], wb_ref[...], preferred_element_type=F32)

    lr = lam_ref[0]
    li = lam_ref[1]

    def scan(t, carry):
        rows = slice(t * SUBLANES, (t + 1) * SUBLANES)
        new = []
        for b in range(nb):
            xr, xi = carry[b]
            nr = lr * xr - li * xi + x_ref[b, rows, :LANES]
            ni = lr * xi + li * xr + x_ref[b, rows, LANES:]
            x_ref[b, rows, :LANES] = nr
            x_ref[b, rows, LANES:] = ni
            new.append((nr, ni))
        return tuple(new)

    final = tuple((state_ref[b, :, :LANES], state_ref[b, :, LANES:]) for b in range(nb))
    for t in range(t5):
        final = scan(t, final)
    for b in range(nb):
        state_ref[b, :, :LANES] = final[b][0]
        state_ref[b, :, LANES:] = final[b][1]

    window = lax.broadcasted_iota(jnp.int32, (t5, LANES), 1) // win
    for b in range(nb):
        z = jnp.dot(x_ref[b].astype(BF16), wc_ref[...], preferred_element_type=F32)
        z_ref[b, 0] = z[:, :LANES]
        z_ref[b, 1] = z[:, LANES:]
    for b in range(nb):
        halves = []
        for hh in range(2):
            per = SUBLANES // 2
            y = z_ref[b, hh, pl.ds(per * hh + per - 1, t5, stride=SUBLANES), :]
            for jj in range(per - 2, -1, -1):
                y = jnp.where(window == jj,
                              z_ref[b, hh, pl.ds(per * hh + jj, t5, stride=SUBLANES), :], y)
            halves.append(y)
        u = u_ref[b]
        y = jax.nn.gelu(jnp.concatenate(halves, axis=1) + d_ref[0] * u)
        gate = jnp.dot(y.astype(BF16), wg_ref[0], preferred_element_type=F32)
        o_ref[b] = (y * jax.nn.sigmoid(gate)).astype(o_ref.dtype)


def _s5(su, prm, layer):
    batch, seq, _ = su.shape
    fixed = lambda i: (layer, 0, 0)
    lam = pl.BlockSpec((1, SUBLANES, LANES), fixed)
    rows8 = SUBLANES * S5_T
    return pl.pallas_call(
        _s5_kernel,
        grid=(seq // S5_T,),
        in_specs=[
            pl.BlockSpec((batch, S5_T, D_S5), lambda i: (0, i, 0)),
            lam, lam, lam,
            pl.BlockSpec((1, D_S5, LANES), fixed), pl.BlockSpec((1, D_S5, LANES), fixed),
            pl.BlockSpec((1, LANES, D_S5), fixed), pl.BlockSpec((1, LANES, D_S5), fixed),
            pl.BlockSpec((1, 1, D_S5), fixed),
            pl.BlockSpec((1, D_S5, D_S5), fixed),
        ],
        out_specs=pl.BlockSpec((batch, S5_T, D_S5), lambda i: (0, i, 0)),
        out_shape=jax.ShapeDtypeStruct((batch, seq, D_S5), BF16),
        scratch_shapes=[
            pltpu.VMEM((D_S5, 2 * LANES), BF16),
            pltpu.VMEM((2 * LANES, D_S5), BF16),
            pltpu.VMEM((2, SUBLANES, LANES), F32),
            pltpu.VMEM((batch, rows8, D_S5), BF16),
            pltpu.VMEM((batch, rows8, 2 * LANES), F32),
            pltpu.VMEM((batch, 2, rows8, LANES), F32),
            pltpu.VMEM((batch, SUBLANES, 2 * LANES), F32),
        ],
        compiler_params=_cparams(("arbitrary",)),
        name="s5",
    )(su, prm["a_re"], prm["a_im"], prm["log_dt"], prm["b_re"], prm["b_im"],
      prm["c_re"], prm["c_im"], prm["d"], prm["w_glu"])


def _rotary_tables(seq):
    half = HEAD_DIM // 2
    freqs = ROPE_BASE ** (-jnp.arange(half, dtype=F32) / half)
    ang = jnp.arange(seq, dtype=F32)[:, None] * freqs[None, :]
    cos = jnp.cos(ang)
    sin = jnp.sin(ang)
    cos_t = jnp.tile(cos, (1, LANES // half))
    sin_t = jnp.tile(jnp.concatenate([-sin, sin], axis=-1), (1, LANES // HEAD_DIM))
    return cos_t, sin_t


def _s5_params(a_re, a_im, b_re, b_im, c_re, c_im, d, log_dt, w_glu):
    depth = a_re.shape[0]
    pair = jnp.eye(2, dtype=F32)
    half = S5_GROUPS // 2
    pack_b = lambda b: jnp.einsum(
        "ljnph,kn->ljkhnp", b.reshape(depth, half, 2, S5_STATE, S5_GROUP_CH), pair
    ).reshape(depth, D_S5, LANES)
    pack_c = lambda c: jnp.einsum(
        "ljnhp,kn->lnpjkh", c.reshape(depth, half, 2, S5_GROUP_CH, S5_STATE), pair
    ).reshape(depth, LANES, D_S5)
    tile = lambda v: v.reshape(depth, SUBLANES, LANES)
    return {
        "a_re": tile(a_re), "a_im": tile(a_im),
        "log_dt": tile(jnp.repeat(log_dt, S5_STATE, axis=-1)),
        "b_re": pack_b(b_re), "b_im": pack_b(b_im), "c_re": pack_c(c_re), "c_im": pack_c(c_im),
        "d": d.reshape(depth, 1, D_S5), "w_glu": w_glu.astype(BF16),
    }


def kernel(x, norm_w, w_in, fox_b_f, s5_a_re, s5_a_im, s5_b_re, s5_b_im, s5_c_re, s5_c_im,
           s5_d, s5_log_dt, s5_w_glu, ret_gn_w, w_out, final_norm_w):
    batch, seq, _ = x.shape
    depth = w_in.shape[0]
    m = batch * seq
    cos_t, sin_t = _rotary_tables(seq)
    log_gamma = jnp.log1p(-(2.0 ** (-5.0 - jnp.arange(RET_HEADS, dtype=F32))))
    lg = jnp.repeat(log_gamma, HEAD_DIM).reshape(1, D_RET)
    norm_g = jnp.concatenate([norm_w, final_norm_w[None]], axis=0).reshape(depth + 1, 1, D_MODEL)

    order = jnp.argsort(fox_b_f, axis=1).astype(jnp.int32)
    bf_sorted = jnp.take_along_axis(fox_b_f, order, axis=1)
    bf_pad = jnp.pad(bf_sorted, ((0, 0), (0, LANES - FOX_HEADS))).reshape(depth, 1, LANES)
    gn_w = ret_gn_w.reshape(depth, 1, D_RET)
    s5_prm = _s5_params(s5_a_re, s5_a_im, s5_b_re, s5_b_im, s5_c_re, s5_c_im, s5_d, s5_log_dt,
                        s5_w_glu)

    w_in_t = jnp.swapaxes(w_in, 1, 2)
    packed = [_pack_weights(order[l], w_in_t, w_out, l) for l in range(depth)]
    x2 = x.reshape(m, D_MODEL)
    qkv, su, rest, flog = _projections("first", m, norm_g, 0, (x2, norm_g, packed[0][0]))
    for l in range(depth):
        rest3 = rest.reshape(batch, seq, N_REST - D_S5)
        y_fox = _fox_attention(qkv.reshape(batch, seq, N_QKV), flog.reshape(batch, seq, LANES),
                               bf_pad, l)
        y_s5 = _s5(su.reshape(batch, seq, D_S5), s5_prm, l)
        y_ret = _retention(rest3, cos_t, sin_t, lg, gn_w, l)
        ys = (y_fox.reshape(m, D_FOX), y_s5.reshape(m, D_S5), y_ret.reshape(m, D_RET), rest, x2)
        if l < depth - 1:
            x2, qkv, su, rest, flog = _projections(
                "mid", m, norm_g, l + 1, ys + (packed[l][1], norm_g, packed[l + 1][0]))
    out = _projections("last", m, norm_g, depth, ys + (packed[depth - 1][1], norm_g))
    return out.reshape(batch, seq, D_MODEL)
```

```python
import functools
import math

import jax
import jax.numpy as jnp
from jax import lax
from jax.experimental import pallas as pl
from jax.experimental.pallas import tpu as pltpu

F32 = jnp.float32
BF16 = jnp.bfloat16

D_MODEL = 1024
HEAD_DIM = 64
CHUNK = 64
D_FOX = 512
FOX_HEADS = 8
D_S5 = 256
S5_GROUPS = 16
S5_GROUP_CH = 16
S5_STATE = 64
D_RET = 256
RET_HEADS = 4
ROPE_BASE = 10000.0
EPS = 1e-6

LANES = 128
SUBLANES = 8
N_QKV = 3 * D_FOX
N_REST = D_S5 + D_MODEL + 3 * D_RET
N_ALL = N_QKV + N_REST + 128
D_IN_PROJ = N_QKV + FOX_HEADS + N_REST
VMEM_LIMIT = 56 * 1024 * 1024

TM_PROJ = 512
T_FOX = 512
FOX_GROUP = 2
T_RET = 256
RET_BLK = 512
S5_T = 256
NEG_BIG = -1e30
FOX_SKIP_LOG2 = 40.0
FOX_NORM_SLACK = 1.02
LOG2E = math.log2(math.e)
Q_SCALE = LOG2E / math.sqrt(HEAD_DIM)


def _cparams(sem, flags=None):
    return pltpu.CompilerParams(dimension_semantics=sem, vmem_limit_bytes=VMEM_LIMIT, flags=flags)


def _rms_norm(x, g):
    ms = jnp.mean(x * x, axis=-1, keepdims=True)
    return x * lax.rsqrt(ms + EPS) * g


def _pack_kernel(order_ref, wt_ref, wout_ref, win_ref, wo_ref):
    su_lo = N_QKV + FOX_HEADS
    rqkv_lo = su_lo + D_S5
    gate_lo = rqkv_lo + 3 * D_RET

    def copy_rows(dst, src, size):
        win_ref[dst:dst + size, :] = wt_ref[0, src, :].astype(BF16)

    def copy_heads(dst, src):
        for n in range(FOX_HEADS):
            start = pl.multiple_of(src + order_ref[n] * HEAD_DIM, HEAD_DIM)
            copy_rows(dst + n * HEAD_DIM, pl.ds(start, HEAD_DIM), HEAD_DIM)

    def copy_plain(dst, lo, hi):
        for start in range(lo, hi, 256):
            size = min(256, hi - start)
            copy_rows(dst + start - lo, slice(start, start + size), size)

    for part in range(3):
        copy_heads(part * D_FOX, part * D_FOX)
    copy_plain(N_QKV, su_lo, rqkv_lo)
    copy_heads(N_QKV + D_S5, gate_lo)
    copy_plain(N_QKV + D_S5 + D_FOX, gate_lo + D_FOX, D_IN_PROJ)
    copy_plain(N_QKV + D_S5 + D_MODEL, rqkv_lo, gate_lo)
    flog = [wt_ref[0, pl.ds(N_QKV + order_ref[n], 1), :] for n in range(FOX_HEADS)]
    flog.append(jnp.zeros((LANES - FOX_HEADS, D_MODEL), F32))
    win_ref[N_QKV + N_REST:, :] = jnp.concatenate(flog, axis=0).astype(BF16)

    for n in range(FOX_HEADS):
        src = pl.ds(pl.multiple_of(order_ref[n] * HEAD_DIM, HEAD_DIM), HEAD_DIM)
        wo_ref[n * HEAD_DIM:(n + 1) * HEAD_DIM, :] = wout_ref[0, src, :].astype(BF16)
    wo_ref[D_FOX:, :] = wout_ref[0, D_FOX:, :].astype(BF16)


def _pack_weights(order, w_in_t, w_out, layer):
    grid_spec = pltpu.PrefetchScalarGridSpec(
        num_scalar_prefetch=1,
        grid=(1,),
        in_specs=[
            pl.BlockSpec((1, D_IN_PROJ, D_MODEL), lambda i, order: (layer, 0, 0)),
            pl.BlockSpec((1, D_MODEL, D_MODEL), lambda i, order: (layer, 0, 0)),
        ],
        out_specs=[
            pl.BlockSpec((N_ALL, D_MODEL), lambda i, order: (0, 0)),
            pl.BlockSpec((D_MODEL, D_MODEL), lambda i, order: (0, 0)),
        ],
    )
    return pl.pallas_call(
        _pack_kernel,
        grid_spec=grid_spec,
        out_shape=[jax.ShapeDtypeStruct((N_ALL, D_MODEL), BF16),
                   jax.ShapeDtypeStruct((D_MODEL, D_MODEL), BF16)],
        compiler_params=_cparams(("arbitrary",)),
        name="pack_weights",
    )(order, w_in_t, w_out)


def _project_in(h, w_ref, oa_ref, su_ref, ob_ref, fl_ref):
    nt = (((1,), (1,)), ((), ()))
    qkv = lax.dot_general(h, w_ref[:N_QKV, :], nt, preferred_element_type=F32)
    oa_ref[:, :D_FOX] = (qkv[:, :D_FOX] * Q_SCALE).astype(oa_ref.dtype)
    oa_ref[:, D_FOX:] = qkv[:, D_FOX:].astype(oa_ref.dtype)
    rest = lax.dot_general(h, w_ref[N_QKV:N_QKV + N_REST, :], nt, preferred_element_type=F32)
    su_ref[...] = rest[:, :D_S5]
    ob_ref[...] = rest[:, D_S5:].astype(ob_ref.dtype)
    fl_ref[...] = lax.dot_general(h, w_ref[N_QKV + N_REST:, :], nt, preferred_element_type=F32)


def _project_out(yf_ref, ys_ref, yr_ref, gate_ref, x_ref, w_ref):
    g = gate_ref[...].astype(F32)
    g = g * jax.nn.sigmoid(g)
    acc = x_ref[...]
    off = 0
    for y_ref in (yf_ref, ys_ref, yr_ref):
        width = y_ref.shape[-1]
        y = (y_ref[...] * g[:, off:off + width]).astype(BF16)
        acc = acc + jnp.dot(y, w_ref[off:off + width, :], preferred_element_type=F32)
        off += width
    return acc


def _first_kernel(x_ref, g_ref, win_ref, oa_ref, su_ref, ob_ref, fl_ref):
    h = _rms_norm(x_ref[...], g_ref[0]).astype(BF16)
    _project_in(h, win_ref, oa_ref, su_ref, ob_ref, fl_ref)


def _mid_kernel(yf_ref, ys_ref, yr_ref, gate_ref, x_ref, wo_ref, g_ref, win_ref,
                xo_ref, oa_ref, su_ref, ob_ref, fl_ref):
    acc = _project_out(yf_ref, ys_ref, yr_ref, gate_ref, x_ref, wo_ref)
    xo_ref[...] = acc
    h = _rms_norm(acc, g_ref[0]).astype(BF16)
    _project_in(h, win_ref, oa_ref, su_ref, ob_ref, fl_ref)


def _last_kernel(yf_ref, ys_ref, yr_ref, gate_ref, x_ref, wo_ref, g_ref, o_ref):
    acc = _project_out(yf_ref, ys_ref, yr_ref, gate_ref, x_ref, wo_ref)
    o_ref[...] = _rms_norm(acc, g_ref[0])


def _projections(kind, m, norm_g, norm_layer, operands):
    row = lambda i: (i, 0)
    fixed = lambda i: (0, 0)
    tile = pl.BlockSpec((TM_PROJ, D_MODEL), row)
    once = pl.Buffered(1)
    norm_spec = pl.BlockSpec((1, 1, D_MODEL), lambda i: (norm_layer, 0, 0))
    win_spec = pl.BlockSpec((N_ALL, D_MODEL), fixed, pipeline_mode=once)
    wo_spec = pl.BlockSpec((D_MODEL, D_MODEL), fixed, pipeline_mode=once)
    out_in_specs = [pl.BlockSpec((TM_PROJ, D_FOX), row), pl.BlockSpec((TM_PROJ, D_S5), row),
                    pl.BlockSpec((TM_PROJ, D_RET), row), tile, tile]
    in_out_specs = [pl.BlockSpec((TM_PROJ, N_QKV), row), pl.BlockSpec((TM_PROJ, D_S5), row),
                    pl.BlockSpec((TM_PROJ, N_REST - D_S5), row),
                    pl.BlockSpec((TM_PROJ, LANES), row)]
    in_out_shapes = [jax.ShapeDtypeStruct((m, N_QKV), BF16), jax.ShapeDtypeStruct((m, D_S5), F32),
                     jax.ShapeDtypeStruct((m, N_REST - D_S5), BF16),
                     jax.ShapeDtypeStruct((m, LANES), F32)]
    x_shape = jax.ShapeDtypeStruct((m, D_MODEL), F32)
    if kind == "first":
        body, in_specs = _first_kernel, [tile, norm_spec, win_spec]
        out_specs, out_shape = in_out_specs, in_out_shapes
    elif kind == "mid":
        body, in_specs = _mid_kernel, out_in_specs + [wo_spec, norm_spec, win_spec]
        out_specs, out_shape = [tile] + in_out_specs, [x_shape] + in_out_shapes
    else:
        body, in_specs = _last_kernel, out_in_specs + [wo_spec, norm_spec]
        out_specs, out_shape = tile, x_shape
    return pl.pallas_call(
        body,
        grid=(m // TM_PROJ,),
        in_specs=in_specs,
        out_specs=out_specs,
        out_shape=out_shape,
        compiler_params=_cparams(("parallel",)),
        name="proj_" + kind,
    )(*operands)


def _fox_kernel(q_ref, k_ref, v_ref, fl_ref, bf_ref, o_ref, kaug_ref, vaug_ref, c_ref, clast_ref,
                knorm_ref):
    t = T_FOX
    i = pl.program_id(1)
    seq = k_ref.shape[1]
    lane = lax.broadcasted_iota(jnp.int32, (t, LANES), 1)
    lane1 = lane[0:1, :]
    own = (lane < HEAD_DIM, lane >= HEAD_DIM)
    n_pairs = D_FOX // LANES
    heads = [(g, e) for g in range(n_pairs) for e in range(2)]
    cols = lambda g: slice(g * LANES, (g + 1) * LANES)
    x0 = lambda e: HEAD_DIM if e == 0 else 0
    shift = lambda g, e: (x0(e) - (2 * g + e)) % LANES

    half_id = lax.broadcasted_iota(jnp.int32, (LANES, LANES), 0) // HEAD_DIM
    same_half = (half_id == lax.broadcasted_iota(jnp.int32, (LANES, LANES), 1) // HEAD_DIM)
    head_sum = jnp.where(same_half, 1.0, 0.0).astype(BF16)

    def max_sq_norms(ref, rows):
        out = jnp.zeros((1, LANES), F32)
        for g in range(n_pairs):
            x = ref[0, rows, cols(g)]
            sums = jnp.dot(x * x, head_sum, preferred_element_type=F32)
            top = jnp.max(sums, axis=0, keepdims=True) * FOX_NORM_SLACK
            out = jnp.where(lane1 == 2 * g, top,
                            jnp.where(lane1 == 2 * g + 1, pltpu.roll(top, HEAD_DIM, 1), out))
        return out

    @pl.when(i == 0)
    def _():
        r = lax.broadcasted_iota(jnp.int32, (t, t), 0)
        s = lax.broadcasted_iota(jnp.int32, (t, t), 1)
        tri = jnp.where(s <= r, 1.0, 0.0).astype(BF16)

        def build(j, carried):
            carry, k_max = carried
            rows = pl.ds(pl.multiple_of(j * t, t), t)
            z = fl_ref[0, rows, :] + bf_ref[0]
            logf = jnp.minimum(z, 0.0) - jnp.log1p(jnp.exp(-jnp.abs(z)))
            top = logf.astype(BF16)
            rem = logf - top.astype(F32)
            middle = rem.astype(BF16)
            bottom = (rem - middle.astype(F32)).astype(BF16)
            c = carry + sum(jnp.dot(tri, piece, preferred_element_type=F32)
                            for piece in (bottom, middle, top))
            c_ref[rows, :] = c
            clast_ref[pl.ds(j, 1), :] = c[t - 1:t, :]
            k_max = jnp.maximum(k_max, max_sq_norms(k_ref, rows))
            knorm_ref[pl.ds(j, 1), :] = k_max
            b = -LOG2E * c
            hi = b.astype(BF16).astype(F32)
            mid = (b - hi).astype(BF16).astype(F32)
            lo = b - hi - mid
            pieces = jnp.where(lane < 8, hi, jnp.where(lane < 16, pltpu.roll(mid, 8, 1), jnp.where(
                lane < 24, pltpu.roll(lo, 16, 1), jnp.where(lane < 32, 1.0, 0.0))))
            for n, (g, e) in enumerate(heads):
                extra = pltpu.roll(pieces, shift(g, e), 1).astype(BF16)
                kaug_ref[n, rows, :] = jnp.where(own[e], k_ref[0, rows, cols(g)], extra)
                v = v_ref[0, rows, cols(g)]
                vaug_ref[n, rows, :] = jnp.where(own[e], v, jnp.ones_like(v))
            return c[t - 1:t, :], k_max

        zero_row = jnp.zeros((1, LANES), F32)
        carried = (zero_row, zero_row)
        for j in range(seq // t):
            carried = build(j, carried)

    cq0 = LOG2E * c_ref[pl.ds(pl.multiple_of(i * t, t), SUBLANES), :][0:1, :]
    qa = []
    for g, e in heads:
        ones = (lane1 == x0(e)) | (lane1 == x0(e) + 8) | (lane1 == x0(e) + 16)
        extra = jnp.where(ones, 1.0, jnp.where(
            lane1 == x0(e) + 24, pltpu.roll(cq0, (shift(g, e) + 24) % LANES, 1), 0.0))
        qa.append(jnp.where(own[e], q_ref[0, :, cols(g)], extra.astype(BF16)))

    def tile(j, carry, masked, ids):
        m, acc = carry
        rows = pl.ds(pl.multiple_of(j * t, t), t)
        new_m, new_acc = [], []
        for pos, n in enumerate(ids):
            s = lax.dot_general(qa[n], kaug_ref[n, rows, :], (((1,), (1,)), ((), ())),
                                preferred_element_type=F32)
            if masked:
                r = lax.broadcasted_iota(jnp.int32, (t, t), 0)
                cidx = lax.broadcasted_iota(jnp.int32, (t, t), 1)
                s = jnp.where(cidx <= r, s, NEG_BIG)
            m_new = jnp.maximum(m[pos], jnp.max(s, axis=-1, keepdims=True))
            alpha = jnp.exp2(m[pos] - m_new)
            p = jnp.exp2(s - m_new).astype(BF16)
            new_m.append(m_new)
            new_acc.append(acc[pos] * alpha + jnp.dot(p, vaug_ref[n, rows, :],
                                                      preferred_element_type=F32))
        return tuple(new_m), tuple(new_acc)

    n_tiles = seq // t
    tile_id = lax.broadcasted_iota(jnp.int32, (n_tiles, LANES), 0)
    cq0_used = cq0.astype(BF16).astype(F32)
    upper = (jnp.sqrt(max_sq_norms(q_ref, slice(None)) * knorm_ref[...])
             + (cq0_used - LOG2E * clast_ref[...]))

    def needed(j, m, ids):
        row_min = jnp.zeros((1, LANES), F32)
        for pos, n in enumerate(ids):
            row_min = jnp.where(lane1 == n, jnp.min(m[pos], axis=0, keepdims=True), row_min)
        upper_j = jnp.sum(jnp.where(tile_id == j, upper, 0.0), axis=0, keepdims=True)
        hit = (upper_j >= row_min - FOX_SKIP_LOG2) & (lane1 >= ids[0]) & (lane1 <= ids[-1])
        return (j >= 0) & (jnp.max(jnp.where(hit, 1.0, 0.0)) > 0.0)

    acc = []
    for first in range(0, len(heads), FOX_GROUP):
        ids = list(range(first, first + FOX_GROUP))
        init = ((jnp.full((t, 1), NEG_BIG, F32),) * FOX_GROUP,
                (jnp.zeros((t, LANES), F32),) * FOX_GROUP)
        m, group_acc = tile(i, init, True, ids)

        def walk(state, ids=ids):
            j, _, m, group_acc = state
            m, group_acc = tile(j, (m, group_acc), False, ids)
            return j - 1, needed(j - 1, m, ids), m, group_acc

        state = lax.while_loop(lambda state: state[1], walk,
                               (i - 1, needed(i - 1, m, ids), m, group_acc))
        acc.extend(state[3])
    outs = [a / pltpu.roll(a, HEAD_DIM, 1) for a in acc]
    for g in range(n_pairs):
        o_ref[0, :, cols(g)] = jnp.where(own[0], outs[2 * g], outs[2 * g + 1]).astype(o_ref.dtype)


def _fox_attention(qkv, flog, bf_pad, layer):
    batch, seq, _ = qkv.shape
    n_heads = D_FOX // HEAD_DIM
    once = pl.Buffered(1)
    return pl.pallas_call(
        _fox_kernel,
        grid=(batch, seq // T_FOX),
        in_specs=[
            pl.BlockSpec((1, T_FOX, D_FOX), lambda b, i: (b, i, 0)),
            pl.BlockSpec((1, seq, D_FOX), lambda b, i: (b, 0, 1), pipeline_mode=once),
            pl.BlockSpec((1, seq, D_FOX), lambda b, i: (b, 0, 2), pipeline_mode=once),
            pl.BlockSpec((1, seq, LANES), lambda b, i: (b, 0, 0), pipeline_mode=once),
            pl.BlockSpec((1, 1, LANES), lambda b, i: (layer, 0, 0)),
        ],
        out_specs=pl.BlockSpec((1, T_FOX, D_FOX), lambda b, i: (b, i, 0)),
        out_shape=jax.ShapeDtypeStruct((batch, seq, D_FOX), BF16),
        scratch_shapes=[pltpu.VMEM((n_heads, seq, LANES), BF16),
                        pltpu.VMEM((n_heads, seq, LANES), BF16),
                        pltpu.VMEM((seq, LANES), F32),
                        pltpu.VMEM((seq // T_FOX, LANES), F32),
                        pltpu.VMEM((seq // T_FOX, LANES), F32)],
        compiler_params=_cparams(("parallel", "arbitrary")),
        name="fox_attention",
    )(qkv, qkv, qkv, flog, bf_pad)


def _ret_kernel(q_ref, k_ref, v_ref, cos_ref, sin_ref, lg_ref, gn_ref, o_ref,
                state_ref, dmat_ref, wq_ref, wk_ref):
    t = T_RET
    n_pairs = D_RET // LANES
    scale = 1.0 / math.sqrt(HEAD_DIM)
    lane = lax.broadcasted_iota(jnp.int32, (t, LANES), 1)
    first = lane < HEAD_DIM
    low_half = (lane % HEAD_DIM) < (HEAD_DIM // 2)
    cols = lambda p: slice(p * LANES, (p + 1) * LANES)

    @pl.when((pl.program_id(0) == 0) & (pl.program_id(1) == 0))
    def _():
        pos = lax.broadcasted_iota(jnp.int32, (t, 1), 0).astype(F32)
        r = lax.broadcasted_iota(jnp.int32, (t, t), 0)
        s = lax.broadcasted_iota(jnp.int32, (t, t), 1)
        dist = jnp.abs(r - s).astype(F32)
        visible = (s // CHUNK) <= (r // CHUNK)
        for h in range(RET_HEADS):
            lg = lg_ref[:, h * HEAD_DIM:h * HEAD_DIM + 1]
            dmat_ref[h] = jnp.where(visible, jnp.exp(lg * dist) * scale, 0.0)
        wq_ref[...] = jnp.exp(lg_ref[...] * (pos + 1.0)) * scale
        wk_ref[...] = jnp.exp(lg_ref[...] * (t - 1.0 - pos))

    @pl.when(pl.program_id(1) == 0)
    def _():
        state_ref[...] = jnp.zeros_like(state_ref)

    rb = lax.broadcasted_iota(jnp.int32, (LANES, LANES), 0)
    cb = lax.broadcasted_iota(jnp.int32, (LANES, LANES), 1)
    same_head = (rb // HEAD_DIM) == (cb // HEAD_DIM)

    def head_mean(x):
        s0 = jnp.sum(jnp.where(first, x, 0.0), axis=-1, keepdims=True)
        s1 = jnp.sum(jnp.where(first, 0.0, x), axis=-1, keepdims=True)
        return jnp.where(first, s0, s1) * (1.0 / HEAD_DIM)

    for sub in range(RET_BLK // t):
        rows = slice(sub * t, (sub + 1) * t)
        cos = cos_ref[rows, :]
        sin = sin_ref[rows, :]

        def rotary(x):
            swapped = jnp.where(low_half, pltpu.roll(x, LANES - HEAD_DIM // 2, 1),
                                pltpu.roll(x, HEAD_DIM // 2, 1))
            return x * cos + swapped * sin

        for p in range(n_pairs):
            q = rotary(q_ref[0, rows, cols(p)].astype(F32))
            k = rotary(k_ref[0, rows, cols(p)].astype(F32))
            vb = v_ref[0, rows, cols(p)]
            kb = k.astype(BF16)
            zero = jnp.zeros_like(q)
            inner = []
            for e in range(2):
                qe = jnp.where(first, q, zero) if e == 0 else jnp.where(first, zero, q)
                sc = lax.dot_general(qe.astype(BF16), kb, (((1,), (1,)), ((), ())),
                                     preferred_element_type=F32) * dmat_ref[2 * p + e]
                inner.append(jnp.dot(sc.astype(BF16), vb, preferred_element_type=F32))
            state = state_ref[p]
            o = jnp.where(first, inner[0], inner[1]) + jnp.dot(
                (q * wq_ref[:, cols(p)]).astype(BF16), state.astype(BF16),
                preferred_element_type=F32)
            upd = lax.dot_general((k * wk_ref[:, cols(p)]).astype(BF16), vb,
                                  (((0,), (0,)), ((), ())), preferred_element_type=F32)
            decay = jnp.exp(lg_ref[:, cols(p)] * float(t))
            state_ref[p] = jnp.where(same_head, state * decay + upd, 0.0)
            d = o - head_mean(o)
            var = head_mean(d * d)
            o_ref[0, rows, cols(p)] = (d * lax.rsqrt(var + EPS) * gn_ref[0, :, cols(p)]).astype(
                o_ref.dtype)


def _retention(rest, cos_t, sin_t, lg, gn_w, layer):
    batch, seq, _ = rest.shape
    n_pairs = D_RET // LANES
    tok = lambda blk: pl.BlockSpec((1, RET_BLK, D_RET), lambda b, i, blk=blk: (b, i, blk))
    fixed = pl.BlockSpec((1, D_RET), lambda b, i: (0, 0))
    table = pl.BlockSpec((RET_BLK, LANES), lambda b, i: (i, 0))
    return pl.pallas_call(
        _ret_kernel,
        grid=(batch, seq // RET_BLK),
        in_specs=[tok(4), tok(5), tok(6), table, table, fixed,
                  pl.BlockSpec((1, 1, D_RET), lambda b, i: (layer, 0, 0))],
        out_specs=pl.BlockSpec((1, RET_BLK, D_RET), lambda b, i: (b, i, 0)),
        out_shape=jax.ShapeDtypeStruct((batch, seq, D_RET), BF16),
        scratch_shapes=[
            pltpu.VMEM((n_pairs, LANES, LANES), F32),
            pltpu.VMEM((RET_HEADS, T_RET, T_RET), F32),
            pltpu.VMEM((T_RET, D_RET), F32),
            pltpu.VMEM((T_RET, D_RET), F32),
        ],
        compiler_params=_cparams(("arbitrary", "arbitrary")),
        name="retention",
    )(rest, rest, rest, cos_t, sin_t, lg, gn_w)


def _s5_kernel(u_ref, are_ref, aim_ref, ldt_ref, br_ref, bi_ref, cr_ref, ci_ref, d_ref, wg_ref,
               o_ref, wb_ref, wc_ref, lam_ref, lhs_ref, x_ref, z_ref, state_ref):
    nb = u_ref.shape[0]
    t5 = S5_T
    win = D_S5 // SUBLANES

    @pl.when(pl.program_id(0) == 0)
    def _():
        dt = jnp.exp(ldt_ref[0])
        ar = are_ref[0]
        ai = aim_ref[0]
        mag = jnp.exp(ar * dt)
        lr = mag * jnp.cos(ai * dt)
        li = mag * jnp.sin(ai * dt)
        den = ar * ar + ai * ai
        fr = ((lr - 1.0) * ar + li * ai) / den
        fi = (li * ar - (lr - 1.0) * ai) / den
        lam_ref[0] = lr
        lam_ref[1] = li
        for j in range(SUBLANES):
            rows = slice(win * j, win * (j + 1))
            f_r = fr[j:j + 1, :]
            f_i = fi[j:j + 1, :]
            b_r = br_ref[0, rows, :]
            b_i = bi_ref[0, rows, :]
            wb_ref[rows, :LANES] = (f_r * b_r - f_i * b_i).astype(BF16)
            wb_ref[rows, LANES:] = (f_r * b_i + f_i * b_r).astype(BF16)
        wc_ref[:LANES, :] = cr_ref[0].astype(BF16)
        wc_ref[LANES:, :] = (-ci_ref[0]).astype(BF16)
        state_ref[...] = jnp.zeros_like(state_ref)

    sub = lax.broadcasted_iota(jnp.int32, (2 * SUBLANES, D_S5), 0)
    keep = sub % SUBLANES == lax.broadcasted_iota(jnp.int32, (2 * SUBLANES, D_S5), 1) // win
    first = sub < SUBLANES

    for tt in range(t5 // SUBLANES):
        for b in range(nb):
            tile = u_ref[b, tt * SUBLANES:(tt + 1) * SUBLANES, :]
            for s in range(0, SUBLANES, 2):
                two = jnp.where(first,
                                jnp.broadcast_to(tile[s:s + 1, :], (2 * SUBLANES, D_S5)),
                                jnp.broadcast_to(tile[s + 1:s + 2, :], (2 * SUBLANES, D_S5)))
                row0 = (tt * SUBLANES + s) * SUBLANES
                lhs_ref[b, row0:row0 + 2 * SUBLANES, :] = jnp.where(keep, two, 0.0).astype(BF16)
    for b in range(nb):
        x_ref[b] = jnp.dot(lhs_ref[b], wb_ref[...], preferred_element_type=F32)

    lr = lam_ref[0]
    li = lam_ref[1]

    def scan(t, carry):
        rows = slice(t * SUBLANES, (t + 1) * SUBLANES)
        new = []
        for b in range(nb):
            xr, xi = carry[b]
            nr = lr * xr - li * xi + x_ref[b, rows, :LANES]
            ni = lr * xi + li * xr + x_ref[b, rows, LANES:]
            x_ref[b, rows, :LANES] = nr
            x_ref[b, rows, LANES:] = ni
            new.append((nr, ni))
        return tuple(new)

    final = tuple((state_ref[b, :, :LANES], state_ref[b, :, LANES:]) for b in range(nb))
    for t in range(t5):
        final = scan(t, final)
    for b in range(nb):
        state_ref[b, :, :LANES] = final[b][0]
        state_ref[b, :, LANES:] = final[b][1]

    window = lax.broadcasted_iota(jnp.int32, (t5, LANES), 1) // win
    for b in range(nb):
        z = jnp.dot(x_ref[b].astype(BF16), wc_ref[...], preferred_element_type=F32)
        z_ref[b, 0] = z[:, :LANES]
        z_ref[b, 1] = z[:, LANES:]
    for b in range(nb):
        halves = []
        for hh in range(2):
            per = SUBLANES // 2
            y = z_ref[b, hh, pl.ds(per * hh + per - 1, t5, stride=SUBLANES), :]
            for jj in range(per - 2, -1, -1):
                y = jnp.where(window == jj,
                              z_ref[b, hh, pl.ds(per * hh + jj, t5, stride=SUBLANES), :], y)
            halves.append(y)
        u = u_ref[b]
        y = jax.nn.gelu(jnp.concatenate(halves, axis=1) + d_ref[0] * u)
        gate = jnp.dot(y.astype(BF16), wg_ref[0], preferred_element_type=F32)
        o_ref[b] = (y * jax.nn.sigmoid(gate)).astype(o_ref.dtype)


def _s5(su, prm, layer):
    batch, seq, _ = su.shape
    fixed = lambda i: (layer, 0, 0)
    lam = pl.BlockSpec((1, SUBLANES, LANES), fixed)
    rows8 = SUBLANES * S5_T
    return pl.pallas_call(
        _s5_kernel,
        grid=(seq // S5_T,),
        in_specs=[
            pl.BlockSpec((batch, S5_T, D_S5), lambda i: (0, i, 0)),
            lam, lam, lam,
            pl.BlockSpec((1, D_S5, LANES), fixed), pl.BlockSpec((1, D_S5, LANES), fixed),
            pl.BlockSpec((1, LANES, D_S5), fixed), pl.BlockSpec((1, LANES, D_S5), fixed),
            pl.BlockSpec((1, 1, D_S5), fixed),
            pl.BlockSpec((1, D_S5, D_S5), fixed),
        ],
        out_specs=pl.BlockSpec((batch, S5_T, D_S5), lambda i: (0, i, 0)),
        out_shape=jax.ShapeDtypeStruct((batch, seq, D_S5), BF16),
        scratch_shapes=[
            pltpu.VMEM((D_S5, 2 * LANES), BF16),
            pltpu.VMEM((2 * LANES, D_S5), BF16),
            pltpu.VMEM((2, SUBLANES, LANES), F32),
            pltpu.VMEM((batch, rows8, D_S5), BF16),
            pltpu.VMEM((batch, rows8, 2 * LANES), F32),
            pltpu.VMEM((batch, 2, rows8, LANES), F32),
            pltpu.VMEM((batch, SUBLANES, 2 * LANES), F32),
        ],
        compiler_params=_cparams(("arbitrary",)),
        name="s5",
    )(su, prm["a_re"], prm["a_im"], prm["log_dt"], prm["b_re"], prm["b_im"],
      prm["c_re"], prm["c_im"], prm["d"], prm["w_glu"])


def _rotary_tables(seq):
    half = HEAD_DIM // 2
    lane = jnp.arange(LANES)
    freqs = ROPE_BASE ** (-(lane % half).astype(F32) / half)
    ang = jnp.arange(seq, dtype=F32)[:, None] * freqs[None, :]
    sign = jnp.where(lane % HEAD_DIM < half, -1.0, 1.0).astype(F32)
    return jnp.cos(ang), jnp.sin(ang) * sign[None, :]


def _s5_params(a_re, a_im, b_re, b_im, c_re, c_im, d, log_dt, w_glu):
    depth = a_re.shape[0]
    pair = jnp.eye(2, dtype=F32)
    half = S5_GROUPS // 2
    pack_b = lambda b: jnp.einsum(
        "ljnph,kn->ljkhnp", b.reshape(depth, half, 2, S5_STATE, S5_GROUP_CH), pair
    ).reshape(depth, D_S5, LANES)
    pack_c = lambda c: jnp.einsum(
        "ljnhp,kn->lnpjkh", c.reshape(depth, half, 2, S5_GROUP_CH, S5_STATE), pair
    ).reshape(depth, LANES, D_S5)
    tile = lambda v: v.reshape(depth, SUBLANES, LANES)
    return {
        "a_re": tile(a_re), "a_im": tile(a_im),
        "log_dt": tile(jnp.repeat(log_dt, S5_STATE, axis=-1)),
        "b_re": pack_b(b_re), "b_im": pack_b(b_im), "c_re": pack_c(c_re), "c_im": pack_c(c_im),
        "d": d.reshape(depth, 1, D_S5), "w_glu": w_glu.astype(BF16),
    }


def kernel(x, norm_w, w_in, fox_b_f, s5_a_re, s5_a_im, s5_b_re, s5_b_im, s5_c_re, s5_c_im,
           s5_d, s5_log_dt, s5_w_glu, ret_gn_w, w_out, final_norm_w):
    batch, seq, _ = x.shape
    depth = w_in.shape[0]
    m = batch * seq
    cos_t, sin_t = _rotary_tables(seq)
    log_gamma = jnp.log1p(-(2.0 ** (-5.0 - jnp.arange(RET_HEADS, dtype=F32))))
    lg = jnp.repeat(log_gamma, HEAD_DIM).reshape(1, D_RET)
    norm_g = jnp.concatenate([norm_w, final_norm_w[None]], axis=0).reshape(depth + 1, 1, D_MODEL)

    order = jnp.argsort(fox_b_f, axis=1).astype(jnp.int32)
    bf_sorted = jnp.take_along_axis(fox_b_f, order, axis=1)
    bf_pad = jnp.pad(bf_sorted, ((0, 0), (0, LANES - FOX_HEADS))).reshape(depth, 1, LANES)
    gn_w = ret_gn_w.reshape(depth, 1, D_RET)
    s5_prm = _s5_params(s5_a_re, s5_a_im, s5_b_re, s5_b_im, s5_c_re, s5_c_im, s5_d, s5_log_dt,
                        s5_w_glu)

    w_in_t = jnp.swapaxes(w_in, 1, 2)
    packed = [_pack_weights(order[l], w_in_t, w_out, l) for l in range(depth)]
    x2 = x.reshape(m, D_MODEL)
    qkv, su, rest, flog = _projections("first", m, norm_g, 0, (x2, norm_g, packed[0][0]))
    for l in range(depth):
        rest3 = rest.reshape(batch, seq, N_REST - D_S5)
        y_fox = _fox_attention(qkv.reshape(batch, seq, N_QKV), flog.reshape(batch, seq, LANES),
                               bf_pad, l)
        y_s5 = _s5(su.reshape(batch, seq, D_S5), s5_prm, l)
        y_ret = _retention(rest3, cos_t, sin_t, lg, gn_w, l)
        ys = (y_fox.reshape(m, D_FOX), y_s5.reshape(m, D_S5), y_ret.reshape(m, D_RET), rest, x2)
        if l < depth - 1:
            x2, qkv, su, rest, flog = _projections(
                "mid", m, norm_g, l + 1, ys + (packed[l][1], norm_g, packed[l + 1][0]))
    out = _projections("last", m, norm_g, depth, ys + (packed[depth - 1][1], norm_g))
    return out.reshape(batch, seq, D_MODEL)
```

```python
import math

import jax
import jax.numpy as jnp
from jax import lax
from jax.experimental import pallas as pl
from jax.experimental.pallas import tpu as pltpu

F32 = jnp.float32
BF16 = jnp.bfloat16

D_MODEL = 1024
HEAD_DIM = 64
CHUNK = 64
D_FOX = 512
FOX_HEADS = 8
D_S5 = 256
S5_GROUPS = 16
S5_GROUP_CH = 16
S5_STATE = 64
D_RET = 256
RET_HEADS = 4
ROPE_BASE = 10000.0
EPS = 1e-6

LANES = 128
SUBLANES = 8
N_QKV = 3 * D_FOX
N_REST = D_S5 + D_MODEL + 3 * D_RET
N_ALL = N_QKV + N_REST + LANES
D_IN_PROJ = N_QKV + FOX_HEADS + N_REST
VMEM_LIMIT = 56 * 1024 * 1024

TM_PROJ = 512
T_FOX = 512
FOX_GROUP = 2
T_RET = 256
RET_BLK = 512
S5_T = 256
NEG_BIG = -1e30
FOX_SKIP_LOG2 = 40.0
FOX_NORM_SLACK = 1.02
LOG2E = math.log2(math.e)
Q_SCALE = LOG2E / math.sqrt(HEAD_DIM)


def _cparams(sem):
    return pltpu.CompilerParams(dimension_semantics=sem, vmem_limit_bytes=VMEM_LIMIT)


def _rms_norm(x, g):
    ms = jnp.mean(x * x, axis=-1, keepdims=True)
    return x * lax.rsqrt(ms + EPS) * g


def _pack_kernel(order_ref, wt_ref, wout_ref, win_ref, wo_ref):
    su_lo = N_QKV + FOX_HEADS
    rqkv_lo = su_lo + D_S5
    gate_lo = rqkv_lo + 3 * D_RET

    def copy_rows(dst, src, size):
        win_ref[dst:dst + size, :] = wt_ref[0, src, :].astype(BF16)

    def copy_heads(dst, src):
        for n in range(FOX_HEADS):
            start = pl.multiple_of(src + order_ref[n] * HEAD_DIM, HEAD_DIM)
            copy_rows(dst + n * HEAD_DIM, pl.ds(start, HEAD_DIM), HEAD_DIM)

    def copy_plain(dst, lo, hi):
        for start in range(lo, hi, 256):
            size = min(256, hi - start)
            copy_rows(dst + start - lo, slice(start, start + size), size)

    for part in range(3):
        copy_heads(part * D_FOX, part * D_FOX)
    copy_plain(N_QKV, su_lo, rqkv_lo)
    copy_heads(N_QKV + D_S5, gate_lo)
    copy_plain(N_QKV + D_S5 + D_FOX, gate_lo + D_FOX, D_IN_PROJ)
    copy_plain(N_QKV + D_S5 + D_MODEL, rqkv_lo, gate_lo)
    flog = [wt_ref[0, pl.ds(N_QKV + order_ref[n], 1), :] for n in range(FOX_HEADS)]
    flog.append(jnp.zeros((LANES - FOX_HEADS, D_MODEL), F32))
    win_ref[N_QKV + N_REST:, :] = jnp.concatenate(flog, axis=0).astype(BF16)

    for n in range(FOX_HEADS):
        src = pl.ds(pl.multiple_of(order_ref[n] * HEAD_DIM, HEAD_DIM), HEAD_DIM)
        wo_ref[n * HEAD_DIM:(n + 1) * HEAD_DIM, :] = wout_ref[0, src, :].astype(BF16)
    wo_ref[D_FOX:, :] = wout_ref[0, D_FOX:, :].astype(BF16)


def _pack_weights(order, w_in_t, w_out, layer):
    grid_spec = pltpu.PrefetchScalarGridSpec(
        num_scalar_prefetch=1,
        grid=(1,),
        in_specs=[
            pl.BlockSpec((1, D_IN_PROJ, D_MODEL), lambda i, order: (layer, 0, 0)),
            pl.BlockSpec((1, D_MODEL, D_MODEL), lambda i, order: (layer, 0, 0)),
        ],
        out_specs=[
            pl.BlockSpec((N_ALL, D_MODEL), lambda i, order: (0, 0)),
            pl.BlockSpec((D_MODEL, D_MODEL), lambda i, order: (0, 0)),
        ],
    )
    return pl.pallas_call(
        _pack_kernel,
        grid_spec=grid_spec,
        out_shape=[jax.ShapeDtypeStruct((N_ALL, D_MODEL), BF16),
                   jax.ShapeDtypeStruct((D_MODEL, D_MODEL), BF16)],
        compiler_params=_cparams(("arbitrary",)),
        name="pack_weights",
    )(order, w_in_t, w_out)


def _project_in(h, w_ref, oa_ref, su_ref, ob_ref, fl_ref):
    nt = (((1,), (1,)), ((), ()))
    qkv = lax.dot_general(h, w_ref[:N_QKV, :], nt, preferred_element_type=F32)
    oa_ref[:, :D_FOX] = (qkv[:, :D_FOX] * Q_SCALE).astype(oa_ref.dtype)
    oa_ref[:, D_FOX:] = qkv[:, D_FOX:].astype(oa_ref.dtype)
    rest = lax.dot_general(h, w_ref[N_QKV:N_QKV + N_REST, :], nt, preferred_element_type=F32)
    su_ref[...] = rest[:, :D_S5]
    ob_ref[...] = rest[:, D_S5:].astype(ob_ref.dtype)
    fl_ref[...] = lax.dot_general(h, w_ref[N_QKV + N_REST:, :], nt, preferred_element_type=F32)


def _project_out(yf_ref, ys_ref, yr_ref, gate_ref, x_ref, w_ref):
    g = gate_ref[...].astype(F32)
    g = g * jax.nn.sigmoid(g)
    acc = x_ref[...]
    off = 0
    for y_ref in (yf_ref, ys_ref, yr_ref):
        width = y_ref.shape[-1]
        y = (y_ref[...] * g[:, off:off + width]).astype(BF16)
        acc = acc + jnp.dot(y, w_ref[off:off + width, :], preferred_element_type=F32)
        off += width
    return acc


def _first_kernel(x_ref, g_ref, win_ref, oa_ref, su_ref, ob_ref, fl_ref):
    h = _rms_norm(x_ref[...], g_ref[0]).astype(BF16)
    _project_in(h, win_ref, oa_ref, su_ref, ob_ref, fl_ref)


def _mid_kernel(yf_ref, ys_ref, yr_ref, gate_ref, x_ref, wo_ref, g_ref, win_ref,
                xo_ref, oa_ref, su_ref, ob_ref, fl_ref):
    acc = _project_out(yf_ref, ys_ref, yr_ref, gate_ref, x_ref, wo_ref)
    xo_ref[...] = acc
    h = _rms_norm(acc, g_ref[0]).astype(BF16)
    _project_in(h, win_ref, oa_ref, su_ref, ob_ref, fl_ref)


def _last_kernel(yf_ref, ys_ref, yr_ref, gate_ref, x_ref, wo_ref, g_ref, o_ref):
    acc = _project_out(yf_ref, ys_ref, yr_ref, gate_ref, x_ref, wo_ref)
    o_ref[...] = _rms_norm(acc, g_ref[0])


def _projections(kind, m, norm_g, norm_layer, operands):
    row = lambda i: (i, 0)
    fixed = lambda i: (0, 0)
    tile = pl.BlockSpec((TM_PROJ, D_MODEL), row)
    once = pl.Buffered(1)
    norm_spec = pl.BlockSpec((1, 1, D_MODEL), lambda i: (norm_layer, 0, 0))
    win_spec = pl.BlockSpec((N_ALL, D_MODEL), fixed, pipeline_mode=once)
    wo_spec = pl.BlockSpec((D_MODEL, D_MODEL), fixed, pipeline_mode=once)
    out_in_specs = [pl.BlockSpec((TM_PROJ, D_FOX), row), pl.BlockSpec((TM_PROJ, D_S5), row),
                    pl.BlockSpec((TM_PROJ, D_RET), row), tile, tile]
    in_out_specs = [pl.BlockSpec((TM_PROJ, N_QKV), row), pl.BlockSpec((TM_PROJ, D_S5), row),
                    pl.BlockSpec((TM_PROJ, N_REST - D_S5), row),
                    pl.BlockSpec((TM_PROJ, LANES), row)]
    in_out_shapes = [jax.ShapeDtypeStruct((m, N_QKV), BF16), jax.ShapeDtypeStruct((m, D_S5), F32),
                     jax.ShapeDtypeStruct((m, N_REST - D_S5), BF16),
                     jax.ShapeDtypeStruct((m, LANES), F32)]
    x_shape = jax.ShapeDtypeStruct((m, D_MODEL), F32)
    if kind == "first":
        body, in_specs = _first_kernel, [tile, norm_spec, win_spec]
        out_specs, out_shape = in_out_specs, in_out_shapes
    elif kind == "mid":
        body, in_specs = _mid_kernel, out_in_specs + [wo_spec, norm_spec, win_spec]
        out_specs, out_shape = [tile] + in_out_specs, [x_shape] + in_out_shapes
    else:
        body, in_specs = _last_kernel, out_in_specs + [wo_spec, norm_spec]
        out_specs, out_shape = tile, x_shape
    return pl.pallas_call(
        body,
        grid=(m // TM_PROJ,),
        in_specs=in_specs,
        out_specs=out_specs,
        out_shape=out_shape,
        compiler_params=_cparams(("parallel",)),
        name="proj_" + kind,
    )(*operands)


def _fox_kernel(q_ref, k_ref, v_ref, fl_ref, bf_ref, o_ref, kaug_ref, vaug_ref, c_ref, clast_ref,
                knorm_ref):
    t = T_FOX
    i = pl.program_id(1)
    seq = k_ref.shape[1]
    lane = lax.broadcasted_iota(jnp.int32, (t, LANES), 1)
    lane1 = lane[0:1, :]
    own = (lane < HEAD_DIM, lane >= HEAD_DIM)
    n_pairs = D_FOX // LANES
    heads = [(g, e) for g in range(n_pairs) for e in range(2)]
    cols = lambda g: slice(g * LANES, (g + 1) * LANES)
    x0 = lambda e: HEAD_DIM if e == 0 else 0
    shift = lambda g, e: (x0(e) - (2 * g + e)) % LANES

    half_id = lax.broadcasted_iota(jnp.int32, (LANES, LANES), 0) // HEAD_DIM
    same_half = (half_id == lax.broadcasted_iota(jnp.int32, (LANES, LANES), 1) // HEAD_DIM)
    head_sum = jnp.where(same_half, 1.0, 0.0).astype(BF16)

    def max_sq_norms(ref, rows):
        out = jnp.zeros((1, LANES), F32)
        for g in range(n_pairs):
            x = ref[0, rows, cols(g)]
            sums = jnp.dot(x * x, head_sum, preferred_element_type=F32)
            top = jnp.max(sums, axis=0, keepdims=True) * FOX_NORM_SLACK
            out = jnp.where(lane1 == 2 * g, top,
                            jnp.where(lane1 == 2 * g + 1, pltpu.roll(top, HEAD_DIM, 1), out))
        return out

    @pl.when(i == 0)
    def _():
        r = lax.broadcasted_iota(jnp.int32, (t, t), 0)
        s = lax.broadcasted_iota(jnp.int32, (t, t), 1)
        tri = jnp.where(s <= r, 1.0, 0.0).astype(BF16)

        def build(j, carried):
            carry, k_max = carried
            rows = pl.ds(pl.multiple_of(j * t, t), t)
            z = fl_ref[0, rows, :] + bf_ref[0]
            logf = jnp.minimum(z, 0.0) - jnp.log1p(jnp.exp(-jnp.abs(z)))
            top = logf.astype(BF16)
            rem = logf - top.astype(F32)
            middle = rem.astype(BF16)
            bottom = (rem - middle.astype(F32)).astype(BF16)
            c = carry + sum(jnp.dot(tri, piece, preferred_element_type=F32)
                            for piece in (bottom, middle, top))
            c_ref[rows, :] = c
            clast_ref[pl.ds(j, 1), :] = c[t - 1:t, :]
            k_max = jnp.maximum(k_max, max_sq_norms(k_ref, rows))
            knorm_ref[pl.ds(j, 1), :] = k_max
            b = -LOG2E * c
            hi = b.astype(BF16).astype(F32)
            mid = (b - hi).astype(BF16).astype(F32)
            lo = b - hi - mid
            pieces = jnp.where(lane < 8, hi, jnp.where(lane < 16, pltpu.roll(mid, 8, 1), jnp.where(
                lane < 24, pltpu.roll(lo, 16, 1), jnp.where(lane < 32, 1.0, 0.0))))
            for n, (g, e) in enumerate(heads):
                extra = pltpu.roll(pieces, shift(g, e), 1).astype(BF16)
                kaug_ref[n, rows, :] = jnp.where(own[e], k_ref[0, rows, cols(g)], extra)
                v = v_ref[0, rows, cols(g)]
                vaug_ref[n, rows, :] = jnp.where(own[e], v, jnp.ones_like(v))
            return c[t - 1:t, :], k_max

        zero_row = jnp.zeros((1, LANES), F32)
        carried = (zero_row, zero_row)
        for j in range(seq // t):
            carried = build(j, carried)

    cq0 = LOG2E * c_ref[pl.ds(pl.multiple_of(i * t, t), SUBLANES), :][0:1, :]
    qa = []
    for g, e in heads:
        ones = (lane1 == x0(e)) | (lane1 == x0(e) + 8) | (lane1 == x0(e) + 16)
        extra = jnp.where(ones, 1.0, jnp.where(
            lane1 == x0(e) + 24, pltpu.roll(cq0, (shift(g, e) + 24) % LANES, 1), 0.0))
        qa.append(jnp.where(own[e], q_ref[0, :, cols(g)], extra.astype(BF16)))

    def tile(j, carry, masked, ids):
        m, acc = carry
        rows = pl.ds(pl.multiple_of(j * t, t), t)
        new_m, new_acc = [], []
        for pos, n in enumerate(ids):
            s = lax.dot_general(qa[n], kaug_ref[n, rows, :], (((1,), (1,)), ((), ())),
                                preferred_element_type=F32)
            if masked:
                r = lax.broadcasted_iota(jnp.int32, (t, t), 0)
                cidx = lax.broadcasted_iota(jnp.int32, (t, t), 1)
                s = jnp.where(cidx <= r, s, NEG_BIG)
            m_new = jnp.maximum(m[pos], jnp.max(s, axis=-1, keepdims=True))
            alpha = jnp.exp2(m[pos] - m_new)
            p = jnp.exp2(s - m_new).astype(BF16)
            new_m.append(m_new)
            new_acc.append(acc[pos] * alpha + jnp.dot(p, vaug_ref[n, rows, :],
                                                      preferred_element_type=F32))
        return tuple(new_m), tuple(new_acc)

    n_tiles = seq // t
    tile_id = lax.broadcasted_iota(jnp.int32, (n_tiles, LANES), 0)
    cq0_used = cq0.astype(BF16).astype(F32)
    upper = (jnp.sqrt(max_sq_norms(q_ref, slice(None)) * knorm_ref[...])
             + (cq0_used - LOG2E * clast_ref[...]))

    def needed(j, m, ids):
        row_min = jnp.zeros((1, LANES), F32)
        for pos, n in enumerate(ids):
            row_min = jnp.where(lane1 == n, jnp.min(m[pos], axis=0, keepdims=True), row_min)
        upper_j = jnp.sum(jnp.where(tile_id == j, upper, 0.0), axis=0, keepdims=True)
        hit = (upper_j >= row_min - FOX_SKIP_LOG2) & (lane1 >= ids[0]) & (lane1 <= ids[-1])
        return (j >= 0) & (jnp.max(jnp.where(hit, 1.0, 0.0)) > 0.0)

    acc = []
    for first in range(0, len(heads), FOX_GROUP):
        ids = list(range(first, first + FOX_GROUP))
        init = ((jnp.full((t, 1), NEG_BIG, F32),) * FOX_GROUP,
                (jnp.zeros((t, LANES), F32),) * FOX_GROUP)
        m, group_acc = tile(i, init, True, ids)

        def walk(state, ids=ids):
            j, _, m, group_acc = state
            m, group_acc = tile(j, (m, group_acc), False, ids)
            return j - 1, needed(j - 1, m, ids), m, group_acc

        state = lax.while_loop(lambda state: state[1], walk,
                               (i - 1, needed(i - 1, m, ids), m, group_acc))
        acc.extend(state[3])
    outs = [a / pltpu.roll(a, HEAD_DIM, 1) for a in acc]
    for g in range(n_pairs):
        o_ref[0, :, cols(g)] = jnp.where(own[0], outs[2 * g], outs[2 * g + 1]).astype(o_ref.dtype)


def _fox_attention(qkv, flog, bf_pad, layer):
    batch, seq, _ = qkv.shape
    n_heads = D_FOX // HEAD_DIM
    once = pl.Buffered(1)
    return pl.pallas_call(
        _fox_kernel,
        grid=(batch, seq // T_FOX),
        in_specs=[
            pl.BlockSpec((1, T_FOX, D_FOX), lambda b, i: (b, i, 0)),
            pl.BlockSpec((1, seq, D_FOX), lambda b, i: (b, 0, 1), pipeline_mode=once),
            pl.BlockSpec((1, seq, D_FOX), lambda b, i: (b, 0, 2), pipeline_mode=once),
            pl.BlockSpec((1, seq, LANES), lambda b, i: (b, 0, 0), pipeline_mode=once),
            pl.BlockSpec((1, 1, LANES), lambda b, i: (layer, 0, 0)),
        ],
        out_specs=pl.BlockSpec((1, T_FOX, D_FOX), lambda b, i: (b, i, 0)),
        out_shape=jax.ShapeDtypeStruct((batch, seq, D_FOX), BF16),
        scratch_shapes=[pltpu.VMEM((n_heads, seq, LANES), BF16),
                        pltpu.VMEM((n_heads, seq, LANES), BF16),
                        pltpu.VMEM((seq, LANES), F32),
                        pltpu.VMEM((seq // T_FOX, LANES), F32),
                        pltpu.VMEM((seq // T_FOX, LANES), F32)],
        compiler_params=_cparams(("parallel", "arbitrary")),
        name="fox_attention",
    )(qkv, qkv, qkv, flog, bf_pad)


def _ret_kernel(q_ref, k_ref, v_ref, cos_ref, sin_ref, lg_ref, gn_ref, o_ref,
                state_ref, dmat_ref, wq_ref, wk_ref):
    t = T_RET
    n_pairs = D_RET // LANES
    scale = 1.0 / math.sqrt(HEAD_DIM)
    lane = lax.broadcasted_iota(jnp.int32, (t, LANES), 1)
    first = lane < HEAD_DIM
    low_half = (lane % HEAD_DIM) < (HEAD_DIM // 2)
    cols = lambda p: slice(p * LANES, (p + 1) * LANES)

    @pl.when((pl.program_id(0) == 0) & (pl.program_id(1) == 0))
    def _():
        pos = lax.broadcasted_iota(jnp.int32, (t, 1), 0).astype(F32)
        r = lax.broadcasted_iota(jnp.int32, (t, t), 0)
        s = lax.broadcasted_iota(jnp.int32, (t, t), 1)
        dist = jnp.abs(r - s).astype(F32)
        visible = (s // CHUNK) <= (r // CHUNK)
        for h in range(RET_HEADS):
            lg = lg_ref[:, h * HEAD_DIM:h * HEAD_DIM + 1]
            dmat_ref[h] = jnp.where(visible, jnp.exp(lg * dist) * scale, 0.0)
        wq_ref[...] = jnp.exp(lg_ref[...] * (pos + 1.0)) * scale
        wk_ref[...] = jnp.exp(lg_ref[...] * (t - 1.0 - pos))

    @pl.when(pl.program_id(1) == 0)
    def _():
        state_ref[...] = jnp.zeros_like(state_ref)

    rb = lax.broadcasted_iota(jnp.int32, (LANES, LANES), 0)
    cb = lax.broadcasted_iota(jnp.int32, (LANES, LANES), 1)
    same_head = (rb // HEAD_DIM) == (cb // HEAD_DIM)

    def head_mean(x):
        s0 = jnp.sum(jnp.where(first, x, 0.0), axis=-1, keepdims=True)
        s1 = jnp.sum(jnp.where(first, 0.0, x), axis=-1, keepdims=True)
        return jnp.where(first, s0, s1) * (1.0 / HEAD_DIM)

    for sub in range(RET_BLK // t):
        rows = slice(sub * t, (sub + 1) * t)
        cos = cos_ref[rows, :]
        sin = sin_ref[rows, :]

        def rotary(x):
            swapped = jnp.where(low_half, pltpu.roll(x, LANES - HEAD_DIM // 2, 1),
                                pltpu.roll(x, HEAD_DIM // 2, 1))
            return x * cos + swapped * sin

        for p in range(n_pairs):
            q = rotary(q_ref[0, rows, cols(p)].astype(F32))
            k = rotary(k_ref[0, rows, cols(p)].astype(F32))
            vb = v_ref[0, rows, cols(p)]
            kb = k.astype(BF16)
            zero = jnp.zeros_like(q)
            inner = []
            for e in range(2):
                qe = jnp.where(first, q, zero) if e == 0 else jnp.where(first, zero, q)
                sc = lax.dot_general(qe.astype(BF16), kb, (((1,), (1,)), ((), ())),
                                     preferred_element_type=F32) * dmat_ref[2 * p + e]
                inner.append(jnp.dot(sc.astype(BF16), vb, preferred_element_type=F32))
            state = state_ref[p]
            o = jnp.where(first, inner[0], inner[1]) + jnp.dot(
                (q * wq_ref[:, cols(p)]).astype(BF16), state.astype(BF16),
                preferred_element_type=F32)
            upd = lax.dot_general((k * wk_ref[:, cols(p)]).astype(BF16), vb,
                                  (((0,), (0,)), ((), ())), preferred_element_type=F32)
            decay = jnp.exp(lg_ref[:, cols(p)] * float(t))
            state_ref[p] = jnp.where(same_head, state * decay + upd, 0.0)
            d = o - head_mean(o)
            var = head_mean(d * d)
            o_ref[0, rows, cols(p)] = (d * lax.rsqrt(var + EPS) * gn_ref[0, :, cols(p)]).astype(
                o_ref.dtype)


def _retention(rest, cos_t, sin_t, lg, gn_w, layer):
    batch, seq, _ = rest.shape
    n_pairs = D_RET // LANES
    tok = lambda blk: pl.BlockSpec((1, RET_BLK, D_RET), lambda b, i, blk=blk: (b, i, blk))
    fixed = pl.BlockSpec((1, D_RET), lambda b, i: (0, 0))
    table = pl.BlockSpec((RET_BLK, LANES), lambda b, i: (i, 0))
    return pl.pallas_call(
        _ret_kernel,
        grid=(batch, seq // RET_BLK),
        in_specs=[tok(4), tok(5), tok(6), table, table, fixed,
                  pl.BlockSpec((1, 1, D_RET), lambda b, i: (layer, 0, 0))],
        out_specs=pl.BlockSpec((1, RET_BLK, D_RET), lambda b, i: (b, i, 0)),
        out_shape=jax.ShapeDtypeStruct((batch, seq, D_RET), BF16),
        scratch_shapes=[
            pltpu.VMEM((n_pairs, LANES, LANES), F32),
            pltpu.VMEM((RET_HEADS, T_RET, T_RET), F32),
            pltpu.VMEM((T_RET, D_RET), F32),
            pltpu.VMEM((T_RET, D_RET), F32),
        ],
        compiler_params=_cparams(("arbitrary", "arbitrary")),
        name="retention",
    )(rest, rest, rest, cos_t, sin_t, lg, gn_w)


def _s5_kernel(u_ref, are_ref, aim_ref, ldt_ref, br_ref, bi_ref, cr_ref, ci_ref, d_ref, wg_ref,
               o_ref, wb_ref, wc_ref, lam_ref, lhs_ref, x_ref, z_ref, state_ref):
    nb = u_ref.shape[0]
    t5 = S5_T
    win = D_S5 // SUBLANES

    @pl.when(pl.program_id(0) == 0)
    def _():
        dt = jnp.exp(ldt_ref[0])
        ar = are_ref[0]
        ai = aim_ref[0]
        mag = jnp.exp(ar * dt)
        lr = mag * jnp.cos(ai * dt)
        li = mag * jnp.sin(ai * dt)
        den = ar * ar + ai * ai
        fr = ((lr - 1.0) * ar + li * ai) / den
        fi = (li * ar - (lr - 1.0) * ai) / den
        lam_ref[0] = lr
        lam_ref[1] = li
        for j in range(SUBLANES):
            rows = slice(win * j, win * (j + 1))
            f_r = fr[j:j + 1, :]
            f_i = fi[j:j + 1, :]
            b_r = br_ref[0, rows, :]
            b_i = bi_ref[0, rows, :]
            wb_ref[rows, :LANES] = (f_r * b_r - f_i * b_i).astype(BF16)
            wb_ref[rows, LANES:] = (f_r * b_i + f_i * b_r).astype(BF16)
        wc_ref[:LANES, :] = cr_ref[0].astype(BF16)
        wc_ref[LANES:, :] = (-ci_ref[0]).astype(BF16)
        state_ref[...] = jnp.zeros_like(state_ref)

    sub = lax.broadcasted_iota(jnp.int32, (2 * SUBLANES, D_S5), 0)
    keep = sub % SUBLANES == lax.broadcasted_iota(jnp.int32, (2 * SUBLANES, D_S5), 1) // win
    first = sub < SUBLANES

    for tt in range(t5 // SUBLANES):
        for b in range(nb):
            tile = u_ref[b, tt * SUBLANES:(tt + 1) * SUBLANES, :]
            for s in range(0, SUBLANES, 2):
                two = jnp.where(first,
                                jnp.broadcast_to(tile[s:s + 1, :], (2 * SUBLANES, D_S5)),
                                jnp.broadcast_to(tile[s + 1:s + 2, :], (2 * SUBLANES, D_S5)))
                row0 = (tt * SUBLANES + s) * SUBLANES
                lhs_ref[b, row0:row0 + 2 * SUBLANES, :] = jnp.where(keep, two, 0.0).astype(BF16)
    for b in range(nb):
        x_ref[b] = jnp.dot(lhs_ref[b], wb_ref[...], preferred_element_type=F32)

    lr = lam_ref[0]
    li = lam_ref[1]

    def scan(t, carry):
        rows = slice(t * SUBLANES, (t + 1) * SUBLANES)
        new = []
        for b in range(nb):
            xr, xi = carry[b]
            nr = lr * xr - li * xi + x_ref[b, rows, :LANES]
            ni = lr * xi + li * xr + x_ref[b, rows, LANES:]
            x_ref[b, rows, :LANES] = nr
            x_ref[b, rows, LANES:] = ni
            new.append((nr, ni))
        return tuple(new)

    final = tuple((state_ref[b, :, :LANES], state_ref[b, :, LANES:]) for b in range(nb))
    for t in range(t5):
        final = scan(t, final)
    for b in range(nb):
        state_ref[b, :, :LANES] = final[b][0]
        state_ref[b, :, LANES:] = final[b][1]

    window = lax.broadcasted_iota(jnp.int32, (t5, LANES), 1) // win
    for b in range(nb):
        z = jnp.dot(x_ref[b].astype(BF16), wc_ref[...], preferred_element_type=F32)
        z_ref[b, 0] = z[:, :LANES]
        z_ref[b, 1] = z[:, LANES:]
    for b in range(nb):
        halves = []
        for hh in range(2):
            per = SUBLANES // 2
            y = z_ref[b, hh, pl.ds(per * hh + per - 1, t5, stride=SUBLANES), :]
            for jj in range(per - 2, -1, -1):
                y = jnp.where(window == jj,
                              z_ref[b, hh, pl.ds(per * hh + jj, t5, stride=SUBLANES), :], y)
            halves.append(y)
        u = u_ref[b]
        y = jax.nn.gelu(jnp.concatenate(halves, axis=1) + d_ref[0] * u)
        gate = jnp.dot(y.astype(BF16), wg_ref[0], preferred_element_type=F32)
        o_ref[b] = (y * jax.nn.sigmoid(gate)).astype(o_ref.dtype)


def _s5(su, prm, layer):
    batch, seq, _ = su.shape
    fixed = lambda i: (layer, 0, 0)
    lam = pl.BlockSpec((1, SUBLANES, LANES), fixed)
    rows8 = SUBLANES * S5_T
    return pl.pallas_call(
        _s5_kernel,
        grid=(seq // S5_T,),
        in_specs=[
            pl.BlockSpec((batch, S5_T, D_S5), lambda i: (0, i, 0)),
            lam, lam, lam,
            pl.BlockSpec((1, D_S5, LANES), fixed), pl.BlockSpec((1, D_S5, LANES), fixed),
            pl.BlockSpec((1, LANES, D_S5), fixed), pl.BlockSpec((1, LANES, D_S5), fixed),
            pl.BlockSpec((1, 1, D_S5), fixed),
            pl.BlockSpec((1, D_S5, D_S5), fixed),
        ],
        out_specs=pl.BlockSpec((batch, S5_T, D_S5), lambda i: (0, i, 0)),
        out_shape=jax.ShapeDtypeStruct((batch, seq, D_S5), BF16),
        scratch_shapes=[
            pltpu.VMEM((D_S5, 2 * LANES), BF16),
            pltpu.VMEM((2 * LANES, D_S5), BF16),
            pltpu.VMEM((2, SUBLANES, LANES), F32),
            pltpu.VMEM((batch, rows8, D_S5), BF16),
            pltpu.VMEM((batch, rows8, 2 * LANES), F32),
            pltpu.VMEM((batch, 2, rows8, LANES), F32),
            pltpu.VMEM((batch, SUBLANES, 2 * LANES), F32),
        ],
        compiler_params=_cparams(("arbitrary",)),
        name="s5",
    )(su, prm["a_re"], prm["a_im"], prm["log_dt"], prm["b_re"], prm["b_im"],
      prm["c_re"], prm["c_im"], prm["d"], prm["w_glu"])


def _rotary_tables(seq):
    half = HEAD_DIM // 2
    lane = jnp.arange(LANES)
    freqs = ROPE_BASE ** (-(lane % half).astype(F32) / half)
    ang = jnp.arange(seq, dtype=F32)[:, None] * freqs[None, :]
    sign = jnp.where(lane % HEAD_DIM < half, -1.0, 1.0).astype(F32)
    return jnp.cos(ang), jnp.sin(ang) * sign[None, :]


def _s5_params(a_re, a_im, b_re, b_im, c_re, c_im, d, log_dt, w_glu):
    depth = a_re.shape[0]
    pair = jnp.eye(2, dtype=F32)
    half = S5_GROUPS // 2
    pack_b = lambda b: jnp.einsum(
        "ljnph,kn->ljkhnp", b.reshape(depth, half, 2, S5_STATE, S5_GROUP_CH), pair
    ).reshape(depth, D_S5, LANES)
    pack_c = lambda c: jnp.einsum(
        "ljnhp,kn->lnpjkh", c.reshape(depth, half, 2, S5_GROUP_CH, S5_STATE), pair
    ).reshape(depth, LANES, D_S5)
    tile = lambda v: v.reshape(depth, SUBLANES, LANES)
    return {
        "a_re": tile(a_re), "a_im": tile(a_im),
        "log_dt": tile(jnp.repeat(log_dt, S5_STATE, axis=-1)),
        "b_re": pack_b(b_re), "b_im": pack_b(b_im), "c_re": pack_c(c_re), "c_im": pack_c(c_im),
        "d": d.reshape(depth, 1, D_S5), "w_glu": w_glu.astype(BF16),
    }


def kernel(x, norm_w, w_in, fox_b_f, s5_a_re, s5_a_im, s5_b_re, s5_b_im, s5_c_re, s5_c_im,
           s5_d, s5_log_dt, s5_w_glu, ret_gn_w, w_out, final_norm_w):
    batch, seq, _ = x.shape
    depth = w_in.shape[0]
    m = batch * seq
    cos_t, sin_t = _rotary_tables(seq)
    log_gamma = jnp.log1p(-(2.0 ** (-5.0 - jnp.arange(RET_HEADS, dtype=F32))))
    lg = jnp.repeat(log_gamma, HEAD_DIM).reshape(1, D_RET)
    norm_g = jnp.concatenate([norm_w, final_norm_w[None]], axis=0).reshape(depth + 1, 1, D_MODEL)

    order = jnp.argsort(fox_b_f, axis=1).astype(jnp.int32)
    bf_sorted = jnp.take_along_axis(fox_b_f, order, axis=1)
    bf_pad = jnp.pad(bf_sorted, ((0, 0), (0, LANES - FOX_HEADS))).reshape(depth, 1, LANES)
    gn_w = ret_gn_w.reshape(depth, 1, D_RET)
    s5_prm = _s5_params(s5_a_re, s5_a_im, s5_b_re, s5_b_im, s5_c_re, s5_c_im, s5_d, s5_log_dt,
                        s5_w_glu)

    w_in_t = jnp.swapaxes(w_in, 1, 2)
    packed = [_pack_weights(order[l], w_in_t, w_out, l) for l in range(depth)]
    x2 = x.reshape(m, D_MODEL)
    qkv, su, rest, flog = _projections("first", m, norm_g, 0, (x2, norm_g, packed[0][0]))
    for l in range(depth):
        rest3 = rest.reshape(batch, seq, N_REST - D_S5)
        y_fox = _fox_attention(qkv.reshape(batch, seq, N_QKV), flog.reshape(batch, seq, LANES),
                               bf_pad, l)
        y_s5 = _s5(su.reshape(batch, seq, D_S5), s5_prm, l)
        y_ret = _retention(rest3, cos_t, sin_t, lg, gn_w, l)
        ys = (y_fox.reshape(m, D_FOX), y_s5.reshape(m, D_S5), y_ret.reshape(m, D_RET), rest, x2)
        if l < depth - 1:
            x2, qkv, su, rest, flog = _projections(
                "mid", m, norm_g, l + 1, ys + (packed[l][1], norm_g, packed[l + 1][0]))
    out = _projections("last", m, norm_g, depth, ys + (packed[depth - 1][1], norm_g))
    return out.reshape(batch, seq, D_MODEL)
```

```python
import math

import jax
import jax.numpy as jnp
from jax import lax
from jax.experimental import pallas as pl
from jax.experimental.pallas import tpu as pltpu

F32 = jnp.float32
BF16 = jnp.bfloat16

D_MODEL = 1024
HEAD_DIM = 64
CHUNK = 64
D_FOX = 512
FOX_HEADS = 8
D_S5 = 256
S5_GROUPS = 16
S5_GROUP_CH = 16
S5_STATE = 64
D_RET = 256
RET_HEADS = 4
ROPE_BASE = 10000.0
EPS = 1e-6

LANES = 128
SUBLANES = 8
N_QKV = 3 * D_FOX
N_REST = D_S5 + D_MODEL + 3 * D_RET
N_ALL = N_QKV + N_REST + LANES
D_IN_PROJ = N_QKV + FOX_HEADS + N_REST
VMEM_LIMIT = 56 * 1024 * 1024

TM_PROJ = 512
T_FOX = 512
FOX_GROUP = 2
T_RET = 256
RET_BLK = 512
S5_T = 256
NEG_BIG = -1e30
FOX_SKIP_LOG2 = 40.0
FOX_NORM_SLACK = 1.02
LOG2E = math.log2(math.e)
Q_SCALE = LOG2E / math.sqrt(HEAD_DIM)


def _cparams(sem):
    return pltpu.CompilerParams(dimension_semantics=sem, vmem_limit_bytes=VMEM_LIMIT)


def _rms_norm(x, g):
    ms = jnp.mean(x * x, axis=-1, keepdims=True)
    return x * lax.rsqrt(ms + EPS) * g


def _pack_in_proj(order_ref, wt_ref, win_ref):
    su_lo = N_QKV + FOX_HEADS
    rqkv_lo = su_lo + D_S5
    gate_lo = rqkv_lo + 3 * D_RET

    def copy_rows(dst, src, size):
        win_ref[dst:dst + size, :] = wt_ref[0, src, :].astype(BF16)

    def copy_heads(dst, src):
        for n in range(FOX_HEADS):
            start = pl.multiple_of(src + order_ref[n] * HEAD_DIM, HEAD_DIM)
            copy_rows(dst + n * HEAD_DIM, pl.ds(start, HEAD_DIM), HEAD_DIM)

    def copy_plain(dst, lo, hi):
        for start in range(lo, hi, 256):
            size = min(256, hi - start)
            copy_rows(dst + start - lo, slice(start, start + size), size)

    for part in range(3):
        copy_heads(part * D_FOX, part * D_FOX)
    copy_plain(N_QKV, su_lo, rqkv_lo)
    copy_heads(N_QKV + D_S5, gate_lo)
    copy_plain(N_QKV + D_S5 + D_FOX, gate_lo + D_FOX, D_IN_PROJ)
    copy_plain(N_QKV + D_S5 + D_MODEL, rqkv_lo, gate_lo)
    flog = [wt_ref[0, pl.ds(N_QKV + order_ref[n], 1), :] for n in range(FOX_HEADS)]
    flog.append(jnp.zeros((LANES - FOX_HEADS, D_MODEL), F32))
    win_ref[N_QKV + N_REST:, :] = jnp.concatenate(flog, axis=0).astype(BF16)


def _project_in(h, w_ref, oa_ref, su_ref, ob_ref, fl_ref):
    nt = (((1,), (1,)), ((), ()))
    qkv = lax.dot_general(h, w_ref[:N_QKV, :], nt, preferred_element_type=F32)
    oa_ref[:, :D_FOX] = (qkv[:, :D_FOX] * Q_SCALE).astype(oa_ref.dtype)
    oa_ref[:, D_FOX:] = qkv[:, D_FOX:].astype(oa_ref.dtype)
    rest = lax.dot_general(h, w_ref[N_QKV:N_QKV + N_REST, :], nt, preferred_element_type=F32)
    su_ref[...] = rest[:, :D_S5]
    ob_ref[...] = rest[:, D_S5:].astype(ob_ref.dtype)
    fl_ref[...] = lax.dot_general(h, w_ref[N_QKV + N_REST:, :], nt, preferred_element_type=F32)


def _project_out(yf_ref, ys_ref, yr_ref, gate_ref, x_ref, w_ref):
    g = gate_ref[...].astype(F32)
    g = g * jax.nn.sigmoid(g)
    acc = x_ref[...]
    off = 0
    for y_ref in (yf_ref, ys_ref, yr_ref):
        width = y_ref.shape[-1]
        y = (y_ref[...] * g[:, off:off + width]).astype(BF16)
        acc = acc + jnp.dot(y, w_ref[off:off + width, :], preferred_element_type=F32)
        off += width
    return acc


def _first_kernel(order_ref, x_ref, g_ref, wt_ref, oa_ref, su_ref, ob_ref, fl_ref, win_ref):
    @pl.when(pl.program_id(0) == 0)
    def _():
        _pack_in_proj(order_ref, wt_ref, win_ref)

    h = _rms_norm(x_ref[...], g_ref[0]).astype(BF16)
    _project_in(h, win_ref, oa_ref, su_ref, ob_ref, fl_ref)


def _mid_kernel(order_ref, yf_ref, ys_ref, yr_ref, gate_ref, x_ref, wo_ref, g_ref, wt_ref,
                xo_ref, oa_ref, su_ref, ob_ref, fl_ref, win_ref):
    @pl.when(pl.program_id(0) == 0)
    def _():
        _pack_in_proj(order_ref, wt_ref, win_ref)

    acc = _project_out(yf_ref, ys_ref, yr_ref, gate_ref, x_ref, wo_ref)
    xo_ref[...] = acc
    h = _rms_norm(acc, g_ref[0]).astype(BF16)
    _project_in(h, win_ref, oa_ref, su_ref, ob_ref, fl_ref)


def _last_kernel(order_ref, yf_ref, ys_ref, yr_ref, gate_ref, x_ref, wo_ref, g_ref, o_ref):
    acc = _project_out(yf_ref, ys_ref, yr_ref, gate_ref, x_ref, wo_ref)
    o_ref[...] = _rms_norm(acc, g_ref[0])


def _projections(kind, m, order, norm_layer, in_layer, operands):
    row = lambda i, order: (i, 0)
    fixed = lambda i, order: (0, 0)
    tile = pl.BlockSpec((TM_PROJ, D_MODEL), row)
    once = pl.Buffered(1)
    norm_spec = pl.BlockSpec((1, 1, D_MODEL), lambda i, order: (norm_layer, 0, 0))
    wt_spec = pl.BlockSpec((1, D_IN_PROJ, D_MODEL), lambda i, order: (in_layer, 0, 0),
                           pipeline_mode=once)
    wo_spec = pl.BlockSpec((D_MODEL, D_MODEL), fixed, pipeline_mode=once)
    out_in_specs = [pl.BlockSpec((TM_PROJ, D_FOX), row), pl.BlockSpec((TM_PROJ, D_S5), row),
                    pl.BlockSpec((TM_PROJ, D_RET), row), tile, tile]
    in_out_specs = [pl.BlockSpec((TM_PROJ, N_QKV), row), pl.BlockSpec((TM_PROJ, D_S5), row),
                    pl.BlockSpec((TM_PROJ, N_REST - D_S5), row),
                    pl.BlockSpec((TM_PROJ, LANES), row)]
    in_out_shapes = [jax.ShapeDtypeStruct((m, N_QKV), BF16), jax.ShapeDtypeStruct((m, D_S5), F32),
                     jax.ShapeDtypeStruct((m, N_REST - D_S5), BF16),
                     jax.ShapeDtypeStruct((m, LANES), F32)]
    x_shape = jax.ShapeDtypeStruct((m, D_MODEL), F32)
    packed_w = [pltpu.VMEM((N_ALL, D_MODEL), BF16)]
    if kind == "first":
        body, in_specs, scratch = _first_kernel, [tile, norm_spec, wt_spec], packed_w
        out_specs, out_shape = in_out_specs, in_out_shapes
    elif kind == "mid":
        body, scratch = _mid_kernel, packed_w
        in_specs = out_in_specs + [wo_spec, norm_spec, wt_spec]
        out_specs, out_shape = [tile] + in_out_specs, [x_shape] + in_out_shapes
    else:
        body, in_specs, scratch = _last_kernel, out_in_specs + [wo_spec, norm_spec], []
        out_specs, out_shape = tile, x_shape
    grid_spec = pltpu.PrefetchScalarGridSpec(
        num_scalar_prefetch=1, grid=(m // TM_PROJ,), in_specs=in_specs, out_specs=out_specs,
        scratch_shapes=scratch)
    return pl.pallas_call(
        body,
        grid_spec=grid_spec,
        out_shape=out_shape,
        compiler_params=_cparams(("arbitrary",)),
        name="proj_" + kind,
    )(order, *operands)


def _fox_kernel(q_ref, k_ref, v_ref, fl_ref, bf_ref, o_ref, kaug_ref, vaug_ref, c_ref, clast_ref,
                knorm_ref):
    t = T_FOX
    i = pl.program_id(1)
    seq = k_ref.shape[1]
    lane = lax.broadcasted_iota(jnp.int32, (t, LANES), 1)
    lane1 = lane[0:1, :]
    own = (lane < HEAD_DIM, lane >= HEAD_DIM)
    n_pairs = D_FOX // LANES
    heads = [(g, e) for g in range(n_pairs) for e in range(2)]
    cols = lambda g: slice(g * LANES, (g + 1) * LANES)
    x0 = lambda e: HEAD_DIM if e == 0 else 0
    shift = lambda g, e: (x0(e) - (2 * g + e)) % LANES

    half_id = lax.broadcasted_iota(jnp.int32, (LANES, LANES), 0) // HEAD_DIM
    same_half = (half_id == lax.broadcasted_iota(jnp.int32, (LANES, LANES), 1) // HEAD_DIM)
    head_sum = jnp.where(same_half, 1.0, 0.0).astype(BF16)

    def max_sq_norms(ref, rows):
        out = jnp.zeros((1, LANES), F32)
        for g in range(n_pairs):
            x = ref[0, rows, cols(g)]
            sums = jnp.dot(x * x, head_sum, preferred_element_type=F32)
            top = jnp.max(sums, axis=0, keepdims=True) * FOX_NORM_SLACK
            out = jnp.where(lane1 == 2 * g, top,
                            jnp.where(lane1 == 2 * g + 1, pltpu.roll(top, HEAD_DIM, 1), out))
        return out

    @pl.when(i == 0)
    def _():
        r = lax.broadcasted_iota(jnp.int32, (t, t), 0)
        s = lax.broadcasted_iota(jnp.int32, (t, t), 1)
        tri = jnp.where(s <= r, 1.0, 0.0).astype(BF16)

        def build(j, carried):
            carry, k_max = carried
            rows = pl.ds(pl.multiple_of(j * t, t), t)
            z = fl_ref[0, rows, :] + bf_ref[0]
            logf = jnp.minimum(z, 0.0) - jnp.log1p(jnp.exp(-jnp.abs(z)))
            top = logf.astype(BF16)
            rem = logf - top.astype(F32)
            middle = rem.astype(BF16)
            bottom = (rem - middle.astype(F32)).astype(BF16)
            c = carry + sum(jnp.dot(tri, piece, preferred_element_type=F32)
                            for piece in (bottom, middle, top))
            c_ref[rows, :] = c
            clast_ref[pl.ds(j, 1), :] = c[t - 1:t, :]
            k_max = jnp.maximum(k_max, max_sq_norms(k_ref, rows))
            knorm_ref[pl.ds(j, 1), :] = k_max
            b = -LOG2E * c
            hi = b.astype(BF16).astype(F32)
            mid = (b - hi).astype(BF16).astype(F32)
            lo = b - hi - mid
            pieces = jnp.where(lane < 8, hi, jnp.where(lane < 16, pltpu.roll(mid, 8, 1), jnp.where(
                lane < 24, pltpu.roll(lo, 16, 1), jnp.where(lane < 32, 1.0, 0.0))))
            for n, (g, e) in enumerate(heads):
                extra = pltpu.roll(pieces, shift(g, e), 1).astype(BF16)
                kaug_ref[n, rows, :] = jnp.where(own[e], k_ref[0, rows, cols(g)], extra)
                v = v_ref[0, rows, cols(g)]
                vaug_ref[n, rows, :] = jnp.where(own[e], v, jnp.ones_like(v))
            return c[t - 1:t, :], k_max

        zero_row = jnp.zeros((1, LANES), F32)
        carried = (zero_row, zero_row)
        for j in range(seq // t):
            carried = build(j, carried)

    cq0 = LOG2E * c_ref[pl.ds(pl.multiple_of(i * t, t), SUBLANES), :][0:1, :]
    qa = []
    for g, e in heads:
        ones = (lane1 == x0(e)) | (lane1 == x0(e) + 8) | (lane1 == x0(e) + 16)
        extra = jnp.where(ones, 1.0, jnp.where(
            lane1 == x0(e) + 24, pltpu.roll(cq0, (shift(g, e) + 24) % LANES, 1), 0.0))
        qa.append(jnp.where(own[e], q_ref[0, :, cols(g)], extra.astype(BF16)))

    def tile(j, carry, masked, ids):
        m, acc = carry
        rows = pl.ds(pl.multiple_of(j * t, t), t)
        new_m, new_acc = [], []
        for pos, n in enumerate(ids):
            s = lax.dot_general(qa[n], kaug_ref[n, rows, :], (((1,), (1,)), ((), ())),
                                preferred_element_type=F32)
            if masked:
                r = lax.broadcasted_iota(jnp.int32, (t, t), 0)
                cidx = lax.broadcasted_iota(jnp.int32, (t, t), 1)
                s = jnp.where(cidx <= r, s, NEG_BIG)
            m_new = jnp.maximum(m[pos], jnp.max(s, axis=-1, keepdims=True))
            alpha = jnp.exp2(m[pos] - m_new)
            p = jnp.exp2(s - m_new).astype(BF16)
            new_m.append(m_new)
            new_acc.append(acc[pos] * alpha + jnp.dot(p, vaug_ref[n, rows, :],
                                                      preferred_element_type=F32))
        return tuple(new_m), tuple(new_acc)

    n_tiles = seq // t
    tile_id = lax.broadcasted_iota(jnp.int32, (n_tiles, LANES), 0)
    cq0_used = cq0.astype(BF16).astype(F32)
    upper = (jnp.sqrt(max_sq_norms(q_ref, slice(None)) * knorm_ref[...])
             + (cq0_used - LOG2E * clast_ref[...]))

    def needed(j, m, ids):
        row_min = jnp.zeros((1, LANES), F32)
        for pos, n in enumerate(ids):
            row_min = jnp.where(lane1 == n, jnp.min(m[pos], axis=0, keepdims=True), row_min)
        upper_j = jnp.sum(jnp.where(tile_id == j, upper, 0.0), axis=0, keepdims=True)
        hit = (upper_j >= row_min - FOX_SKIP_LOG2) & (lane1 >= ids[0]) & (lane1 <= ids[-1])
        return (j >= 0) & (jnp.max(jnp.where(hit, 1.0, 0.0)) > 0.0)

    acc = []
    for first in range(0, len(heads), FOX_GROUP):
        ids = list(range(first, first + FOX_GROUP))
        init = ((jnp.full((t, 1), NEG_BIG, F32),) * FOX_GROUP,
                (jnp.zeros((t, LANES), F32),) * FOX_GROUP)
        m, group_acc = tile(i, init, True, ids)

        def walk(state, ids=ids):
            j, _, m, group_acc = state
            m, group_acc = tile(j, (m, group_acc), False, ids)
            return j - 1, needed(j - 1, m, ids), m, group_acc

        state = lax.while_loop(lambda state: state[1], walk,
                               (i - 1, needed(i - 1, m, ids), m, group_acc))
        acc.extend(state[3])
    outs = [a / pltpu.roll(a, HEAD_DIM, 1) for a in acc]
    for g in range(n_pairs):
        o_ref[0, :, cols(g)] = jnp.where(own[0], outs[2 * g], outs[2 * g + 1]).astype(o_ref.dtype)


def _fox_attention(qkv, flog, bf_pad, layer):
    batch, seq, _ = qkv.shape
    n_heads = D_FOX // HEAD_DIM
    once = pl.Buffered(1)
    return pl.pallas_call(
        _fox_kernel,
        grid=(batch, seq // T_FOX),
        in_specs=[
            pl.BlockSpec((1, T_FOX, D_FOX), lambda b, i: (b, i, 0)),
            pl.BlockSpec((1, seq, D_FOX), lambda b, i: (b, 0, 1), pipeline_mode=once),
            pl.BlockSpec((1, seq, D_FOX), lambda b, i: (b, 0, 2), pipeline_mode=once),
            pl.BlockSpec((1, seq, LANES), lambda b, i: (b, 0, 0), pipeline_mode=once),
            pl.BlockSpec((1, 1, LANES), lambda b, i: (layer, 0, 0)),
        ],
        out_specs=pl.BlockSpec((1, T_FOX, D_FOX), lambda b, i: (b, i, 0)),
        out_shape=jax.ShapeDtypeStruct((batch, seq, D_FOX), BF16),
        scratch_shapes=[pltpu.VMEM((n_heads, seq, LANES), BF16),
                        pltpu.VMEM((n_heads, seq, LANES), BF16),
                        pltpu.VMEM((seq, LANES), F32),
                        pltpu.VMEM((seq // T_FOX, LANES), F32),
                        pltpu.VMEM((seq // T_FOX, LANES), F32)],
        compiler_params=_cparams(("parallel", "arbitrary")),
        name="fox_attention",
    )(qkv, qkv, qkv, flog, bf_pad)


def _ret_kernel(q_ref, k_ref, v_ref, cos_ref, sin_ref, lg_ref, gn_ref, o_ref,
                state_ref, dmat_ref, wq_ref, wk_ref):
    t = T_RET
    n_pairs = D_RET // LANES
    scale = 1.0 / math.sqrt(HEAD_DIM)
    lane = lax.broadcasted_iota(jnp.int32, (t, LANES), 1)
    first = lane < HEAD_DIM
    low_half = (lane % HEAD_DIM) < (HEAD_DIM // 2)
    cols = lambda p: slice(p * LANES, (p + 1) * LANES)

    @pl.when((pl.program_id(0) == 0) & (pl.program_id(1) == 0))
    def _():
        pos = lax.broadcasted_iota(jnp.int32, (t, 1), 0).astype(F32)
        r = lax.broadcasted_iota(jnp.int32, (t, t), 0)
        s = lax.broadcasted_iota(jnp.int32, (t, t), 1)
        dist = jnp.abs(r - s).astype(F32)
        visible = (s // CHUNK) <= (r // CHUNK)
        for h in range(RET_HEADS):
            lg = lg_ref[:, h * HEAD_DIM:h * HEAD_DIM + 1]
            dmat_ref[h] = jnp.where(visible, jnp.exp(lg * dist) * scale, 0.0)
        wq_ref[...] = jnp.exp(lg_ref[...] * (pos + 1.0)) * scale
        wk_ref[...] = jnp.exp(lg_ref[...] * (t - 1.0 - pos))

    @pl.when(pl.program_id(1) == 0)
    def _():
        state_ref[...] = jnp.zeros_like(state_ref)

    rb = lax.broadcasted_iota(jnp.int32, (LANES, LANES), 0)
    cb = lax.broadcasted_iota(jnp.int32, (LANES, LANES), 1)
    same_head = (rb // HEAD_DIM) == (cb // HEAD_DIM)

    def head_mean(x):
        s0 = jnp.sum(jnp.where(first, x, 0.0), axis=-1, keepdims=True)
        s1 = jnp.sum(jnp.where(first, 0.0, x), axis=-1, keepdims=True)
        return jnp.where(first, s0, s1) * (1.0 / HEAD_DIM)

    for sub in range(RET_BLK // t):
        rows = slice(sub * t, (sub + 1) * t)
        cos = cos_ref[rows, :]
        sin = sin_ref[rows, :]

        def rotary(x):
            swapped = jnp.where(low_half, pltpu.roll(x, LANES - HEAD_DIM // 2, 1),
                                pltpu.roll(x, HEAD_DIM // 2, 1))
            return x * cos + swapped * sin

        for p in range(n_pairs):
            q = rotary(q_ref[0, rows, cols(p)].astype(F32))
            k = rotary(k_ref[0, rows, cols(p)].astype(F32))
            vb = v_ref[0, rows, cols(p)]
            kb = k.astype(BF16)
            zero = jnp.zeros_like(q)
            inner = []
            for e in range(2):
                qe = jnp.where(first, q, zero) if e == 0 else jnp.where(first, zero, q)
                sc = lax.dot_general(qe.astype(BF16), kb, (((1,), (1,)), ((), ())),
                                     preferred_element_type=F32) * dmat_ref[2 * p + e]
                inner.append(jnp.dot(sc.astype(BF16), vb, preferred_element_type=F32))
            state = state_ref[p]
            o = jnp.where(first, inner[0], inner[1]) + jnp.dot(
                (q * wq_ref[:, cols(p)]).astype(BF16), state.astype(BF16),
                preferred_element_type=F32)
            upd = lax.dot_general((k * wk_ref[:, cols(p)]).astype(BF16), vb,
                                  (((0,), (0,)), ((), ())), preferred_element_type=F32)
            decay = jnp.exp(lg_ref[:, cols(p)] * float(t))
            state_ref[p] = jnp.where(same_head, state * decay + upd, 0.0)
            d = o - head_mean(o)
            var = head_mean(d * d)
            o_ref[0, rows, cols(p)] = (d * lax.rsqrt(var + EPS) * gn_ref[0, :, cols(p)]).astype(
                o_ref.dtype)


def _retention(rest, cos_t, sin_t, lg, gn_w, layer):
    batch, seq, _ = rest.shape
    n_pairs = D_RET // LANES
    tok = lambda blk: pl.BlockSpec((1, RET_BLK, D_RET), lambda b, i, blk=blk: (b, i, blk))
    fixed = pl.BlockSpec((1, D_RET), lambda b, i: (0, 0))
    table = pl.BlockSpec((RET_BLK, LANES), lambda b, i: (i, 0))
    return pl.pallas_call(
        _ret_kernel,
        grid=(batch, seq // RET_BLK),
        in_specs=[tok(4), tok(5), tok(6), table, table, fixed,
                  pl.BlockSpec((1, 1, D_RET), lambda b, i: (layer, 0, 0))],
        out_specs=pl.BlockSpec((1, RET_BLK, D_RET), lambda b, i: (b, i, 0)),
        out_shape=jax.ShapeDtypeStruct((batch, seq, D_RET), BF16),
        scratch_shapes=[
            pltpu.VMEM((n_pairs, LANES, LANES), F32),
            pltpu.VMEM((RET_HEADS, T_RET, T_RET), F32),
            pltpu.VMEM((T_RET, D_RET), F32),
            pltpu.VMEM((T_RET, D_RET), F32),
        ],
        compiler_params=_cparams(("arbitrary", "arbitrary")),
        name="retention",
    )(rest, rest, rest, cos_t, sin_t, lg, gn_w)


def _s5_kernel(u_ref, are_ref, aim_ref, ldt_ref, br_ref, bi_ref, cr_ref, ci_ref, d_ref, wg_ref,
               o_ref, wb_ref, wc_ref, lam_ref, lhs_ref, x_ref, z_ref, state_ref):
    nb = u_ref.shape[0]
    t5 = S5_T
    win = D_S5 // SUBLANES

    @pl.when(pl.program_id(0) == 0)
    def _():
        dt = jnp.exp(ldt_ref[0])
        ar = are_ref[0]
        ai = aim_ref[0]
        mag = jnp.exp(ar * dt)
        lr = mag * jnp.cos(ai * dt)
        li = mag * jnp.sin(ai * dt)
        den = ar * ar + ai * ai
        fr = ((lr - 1.0) * ar + li * ai) / den
        fi = (li * ar - (lr - 1.0) * ai) / den
        lam_ref[0] = lr
        lam_ref[1] = li
        for j in range(SUBLANES):
            rows = slice(win * j, win * (j + 1))
            f_r = fr[j:j + 1, :]
            f_i = fi[j:j + 1, :]
            b_r = br_ref[0, rows, :]
            b_i = bi_ref[0, rows, :]
            wb_ref[rows, :LANES] = (f_r * b_r - f_i * b_i).astype(BF16)
            wb_ref[rows, LANES:] = (f_r * b_i + f_i * b_r).astype(BF16)
        wc_ref[:LANES, :] = cr_ref[0].astype(BF16)
        wc_ref[LANES:, :] = (-ci_ref[0]).astype(BF16)
        state_ref[...] = jnp.zeros_like(state_ref)

    sub = lax.broadcasted_iota(jnp.int32, (2 * SUBLANES, D_S5), 0)
    keep = sub % SUBLANES == lax.broadcasted_iota(jnp.int32, (2 * SUBLANES, D_S5), 1) // win
    first = sub < SUBLANES

    for tt in range(t5 // SUBLANES):
        for b in range(nb):
            tile = u_ref[b, tt * SUBLANES:(tt + 1) * SUBLANES, :]
            for s in range(0, SUBLANES, 2):
                two = jnp.where(first,
                                jnp.broadcast_to(tile[s:s + 1, :], (2 * SUBLANES, D_S5)),
                                jnp.broadcast_to(tile[s + 1:s + 2, :], (2 * SUBLANES, D_S5)))
                row0 = (tt * SUBLANES + s) * SUBLANES
                lhs_ref[b, row0:row0 + 2 * SUBLANES, :] = jnp.where(keep, two, 0.0).astype(BF16)
    for b in range(nb):
        x_ref[b] = jnp.dot(lhs_ref[b], wb_ref[...], preferred_element_type=F32)

    lr = lam_ref[0]
    li = lam_ref[1]

    def scan(t, carry):
        rows = slice(t * SUBLANES, (t + 1) * SUBLANES)
        new = []
        for b in range(nb):
            xr, xi = carry[b]
            nr = lr * xr - li * xi + x_ref[b, rows, :LANES]
            ni = lr * xi + li * xr + x_ref[b, rows, LANES:]
            x_ref[b, rows, :LANES] = nr
            x_ref[b, rows, LANES:] = ni
            new.append((nr, ni))
        return tuple(new)

    final = tuple((state_ref[b, :, :LANES], state_ref[b, :, LANES:]) for b in range(nb))
    for t in range(t5):
        final = scan(t, final)
    for b in range(nb):
        state_ref[b, :, :LANES] = final[b][0]
        state_ref[b, :, LANES:] = final[b][1]

    window = lax.broadcasted_iota(jnp.int32, (t5, LANES), 1) // win
    for b in range(nb):
        z = jnp.dot(x_ref[b].astype(BF16), wc_ref[...], preferred_element_type=F32)
        z_ref[b, 0] = z[:, :LANES]
        z_ref[b, 1] = z[:, LANES:]
    for b in range(nb):
        halves = []
        for hh in range(2):
            per = SUBLANES // 2
            y = z_ref[b, hh, pl.ds(per * hh + per - 1, t5, stride=SUBLANES), :]
            for jj in range(per - 2, -1, -1):
                y = jnp.where(window == jj,
                              z_ref[b, hh, pl.ds(per * hh + jj, t5, stride=SUBLANES), :], y)
            halves.append(y)
        u = u_ref[b]
        y = jax.nn.gelu(jnp.concatenate(halves, axis=1) + d_ref[0] * u)
        gate = jnp.dot(y.astype(BF16), wg_ref[0], preferred_element_type=F32)
        o_ref[b] = (y * jax.nn.sigmoid(gate)).astype(o_ref.dtype)


def _s5(su, prm, layer):
    batch, seq, _ = su.shape
    fixed = lambda i: (layer, 0, 0)
    lam = pl.BlockSpec((1, SUBLANES, LANES), fixed)
    rows8 = SUBLANES * S5_T
    return pl.pallas_call(
        _s5_kernel,
        grid=(seq // S5_T,),
        in_specs=[
            pl.BlockSpec((batch, S5_T, D_S5), lambda i: (0, i, 0)),
            lam, lam, lam,
            pl.BlockSpec((1, D_S5, LANES), fixed), pl.BlockSpec((1, D_S5, LANES), fixed),
            pl.BlockSpec((1, LANES, D_S5), fixed), pl.BlockSpec((1, LANES, D_S5), fixed),
            pl.BlockSpec((1, 1, D_S5), fixed),
            pl.BlockSpec((1, D_S5, D_S5), fixed),
        ],
        out_specs=pl.BlockSpec((batch, S5_T, D_S5), lambda i: (0, i, 0)),
        out_shape=jax.ShapeDtypeStruct((batch, seq, D_S5), BF16),
        scratch_shapes=[
            pltpu.VMEM((D_S5, 2 * LANES), BF16),
            pltpu.VMEM((2 * LANES, D_S5), BF16),
            pltpu.VMEM((2, SUBLANES, LANES), F32),
            pltpu.VMEM((batch, rows8, D_S5), BF16),
            pltpu.VMEM((batch, rows8, 2 * LANES), F32),
            pltpu.VMEM((batch, 2, rows8, LANES), F32),
            pltpu.VMEM((batch, SUBLANES, 2 * LANES), F32),
        ],
        compiler_params=_cparams(("arbitrary",)),
        name="s5",
    )(su, prm["a_re"], prm["a_im"], prm["log_dt"], prm["b_re"], prm["b_im"],
      prm["c_re"], prm["c_im"], prm["d"], prm["w_glu"])


def _rotary_tables(seq):
    half = HEAD_DIM // 2
    lane = jnp.arange(LANES)
    freqs = ROPE_BASE ** (-(lane % half).astype(F32) / half)
    ang = jnp.arange(seq, dtype=F32)[:, None] * freqs[None, :]
    sign = jnp.where(lane % HEAD_DIM < half, -1.0, 1.0).astype(F32)
    return jnp.cos(ang), jnp.sin(ang) * sign[None, :]


def _s5_params(a_re, a_im, b_re, b_im, c_re, c_im, d, log_dt, w_glu):
    depth = a_re.shape[0]
    pair = jnp.eye(2, dtype=F32)
    half = S5_GROUPS // 2
    pack_b = lambda b: jnp.einsum(
        "ljnph,kn->ljkhnp", b.reshape(depth, half, 2, S5_STATE, S5_GROUP_CH), pair
    ).reshape(depth, D_S5, LANES)
    pack_c = lambda c: jnp.einsum(
        "ljnhp,kn->lnpjkh", c.reshape(depth, half, 2, S5_GROUP_CH, S5_STATE), pair
    ).reshape(depth, LANES, D_S5)
    tile = lambda v: v.reshape(depth, SUBLANES, LANES)
    return {
        "a_re": tile(a_re), "a_im": tile(a_im),
        "log_dt": tile(jnp.repeat(log_dt, S5_STATE, axis=-1)),
        "b_re": pack_b(b_re), "b_im": pack_b(b_im), "c_re": pack_c(c_re), "c_im": pack_c(c_im),
        "d": d.reshape(depth, 1, D_S5), "w_glu": w_glu.astype(BF16),
    }


def kernel(x, norm_w, w_in, fox_b_f, s5_a_re, s5_a_im, s5_b_re, s5_b_im, s5_c_re, s5_c_im,
           s5_d, s5_log_dt, s5_w_glu, ret_gn_w, w_out, final_norm_w):
    batch, seq, _ = x.shape
    depth = w_in.shape[0]
    m = batch * seq
    cos_t, sin_t = _rotary_tables(seq)
    log_gamma = jnp.log1p(-(2.0 ** (-5.0 - jnp.arange(RET_HEADS, dtype=F32))))
    lg = jnp.repeat(log_gamma, HEAD_DIM).reshape(1, D_RET)
    norm_g = jnp.concatenate([norm_w, final_norm_w[None]], axis=0).reshape(depth + 1, 1, D_MODEL)

    order = jnp.argsort(fox_b_f, axis=1).astype(jnp.int32)
    bf_sorted = jnp.take_along_axis(fox_b_f, order, axis=1)
    bf_pad = jnp.pad(bf_sorted, ((0, 0), (0, LANES - FOX_HEADS))).reshape(depth, 1, LANES)
    gn_w = ret_gn_w.reshape(depth, 1, D_RET)
    s5_prm = _s5_params(s5_a_re, s5_a_im, s5_b_re, s5_b_im, s5_c_re, s5_c_im, s5_d, s5_log_dt,
                        s5_w_glu)

    w_in_t = jnp.swapaxes(w_in, 1, 2)
    blocks = D_MODEL // HEAD_DIM
    row_blocks = jnp.concatenate(
        [order, jnp.broadcast_to(jnp.arange(FOX_HEADS, blocks, dtype=jnp.int32),
                                 (depth, blocks - FOX_HEADS))], axis=1)
    w_out_p = jnp.take_along_axis(w_out.reshape(depth, blocks, HEAD_DIM, D_MODEL),
                                  row_blocks[:, :, None, None], axis=1)
    w_out_p = w_out_p.reshape(depth, D_MODEL, D_MODEL).astype(BF16)
    x2 = x.reshape(m, D_MODEL)
    qkv, su, rest, flog = _projections("first", m, order[0], 0, 0, (x2, norm_g, w_in_t))
    for l in range(depth):
        rest3 = rest.reshape(batch, seq, N_REST - D_S5)
        y_fox = _fox_attention(qkv.reshape(batch, seq, N_QKV), flog.reshape(batch, seq, LANES),
                               bf_pad, l)
        y_s5 = _s5(su.reshape(batch, seq, D_S5), s5_prm, l)
        y_ret = _retention(rest3, cos_t, sin_t, lg, gn_w, l)
        ys = (y_fox.reshape(m, D_FOX), y_s5.reshape(m, D_S5), y_ret.reshape(m, D_RET), rest, x2)
        if l < depth - 1:
            x2, qkv, su, rest, flog = _projections(
                "mid", m, order[l + 1], l + 1, l + 1, ys + (w_out_p[l], norm_g, w_in_t))
    out = _projections("last", m, order[depth - 1], depth, 0, ys + (w_out_p[depth - 1], norm_g))
    return out.reshape(batch, seq, D_MODEL)
```

```python
import math

import jax
import jax.numpy as jnp
from jax import lax
from jax.experimental import pallas as pl
from jax.experimental.pallas import tpu as pltpu

F32 = jnp.float32
BF16 = jnp.bfloat16

D_MODEL = 1024
HEAD_DIM = 64
CHUNK = 64
D_FOX = 512
FOX_HEADS = 8
D_S5 = 256
S5_GROUPS = 16
S5_GROUP_CH = 16
S5_STATE = 64
D_RET = 256
RET_HEADS = 4
ROPE_BASE = 10000.0
EPS = 1e-6

LANES = 128
SUBLANES = 8
N_QKV = 3 * D_FOX
N_REST = D_S5 + D_MODEL + 3 * D_RET
N_ALL = N_QKV + N_REST + LANES
D_IN_PROJ = N_QKV + FOX_HEADS + N_REST
VMEM_LIMIT = 56 * 1024 * 1024

TM_PROJ = 512
TM_LAST = 1024
T_FOX = 512
FOX_GROUP = 2
T_RET = 256
RET_BLK = 512
S5_T = 256
NEG_BIG = -1e30
FOX_SKIP_LOG2 = 40.0
FOX_NORM_SLACK = 1.02
LOG2E = math.log2(math.e)
Q_SCALE = LOG2E / math.sqrt(HEAD_DIM)


def _cparams(sem):
    return pltpu.CompilerParams(dimension_semantics=sem, vmem_limit_bytes=VMEM_LIMIT)


def _rms_norm(x, g):
    ms = jnp.mean(x * x, axis=-1, keepdims=True)
    return x * lax.rsqrt(ms + EPS) * g


def _pack_in_proj(order_ref, wt_ref, win_ref):
    su_lo = N_QKV + FOX_HEADS
    rqkv_lo = su_lo + D_S5
    gate_lo = rqkv_lo + 3 * D_RET

    def copy_rows(dst, src, size):
        win_ref[dst:dst + size, :] = wt_ref[0, src, :].astype(BF16)

    def copy_heads(dst, src):
        for n in range(FOX_HEADS):
            start = pl.multiple_of(src + order_ref[n] * HEAD_DIM, HEAD_DIM)
            copy_rows(dst + n * HEAD_DIM, pl.ds(start, HEAD_DIM), HEAD_DIM)

    def copy_plain(dst, lo, hi):
        for start in range(lo, hi, 256):
            size = min(256, hi - start)
            copy_rows(dst + start - lo, slice(start, start + size), size)

    for part in range(3):
        copy_heads(part * D_FOX, part * D_FOX)
    copy_plain(N_QKV, su_lo, rqkv_lo)
    copy_heads(N_QKV + D_S5, gate_lo)
    copy_plain(N_QKV + D_S5 + D_FOX, gate_lo + D_FOX, D_IN_PROJ)
    copy_plain(N_QKV + D_S5 + D_MODEL, rqkv_lo, gate_lo)
    flog = [wt_ref[0, pl.ds(N_QKV + order_ref[n], 1), :] for n in range(FOX_HEADS)]
    flog.append(jnp.zeros((LANES - FOX_HEADS, D_MODEL), F32))
    win_ref[N_QKV + N_REST:, :] = jnp.concatenate(flog, axis=0).astype(BF16)


def _project_in(h, w_ref, oa_ref, su_ref, ob_ref, fl_ref):
    nt = (((1,), (1,)), ((), ()))
    qkv = lax.dot_general(h, w_ref[:N_QKV, :], nt, preferred_element_type=F32)
    oa_ref[:, :D_FOX] = (qkv[:, :D_FOX] * Q_SCALE).astype(oa_ref.dtype)
    oa_ref[:, D_FOX:] = qkv[:, D_FOX:].astype(oa_ref.dtype)
    rest = lax.dot_general(h, w_ref[N_QKV:N_QKV + N_REST, :], nt, preferred_element_type=F32)
    su_ref[...] = rest[:, :D_S5]
    ob_ref[...] = rest[:, D_S5:].astype(ob_ref.dtype)
    fl_ref[...] = lax.dot_general(h, w_ref[N_QKV + N_REST:, :], nt, preferred_element_type=F32)


def _project_out(yf_ref, ys_ref, yr_ref, gate_ref, x_ref, w_ref):
    g = gate_ref[...].astype(F32)
    g = g * jax.nn.sigmoid(g)
    acc = x_ref[...]
    off = 0
    for y_ref in (yf_ref, ys_ref, yr_ref):
        width = y_ref.shape[-1]
        y = (y_ref[...] * g[:, off:off + width]).astype(BF16)
        acc = acc + jnp.dot(y, w_ref[off:off + width, :], preferred_element_type=F32)
        off += width
    return acc


def _first_kernel(order_ref, x_ref, g_ref, wt_ref, oa_ref, su_ref, ob_ref, fl_ref, win_ref):
    @pl.when(pl.program_id(0) == 0)
    def _():
        _pack_in_proj(order_ref, wt_ref, win_ref)

    h = _rms_norm(x_ref[...], g_ref[0]).astype(BF16)
    _project_in(h, win_ref, oa_ref, su_ref, ob_ref, fl_ref)


def _mid_kernel(order_ref, yf_ref, ys_ref, yr_ref, gate_ref, x_ref, wo_ref, g_ref, wt_ref,
                xo_ref, oa_ref, su_ref, ob_ref, fl_ref, win_ref):
    @pl.when(pl.program_id(0) == 0)
    def _():
        _pack_in_proj(order_ref, wt_ref, win_ref)

    acc = _project_out(yf_ref, ys_ref, yr_ref, gate_ref, x_ref, wo_ref)
    xo_ref[...] = acc
    h = _rms_norm(acc, g_ref[0]).astype(BF16)
    _project_in(h, win_ref, oa_ref, su_ref, ob_ref, fl_ref)


def _last_kernel(order_ref, yf_ref, ys_ref, yr_ref, gate_ref, x_ref, wo_ref, g_ref, o_ref):
    acc = _project_out(yf_ref, ys_ref, yr_ref, gate_ref, x_ref, wo_ref)
    o_ref[...] = _rms_norm(acc, g_ref[0])


def _projections(kind, m, order, norm_layer, in_layer, operands):
    tm = TM_LAST if kind == "last" else TM_PROJ
    row = lambda i, order: (i, 0)
    fixed = lambda i, order: (0, 0)
    tile = pl.BlockSpec((tm, D_MODEL), row)
    once = pl.Buffered(1)
    norm_spec = pl.BlockSpec((1, 1, D_MODEL), lambda i, order: (norm_layer, 0, 0))
    wt_spec = pl.BlockSpec((1, D_IN_PROJ, D_MODEL), lambda i, order: (in_layer, 0, 0),
                           pipeline_mode=once)
    wo_spec = pl.BlockSpec((D_MODEL, D_MODEL), fixed, pipeline_mode=once)
    out_in_specs = [pl.BlockSpec((tm, D_FOX), row), pl.BlockSpec((tm, D_S5), row),
                    pl.BlockSpec((tm, D_RET), row), tile, tile]
    in_out_specs = [pl.BlockSpec((tm, N_QKV), row), pl.BlockSpec((tm, D_S5), row),
                    pl.BlockSpec((tm, N_REST - D_S5), row),
                    pl.BlockSpec((tm, LANES), row)]
    in_out_shapes = [jax.ShapeDtypeStruct((m, N_QKV), BF16), jax.ShapeDtypeStruct((m, D_S5), F32),
                     jax.ShapeDtypeStruct((m, N_REST - D_S5), BF16),
                     jax.ShapeDtypeStruct((m, LANES), F32)]
    x_shape = jax.ShapeDtypeStruct((m, D_MODEL), F32)
    packed_w = [pltpu.VMEM((N_ALL, D_MODEL), BF16)]
    if kind == "first":
        body, in_specs, scratch = _first_kernel, [tile, norm_spec, wt_spec], packed_w
        out_specs, out_shape = in_out_specs, in_out_shapes
    elif kind == "mid":
        body, scratch = _mid_kernel, packed_w
        in_specs = out_in_specs + [wo_spec, norm_spec, wt_spec]
        out_specs, out_shape = [tile] + in_out_specs, [x_shape] + in_out_shapes
    else:
        body, in_specs, scratch = _last_kernel, out_in_specs + [wo_spec, norm_spec], []
        out_specs, out_shape = tile, x_shape
    grid_spec = pltpu.PrefetchScalarGridSpec(
        num_scalar_prefetch=1, grid=(m // tm,), in_specs=in_specs, out_specs=out_specs,
        scratch_shapes=scratch)
    return pl.pallas_call(
        body,
        grid_spec=grid_spec,
        out_shape=out_shape,
        compiler_params=_cparams(("arbitrary",)),
        name="proj_" + kind,
    )(order, *operands)


def _fox_kernel(q_ref, k_ref, v_ref, fl_ref, bf_ref, o_ref, kaug_ref, vaug_ref, c_ref, clast_ref,
                knorm_ref):
    t = T_FOX
    i = pl.program_id(1)
    seq = k_ref.shape[1]
    lane = lax.broadcasted_iota(jnp.int32, (t, LANES), 1)
    lane1 = lane[0:1, :]
    own = (lane < HEAD_DIM, lane >= HEAD_DIM)
    n_pairs = D_FOX // LANES
    heads = [(g, e) for g in range(n_pairs) for e in range(2)]
    cols = lambda g: slice(g * LANES, (g + 1) * LANES)
    x0 = lambda e: HEAD_DIM if e == 0 else 0
    shift = lambda g, e: (x0(e) - (2 * g + e)) % LANES

    half_id = lax.broadcasted_iota(jnp.int32, (LANES, LANES), 0) // HEAD_DIM
    same_half = (half_id == lax.broadcasted_iota(jnp.int32, (LANES, LANES), 1) // HEAD_DIM)
    head_sum = jnp.where(same_half, 1.0, 0.0).astype(BF16)

    def max_sq_norms(ref, rows):
        out = jnp.zeros((1, LANES), F32)
        for g in range(n_pairs):
            x = ref[0, rows, cols(g)]
            sums = jnp.dot(x * x, head_sum, preferred_element_type=F32)
            top = jnp.max(sums, axis=0, keepdims=True) * FOX_NORM_SLACK
            out = jnp.where(lane1 == 2 * g, top,
                            jnp.where(lane1 == 2 * g + 1, pltpu.roll(top, HEAD_DIM, 1), out))
        return out

    @pl.when(i == 0)
    def _():
        r = lax.broadcasted_iota(jnp.int32, (t, t), 0)
        s = lax.broadcasted_iota(jnp.int32, (t, t), 1)
        tri = jnp.where(s <= r, 1.0, 0.0).astype(BF16)

        def build(j, carried):
            carry, k_max = carried
            rows = pl.ds(pl.multiple_of(j * t, t), t)
            z = fl_ref[0, rows, :] + bf_ref[0]
            logf = jnp.minimum(z, 0.0) - jnp.log1p(jnp.exp(-jnp.abs(z)))
            top = logf.astype(BF16)
            rem = logf - top.astype(F32)
            middle = rem.astype(BF16)
            bottom = (rem - middle.astype(F32)).astype(BF16)
            c = carry + sum(jnp.dot(tri, piece, preferred_element_type=F32)
                            for piece in (bottom, middle, top))
            c_ref[rows, :] = c
            clast_ref[pl.ds(j, 1), :] = c[t - 1:t, :]
            k_max = jnp.maximum(k_max, max_sq_norms(k_ref, rows))
            knorm_ref[pl.ds(j, 1), :] = k_max
            b = -LOG2E * c
            hi = b.astype(BF16).astype(F32)
            mid = (b - hi).astype(BF16).astype(F32)
            lo = b - hi - mid
            pieces = jnp.where(lane < 8, hi, jnp.where(lane < 16, pltpu.roll(mid, 8, 1), jnp.where(
                lane < 24, pltpu.roll(lo, 16, 1), jnp.where(lane < 32, 1.0, 0.0))))
            for n, (g, e) in enumerate(heads):
                extra = pltpu.roll(pieces, shift(g, e), 1).astype(BF16)
                kaug_ref[n, rows, :] = jnp.where(own[e], k_ref[0, rows, cols(g)], extra)
                v = v_ref[0, rows, cols(g)]
                vaug_ref[n, rows, :] = jnp.where(own[e], v, jnp.ones_like(v))
            return c[t - 1:t, :], k_max

        zero_row = jnp.zeros((1, LANES), F32)
        carried = (zero_row, zero_row)
        for j in range(seq // t):
            carried = build(j, carried)

    cq0 = LOG2E * c_ref[pl.ds(pl.multiple_of(i * t, t), SUBLANES), :][0:1, :]
    qa = []
    for g, e in heads:
        ones = (lane1 == x0(e)) | (lane1 == x0(e) + 8) | (lane1 == x0(e) + 16)
        extra = jnp.where(ones, 1.0, jnp.where(
            lane1 == x0(e) + 24, pltpu.roll(cq0, (shift(g, e) + 24) % LANES, 1), 0.0))
        qa.append(jnp.where(own[e], q_ref[0, :, cols(g)], extra.astype(BF16)))

    def tile(j, carry, masked, ids):
        m, acc = carry
        rows = pl.ds(pl.multiple_of(j * t, t), t)
        new_m, new_acc = [], []
        for pos, n in enumerate(ids):
            s = lax.dot_general(qa[n], kaug_ref[n, rows, :], (((1,), (1,)), ((), ())),
                                preferred_element_type=F32)
            if masked:
                r = lax.broadcasted_iota(jnp.int32, (t, t), 0)
                cidx = lax.broadcasted_iota(jnp.int32, (t, t), 1)
                s = jnp.where(cidx <= r, s, NEG_BIG)
            m_new = jnp.maximum(m[pos], jnp.max(s, axis=-1, keepdims=True))
            alpha = jnp.exp2(m[pos] - m_new)
            p = jnp.exp2(s - m_new).astype(BF16)
            new_m.append(m_new)
            new_acc.append(acc[pos] * alpha + jnp.dot(p, vaug_ref[n, rows, :],
                                                      preferred_element_type=F32))
        return tuple(new_m), tuple(new_acc)

    n_tiles = seq // t
    tile_id = lax.broadcasted_iota(jnp.int32, (n_tiles, LANES), 0)
    cq0_used = cq0.astype(BF16).astype(F32)
    upper = (jnp.sqrt(max_sq_norms(q_ref, slice(None)) * knorm_ref[...])
             + (cq0_used - LOG2E * clast_ref[...]))

    def needed(j, m, ids):
        row_min = jnp.zeros((1, LANES), F32)
        for pos, n in enumerate(ids):
            row_min = jnp.where(lane1 == n, jnp.min(m[pos], axis=0, keepdims=True), row_min)
        upper_j = jnp.sum(jnp.where(tile_id == j, upper, 0.0), axis=0, keepdims=True)
        hit = (upper_j >= row_min - FOX_SKIP_LOG2) & (lane1 >= ids[0]) & (lane1 <= ids[-1])
        return (j >= 0) & (jnp.max(jnp.where(hit, 1.0, 0.0)) > 0.0)

    acc = []
    for first in range(0, len(heads), FOX_GROUP):
        ids = list(range(first, first + FOX_GROUP))
        init = ((jnp.full((t, 1), NEG_BIG, F32),) * FOX_GROUP,
                (jnp.zeros((t, LANES), F32),) * FOX_GROUP)
        m, group_acc = tile(i, init, True, ids)

        def walk(state, ids=ids):
            j, _, m, group_acc = state
            m, group_acc = tile(j, (m, group_acc), False, ids)
            return j - 1, needed(j - 1, m, ids), m, group_acc

        state = lax.while_loop(lambda state: state[1], walk,
                               (i - 1, needed(i - 1, m, ids), m, group_acc))
        acc.extend(state[3])
    outs = [a / pltpu.roll(a, HEAD_DIM, 1) for a in acc]
    for g in range(n_pairs):
        o_ref[0, :, cols(g)] = jnp.where(own[0], outs[2 * g], outs[2 * g + 1]).astype(o_ref.dtype)


def _fox_attention(qkv, flog, bf_pad, layer):
    batch, seq, _ = qkv.shape
    n_heads = D_FOX // HEAD_DIM
    once = pl.Buffered(1)
    return pl.pallas_call(
        _fox_kernel,
        grid=(batch, seq // T_FOX),
        in_specs=[
            pl.BlockSpec((1, T_FOX, D_FOX), lambda b, i: (b, i, 0)),
            pl.BlockSpec((1, seq, D_FOX), lambda b, i: (b, 0, 1), pipeline_mode=once),
            pl.BlockSpec((1, seq, D_FOX), lambda b, i: (b, 0, 2), pipeline_mode=once),
            pl.BlockSpec((1, seq, LANES), lambda b, i: (b, 0, 0), pipeline_mode=once),
            pl.BlockSpec((1, 1, LANES), lambda b, i: (layer, 0, 0)),
        ],
        out_specs=pl.BlockSpec((1, T_FOX, D_FOX), lambda b, i: (b, i, 0)),
        out_shape=jax.ShapeDtypeStruct((batch, seq, D_FOX), BF16),
        scratch_shapes=[pltpu.VMEM((n_heads, seq, LANES), BF16),
                        pltpu.VMEM((n_heads, seq, LANES), BF16),
                        pltpu.VMEM((seq, LANES), F32),
                        pltpu.VMEM((seq // T_FOX, LANES), F32),
                        pltpu.VMEM((seq // T_FOX, LANES), F32)],
        compiler_params=_cparams(("parallel", "arbitrary")),
        name="fox_attention",
    )(qkv, qkv, qkv, flog, bf_pad)


def _ret_kernel(q_ref, k_ref, v_ref, cos_ref, sin_ref, lg_ref, gn_ref, o_ref,
                state_ref, dmat_ref, wq_ref, wk_ref):
    t = T_RET
    n_pairs = D_RET // LANES
    scale = 1.0 / math.sqrt(HEAD_DIM)
    lane = lax.broadcasted_iota(jnp.int32, (t, LANES), 1)
    first = lane < HEAD_DIM
    low_half = (lane % HEAD_DIM) < (HEAD_DIM // 2)
    cols = lambda p: slice(p * LANES, (p + 1) * LANES)

    @pl.when((pl.program_id(0) == 0) & (pl.program_id(1) == 0))
    def _():
        pos = lax.broadcasted_iota(jnp.int32, (t, 1), 0).astype(F32)
        r = lax.broadcasted_iota(jnp.int32, (t, t), 0)
        s = lax.broadcasted_iota(jnp.int32, (t, t), 1)
        dist = jnp.abs(r - s).astype(F32)
        visible = (s // CHUNK) <= (r // CHUNK)
        for h in range(RET_HEADS):
            lg = lg_ref[:, h * HEAD_DIM:h * HEAD_DIM + 1]
            dmat_ref[h] = jnp.where(visible, jnp.exp(lg * dist) * scale, 0.0)
        wq_ref[...] = jnp.exp(lg_ref[...] * (pos + 1.0)) * scale
        wk_ref[...] = jnp.exp(lg_ref[...] * (t - 1.0 - pos))

    @pl.when(pl.program_id(1) == 0)
    def _():
        state_ref[...] = jnp.zeros_like(state_ref)

    rb = lax.broadcasted_iota(jnp.int32, (LANES, LANES), 0)
    cb = lax.broadcasted_iota(jnp.int32, (LANES, LANES), 1)
    same_head = (rb // HEAD_DIM) == (cb // HEAD_DIM)

    def head_mean(x):
        s0 = jnp.sum(jnp.where(first, x, 0.0), axis=-1, keepdims=True)
        s1 = jnp.sum(jnp.where(first, 0.0, x), axis=-1, keepdims=True)
        return jnp.where(first, s0, s1) * (1.0 / HEAD_DIM)

    for sub in range(RET_BLK // t):
        rows = slice(sub * t, (sub + 1) * t)
        cos = cos_ref[rows, :]
        sin = sin_ref[rows, :]

        def rotary(x):
            swapped = jnp.where(low_half, pltpu.roll(x, LANES - HEAD_DIM // 2, 1),
                                pltpu.roll(x, HEAD_DIM // 2, 1))
            return x * cos + swapped * sin

        for p in range(n_pairs):
            q = rotary(q_ref[0, rows, cols(p)].astype(F32))
            k = rotary(k_ref[0, rows, cols(p)].astype(F32))
            vb = v_ref[0, rows, cols(p)]
            kb = k.astype(BF16)
            zero = jnp.zeros_like(q)
            inner = []
            for e in range(2):
                qe = jnp.where(first, q, zero) if e == 0 else jnp.where(first, zero, q)
                sc = lax.dot_general(qe.astype(BF16), kb, (((1,), (1,)), ((), ())),
                                     preferred_element_type=F32) * dmat_ref[2 * p + e]
                inner.append(jnp.dot(sc.astype(BF16), vb, preferred_element_type=F32))
            state = state_ref[p]
            o = jnp.where(first, inner[0], inner[1]) + jnp.dot(
                (q * wq_ref[:, cols(p)]).astype(BF16), state.astype(BF16),
                preferred_element_type=F32)
            upd = lax.dot_general((k * wk_ref[:, cols(p)]).astype(BF16), vb,
                                  (((0,), (0,)), ((), ())), preferred_element_type=F32)
            decay = jnp.exp(lg_ref[:, cols(p)] * float(t))
            state_ref[p] = jnp.where(same_head, state * decay + upd, 0.0)
            d = o - head_mean(o)
            var = head_mean(d * d)
            o_ref[0, rows, cols(p)] = (d * lax.rsqrt(var + EPS) * gn_ref[0, :, cols(p)]).astype(
                o_ref.dtype)


def _retention(rest, cos_t, sin_t, lg, gn_w, layer):
    batch, seq, _ = rest.shape
    n_pairs = D_RET // LANES
    tok = lambda blk: pl.BlockSpec((1, RET_BLK, D_RET), lambda b, i, blk=blk: (b, i, blk))
    fixed = pl.BlockSpec((1, D_RET), lambda b, i: (0, 0))
    table = pl.BlockSpec((RET_BLK, LANES), lambda b, i: (i, 0))
    return pl.pallas_call(
        _ret_kernel,
        grid=(batch, seq // RET_BLK),
        in_specs=[tok(4), tok(5), tok(6), table, table, fixed,
                  pl.BlockSpec((1, 1, D_RET), lambda b, i: (layer, 0, 0))],
        out_specs=pl.BlockSpec((1, RET_BLK, D_RET), lambda b, i: (b, i, 0)),
        out_shape=jax.ShapeDtypeStruct((batch, seq, D_RET), BF16),
        scratch_shapes=[
            pltpu.VMEM((n_pairs, LANES, LANES), F32),
            pltpu.VMEM((RET_HEADS, T_RET, T_RET), F32),
            pltpu.VMEM((T_RET, D_RET), F32),
            pltpu.VMEM((T_RET, D_RET), F32),
        ],
        compiler_params=_cparams(("arbitrary", "arbitrary")),
        name="retention",
    )(rest, rest, rest, cos_t, sin_t, lg, gn_w)


def _s5_kernel(u_ref, are_ref, aim_ref, ldt_ref, br_ref, bi_ref, cr_ref, ci_ref, d_ref, wg_ref,
               o_ref, wb_ref, wc_ref, lam_ref, lhs_ref, x_ref, z_ref, state_ref):
    nb = u_ref.shape[0]
    t5 = S5_T
    win = D_S5 // SUBLANES

    @pl.when(pl.program_id(0) == 0)
    def _():
        dt = jnp.exp(ldt_ref[0])
        ar = are_ref[0]
        ai = aim_ref[0]
        mag = jnp.exp(ar * dt)
        lr = mag * jnp.cos(ai * dt)
        li = mag * jnp.sin(ai * dt)
        den = ar * ar + ai * ai
        fr = ((lr - 1.0) * ar + li * ai) / den
        fi = (li * ar - (lr - 1.0) * ai) / den
        lam_ref[0] = lr
        lam_ref[1] = li
        for j in range(SUBLANES):
            rows = slice(win * j, win * (j + 1))
            f_r = fr[j:j + 1, :]
            f_i = fi[j:j + 1, :]
            b_r = br_ref[0, rows, :]
            b_i = bi_ref[0, rows, :]
            wb_ref[rows, :LANES] = (f_r * b_r - f_i * b_i).astype(BF16)
            wb_ref[rows, LANES:] = (f_r * b_i + f_i * b_r).astype(BF16)
        wc_ref[:LANES, :] = cr_ref[0].astype(BF16)
        wc_ref[LANES:, :] = (-ci_ref[0]).astype(BF16)
        state_ref[...] = jnp.zeros_like(state_ref)

    sub = lax.broadcasted_iota(jnp.int32, (2 * SUBLANES, D_S5), 0)
    keep = sub % SUBLANES == lax.broadcasted_iota(jnp.int32, (2 * SUBLANES, D_S5), 1) // win
    first = sub < SUBLANES

    for tt in range(t5 // SUBLANES):
        for b in range(nb):
            tile = u_ref[b, tt * SUBLANES:(tt + 1) * SUBLANES, :]
            for s in range(0, SUBLANES, 2):
                two = jnp.where(first,
                                jnp.broadcast_to(tile[s:s + 1, :], (2 * SUBLANES, D_S5)),
                                jnp.broadcast_to(tile[s + 1:s + 2, :], (2 * SUBLANES, D_S5)))
                row0 = (tt * SUBLANES + s) * SUBLANES
                lhs_ref[b, row0:row0 + 2 * SUBLANES, :] = jnp.where(keep, two, 0.0).astype(BF16)
    for b in range(nb):
        x_ref[b] = jnp.dot(lhs_ref[b], wb_ref[...], preferred_element_type=F32)

    lr = lam_ref[0]
    li = lam_ref[1]

    def scan(t, carry):
        rows = slice(t * SUBLANES, (t + 1) * SUBLANES)
        new = []
        for b in range(nb):
            xr, xi = carry[b]
            nr = lr * xr - li * xi + x_ref[b, rows, :LANES]
            ni = lr * xi + li * xr + x_ref[b, rows, LANES:]
            x_ref[b, rows, :LANES] = nr
            x_ref[b, rows, LANES:] = ni
            new.append((nr, ni))
        return tuple(new)

    final = tuple((state_ref[b, :, :LANES], state_ref[b, :, LANES:]) for b in range(nb))
    for t in range(t5):
        final = scan(t, final)
    for b in range(nb):
        state_ref[b, :, :LANES] = final[b][0]
        state_ref[b, :, LANES:] = final[b][1]

    window = lax.broadcasted_iota(jnp.int32, (t5, LANES), 1) // win
    for b in range(nb):
        z = jnp.dot(x_ref[b].astype(BF16), wc_ref[...], preferred_element_type=F32)
        z_ref[b, 0] = z[:, :LANES]
        z_ref[b, 1] = z[:, LANES:]
    for b in range(nb):
        halves = []
        for hh in range(2):
            per = SUBLANES // 2
            y = z_ref[b, hh, pl.ds(per * hh + per - 1, t5, stride=SUBLANES), :]
            for jj in range(per - 2, -1, -1):
                y = jnp.where(window == jj,
                              z_ref[b, hh, pl.ds(per * hh + jj, t5, stride=SUBLANES), :], y)
            halves.append(y)
        u = u_ref[b]
        y = jax.nn.gelu(jnp.concatenate(halves, axis=1) + d_ref[0] * u)
        gate = jnp.dot(y.astype(BF16), wg_ref[0], preferred_element_type=F32)
        o_ref[b] = (y * jax.nn.sigmoid(gate)).astype(o_ref.dtype)


def _s5(su, prm, layer):
    batch, seq, _ = su.shape
    fixed = lambda i: (layer, 0, 0)
    lam = pl.BlockSpec((1, SUBLANES, LANES), fixed)
    rows8 = SUBLANES * S5_T
    return pl.pallas_call(
        _s5_kernel,
        grid=(seq // S5_T,),
        in_specs=[
            pl.BlockSpec((batch, S5_T, D_S5), lambda i: (0, i, 0)),
            lam, lam, lam,
            pl.BlockSpec((1, D_S5, LANES), fixed), pl.BlockSpec((1, D_S5, LANES), fixed),
            pl.BlockSpec((1, LANES, D_S5), fixed), pl.BlockSpec((1, LANES, D_S5), fixed),
            pl.BlockSpec((1, 1, D_S5), fixed),
            pl.BlockSpec((1, D_S5, D_S5), fixed),
        ],
        out_specs=pl.BlockSpec((batch, S5_T, D_S5), lambda i: (0, i, 0)),
        out_shape=jax.ShapeDtypeStruct((batch, seq, D_S5), BF16),
        scratch_shapes=[
            pltpu.VMEM((D_S5, 2 * LANES), BF16),
            pltpu.VMEM((2 * LANES, D_S5), BF16),
            pltpu.VMEM((2, SUBLANES, LANES), F32),
            pltpu.VMEM((batch, rows8, D_S5), BF16),
            pltpu.VMEM((batch, rows8, 2 * LANES), F32),
            pltpu.VMEM((batch, 2, rows8, LANES), F32),
            pltpu.VMEM((batch, SUBLANES, 2 * LANES), F32),
        ],
        compiler_params=_cparams(("arbitrary",)),
        name="s5",
    )(su, prm["a_re"], prm["a_im"], prm["log_dt"], prm["b_re"], prm["b_im"],
      prm["c_re"], prm["c_im"], prm["d"], prm["w_glu"])


def _rotary_tables(seq):
    half = HEAD_DIM // 2
    lane = jnp.arange(LANES)
    freqs = ROPE_BASE ** (-(lane % half).astype(F32) / half)
    ang = jnp.arange(seq, dtype=F32)[:, None] * freqs[None, :]
    sign = jnp.where(lane % HEAD_DIM < half, -1.0, 1.0).astype(F32)
    return jnp.cos(ang), jnp.sin(ang) * sign[None, :]


def _s5_params(a_re, a_im, b_re, b_im, c_re, c_im, d, log_dt, w_glu):
    depth = a_re.shape[0]
    pair = jnp.eye(2, dtype=F32)
    half = S5_GROUPS // 2
    pack_b = lambda b: jnp.einsum(
        "ljnph,kn->ljkhnp", b.reshape(depth, half, 2, S5_STATE, S5_GROUP_CH), pair
    ).reshape(depth, D_S5, LANES)
    pack_c = lambda c: jnp.einsum(
        "ljnhp,kn->lnpjkh", c.reshape(depth, half, 2, S5_GROUP_CH, S5_STATE), pair
    ).reshape(depth, LANES, D_S5)
    tile = lambda v: v.reshape(depth, SUBLANES, LANES)
    return {
        "a_re": tile(a_re), "a_im": tile(a_im),
        "log_dt": tile(jnp.repeat(log_dt, S5_STATE, axis=-1)),
        "b_re": pack_b(b_re), "b_im": pack_b(b_im), "c_re": pack_c(c_re), "c_im": pack_c(c_im),
        "d": d.reshape(depth, 1, D_S5), "w_glu": w_glu.astype(BF16),
    }


def kernel(x, norm_w, w_in, fox_b_f, s5_a_re, s5_a_im, s5_b_re, s5_b_im, s5_c_re, s5_c_im,
           s5_d, s5_log_dt, s5_w_glu, ret_gn_w, w_out, final_norm_w):
    batch, seq, _ = x.shape
    depth = w_in.shape[0]
    m = batch * seq
    cos_t, sin_t = _rotary_tables(seq)
    log_gamma = jnp.log1p(-(2.0 ** (-5.0 - jnp.arange(RET_HEADS, dtype=F32))))
    lg = jnp.repeat(log_gamma, HEAD_DIM).reshape(1, D_RET)
    norm_g = jnp.concatenate([norm_w, final_norm_w[None]], axis=0).reshape(depth + 1, 1, D_MODEL)

    order = jnp.argsort(fox_b_f, axis=1).astype(jnp.int32)
    bf_sorted = jnp.take_along_axis(fox_b_f, order, axis=1)
    bf_pad = jnp.pad(bf_sorted, ((0, 0), (0, LANES - FOX_HEADS))).reshape(depth, 1, LANES)
    gn_w = ret_gn_w.reshape(depth, 1, D_RET)
    s5_prm = _s5_params(s5_a_re, s5_a_im, s5_b_re, s5_b_im, s5_c_re, s5_c_im, s5_d, s5_log_dt,
                        s5_w_glu)

    w_in_t = jnp.swapaxes(w_in, 1, 2)
    fox_rows = jnp.take_along_axis(w_out[:, :D_FOX].reshape(depth, FOX_HEADS, HEAD_DIM, D_MODEL),
                                   order[:, :, None, None], axis=1).reshape(depth, D_FOX, D_MODEL)
    w_out_p = jnp.concatenate([fox_rows, w_out[:, D_FOX:]], axis=1).astype(BF16)
    x2 = x.reshape(m, D_MODEL)
    qkv, su, rest, flog = _projections("first", m, order[0], 0, 0, (x2, norm_g, w_in_t))
    for l in range(depth):
        rest3 = rest.reshape(batch, seq, N_REST - D_S5)
        y_fox = _fox_attention(qkv.reshape(batch, seq, N_QKV), flog.reshape(batch, seq, LANES),
                               bf_pad, l)
        y_s5 = _s5(su.reshape(batch, seq, D_S5), s5_prm, l)
        y_ret = _retention(rest3, cos_t, sin_t, lg, gn_w, l)
        ys = (y_fox.reshape(m, D_FOX), y_s5.reshape(m, D_S5), y_ret.reshape(m, D_RET), rest, x2)
        if l < depth - 1:
            x2, qkv, su, rest, flog = _projections(
                "mid", m, order[l + 1], l + 1, l + 1, ys + (w_out_p[l], norm_g, w_in_t))
    out = _projections("last", m, order[depth - 1], depth, 0, ys + (w_out_p[depth - 1], norm_g))
    return out.reshape(batch, seq, D_MODEL)
```

```python
import math

import jax
import jax.numpy as jnp
from jax import lax
from jax.experimental import pallas as pl
from jax.experimental.pallas import tpu as pltpu

F32 = jnp.float32
BF16 = jnp.bfloat16

D_MODEL = 1024
HEAD_DIM = 64
CHUNK = 64
D_FOX = 512
FOX_HEADS = 8
D_S5 = 256
S5_GROUPS = 16
S5_GROUP_CH = 16
S5_STATE = 64
D_RET = 256
RET_HEADS = 4
ROPE_BASE = 10000.0
EPS = 1e-6

LANES = 128
SUBLANES = 8
N_QKV = 3 * D_FOX
N_REST = D_S5 + D_MODEL + 3 * D_RET
N_ALL = N_QKV + N_REST + LANES
D_IN_PROJ = N_QKV + FOX_HEADS + N_REST
VMEM_LIMIT = 56 * 1024 * 1024

TM_PROJ = 512
TM_LAST = 1024
T_FOX = 512
FOX_GROUP = 2
T_RET = 256
RET_BLK = 512
S5_T = 256
S5_BATCH_SPLIT = 2
NEG_BIG = -1e30
FOX_SKIP_LOG2 = 40.0
FOX_NORM_SLACK = 1.02
LOG2E = math.log2(math.e)
Q_SCALE = LOG2E / math.sqrt(HEAD_DIM)


def _cparams(sem):
    return pltpu.CompilerParams(dimension_semantics=sem, vmem_limit_bytes=VMEM_LIMIT)


def _rms_norm(x, g):
    ms = jnp.mean(x * x, axis=-1, keepdims=True)
    return x * lax.rsqrt(ms + EPS) * g


def _pack_in_proj(order_ref, wt_ref, win_ref):
    su_lo = N_QKV + FOX_HEADS
    rqkv_lo = su_lo + D_S5
    gate_lo = rqkv_lo + 3 * D_RET

    def copy_rows(dst, src, size):
        win_ref[dst:dst + size, :] = wt_ref[0, src, :].astype(BF16)

    def copy_heads(dst, src):
        for n in range(FOX_HEADS):
            start = pl.multiple_of(src + order_ref[n] * HEAD_DIM, HEAD_DIM)
            copy_rows(dst + n * HEAD_DIM, pl.ds(start, HEAD_DIM), HEAD_DIM)

    def copy_plain(dst, lo, hi):
        for start in range(lo, hi, 256):
            size = min(256, hi - start)
            copy_rows(dst + start - lo, slice(start, start + size), size)

    for part in range(3):
        copy_heads(part * D_FOX, part * D_FOX)
    copy_plain(N_QKV, su_lo, rqkv_lo)
    copy_heads(N_QKV + D_S5, gate_lo)
    copy_plain(N_QKV + D_S5 + D_FOX, gate_lo + D_FOX, D_IN_PROJ)
    copy_plain(N_QKV + D_S5 + D_MODEL, rqkv_lo, gate_lo)
    flog = [wt_ref[0, pl.ds(N_QKV + order_ref[n], 1), :] for n in range(FOX_HEADS)]
    flog.append(jnp.zeros((LANES - FOX_HEADS, D_MODEL), F32))
    win_ref[N_QKV + N_REST:, :] = jnp.concatenate(flog, axis=0).astype(BF16)


def _project_in(h, w_ref, oa_ref, su_ref, ob_ref, fl_ref):
    nt = (((1,), (1,)), ((), ()))
    qkv = lax.dot_general(h, w_ref[:N_QKV, :], nt, preferred_element_type=F32)
    oa_ref[:, :D_FOX] = (qkv[:, :D_FOX] * Q_SCALE).astype(oa_ref.dtype)
    oa_ref[:, D_FOX:] = qkv[:, D_FOX:].astype(oa_ref.dtype)
    rest = lax.dot_general(h, w_ref[N_QKV:N_QKV + N_REST, :], nt, preferred_element_type=F32)
    su_ref[...] = rest[:, :D_S5]
    ob_ref[...] = rest[:, D_S5:].astype(ob_ref.dtype)
    fl_ref[...] = lax.dot_general(h, w_ref[N_QKV + N_REST:, :], nt, preferred_element_type=F32)


def _project_out(yf_ref, ys_ref, yr_ref, gate_ref, x_ref, w_ref):
    g = gate_ref[...].astype(F32)
    g = g * jax.nn.sigmoid(g)
    acc = x_ref[...]
    off = 0
    for y_ref in (yf_ref, ys_ref, yr_ref):
        width = y_ref.shape[-1]
        y = (y_ref[...] * g[:, off:off + width]).astype(BF16)
        acc = acc + jnp.dot(y, w_ref[off:off + width, :], preferred_element_type=F32)
        off += width
    return acc


def _first_kernel(order_ref, x_ref, g_ref, wt_ref, oa_ref, su_ref, ob_ref, fl_ref, win_ref):
    @pl.when(pl.program_id(0) == 0)
    def _():
        _pack_in_proj(order_ref, wt_ref, win_ref)

    h = _rms_norm(x_ref[...], g_ref[0]).astype(BF16)
    _project_in(h, win_ref, oa_ref, su_ref, ob_ref, fl_ref)


def _mid_kernel(order_ref, yf_ref, ys_ref, yr_ref, gate_ref, x_ref, wo_ref, g_ref, wt_ref,
                xo_ref, oa_ref, su_ref, ob_ref, fl_ref, win_ref):
    @pl.when(pl.program_id(0) == 0)
    def _():
        _pack_in_proj(order_ref, wt_ref, win_ref)

    acc = _project_out(yf_ref, ys_ref, yr_ref, gate_ref, x_ref, wo_ref)
    xo_ref[...] = acc
    h = _rms_norm(acc, g_ref[0]).astype(BF16)
    _project_in(h, win_ref, oa_ref, su_ref, ob_ref, fl_ref)


def _last_kernel(order_ref, yf_ref, ys_ref, yr_ref, gate_ref, x_ref, wo_ref, g_ref, o_ref):
    acc = _project_out(yf_ref, ys_ref, yr_ref, gate_ref, x_ref, wo_ref)
    o_ref[...] = _rms_norm(acc, g_ref[0])


def _projections(kind, m, order, norm_layer, in_layer, operands):
    tm = TM_LAST if kind == "last" else TM_PROJ
    row = lambda i, order: (i, 0)
    fixed = lambda i, order: (0, 0)
    tile = pl.BlockSpec((tm, D_MODEL), row)
    once = pl.Buffered(1)
    norm_spec = pl.BlockSpec((1, 1, D_MODEL), lambda i, order: (norm_layer, 0, 0))
    wt_spec = pl.BlockSpec((1, D_IN_PROJ, D_MODEL), lambda i, order: (in_layer, 0, 0),
                           pipeline_mode=once)
    wo_spec = pl.BlockSpec((D_MODEL, D_MODEL), fixed, pipeline_mode=once)
    out_in_specs = [pl.BlockSpec((tm, D_FOX), row), pl.BlockSpec((tm, D_S5), row),
                    pl.BlockSpec((tm, D_RET), row), tile, tile]
    in_out_specs = [pl.BlockSpec((tm, N_QKV), row), pl.BlockSpec((tm, D_S5), row),
                    pl.BlockSpec((tm, N_REST - D_S5), row),
                    pl.BlockSpec((tm, LANES), row)]
    in_out_shapes = [jax.ShapeDtypeStruct((m, N_QKV), BF16), jax.ShapeDtypeStruct((m, D_S5), F32),
                     jax.ShapeDtypeStruct((m, N_REST - D_S5), BF16),
                     jax.ShapeDtypeStruct((m, LANES), F32)]
    x_shape = jax.ShapeDtypeStruct((m, D_MODEL), F32)
    packed_w = [pltpu.VMEM((N_ALL, D_MODEL), BF16)]
    if kind == "first":
        body, in_specs, scratch = _first_kernel, [tile, norm_spec, wt_spec], packed_w
        out_specs, out_shape = in_out_specs, in_out_shapes
    elif kind == "mid":
        body, scratch = _mid_kernel, packed_w
        in_specs = out_in_specs + [wo_spec, norm_spec, wt_spec]
        out_specs, out_shape = [tile] + in_out_specs, [x_shape] + in_out_shapes
    else:
        body, in_specs, scratch = _last_kernel, out_in_specs + [wo_spec, norm_spec], []
        out_specs, out_shape = tile, x_shape
    grid_spec = pltpu.PrefetchScalarGridSpec(
        num_scalar_prefetch=1, grid=(m // tm,), in_specs=in_specs, out_specs=out_specs,
        scratch_shapes=scratch)
    return pl.pallas_call(
        body,
        grid_spec=grid_spec,
        out_shape=out_shape,
        compiler_params=_cparams(("arbitrary",)),
        name="proj_" + kind,
    )(order, *operands)


def _fox_kernel(q_ref, k_ref, v_ref, fl_ref, bf_ref, o_ref, kaug_ref, vaug_ref, c_ref, clast_ref,
                knorm_ref):
    t = T_FOX
    i = pl.program_id(1)
    seq = k_ref.shape[1]
    lane = lax.broadcasted_iota(jnp.int32, (t, LANES), 1)
    lane1 = lane[0:1, :]
    own = (lane < HEAD_DIM, lane >= HEAD_DIM)
    n_pairs = D_FOX // LANES
    heads = [(g, e) for g in range(n_pairs) for e in range(2)]
    cols = lambda g: slice(g * LANES, (g + 1) * LANES)
    x0 = lambda e: HEAD_DIM if e == 0 else 0
    shift = lambda g, e: (x0(e) - (2 * g + e)) % LANES

    half_id = lax.broadcasted_iota(jnp.int32, (LANES, LANES), 0) // HEAD_DIM
    same_half = (half_id == lax.broadcasted_iota(jnp.int32, (LANES, LANES), 1) // HEAD_DIM)
    head_sum = jnp.where(same_half, 1.0, 0.0).astype(BF16)

    def max_sq_norms(ref, rows):
        out = jnp.zeros((1, LANES), F32)
        for g in range(n_pairs):
            x = ref[0, rows, cols(g)]
            sums = jnp.dot(x * x, head_sum, preferred_element_type=F32)
            top = jnp.max(sums, axis=0, keepdims=True) * FOX_NORM_SLACK
            out = jnp.where(lane1 == 2 * g, top,
                            jnp.where(lane1 == 2 * g + 1, pltpu.roll(top, HEAD_DIM, 1), out))
        return out

    @pl.when(i == 0)
    def _():
        r = lax.broadcasted_iota(jnp.int32, (t, t), 0)
        s = lax.broadcasted_iota(jnp.int32, (t, t), 1)
        tri = jnp.where(s <= r, 1.0, 0.0).astype(BF16)

        def build(j, carried):
            carry, k_max = carried
            rows = pl.ds(pl.multiple_of(j * t, t), t)
            z = fl_ref[0, rows, :] + bf_ref[0]
            logf = jnp.minimum(z, 0.0) - jnp.log1p(jnp.exp(-jnp.abs(z)))
            top = logf.astype(BF16)
            rem = logf - top.astype(F32)
            middle = rem.astype(BF16)
            bottom = (rem - middle.astype(F32)).astype(BF16)
            c = carry + sum(jnp.dot(tri, piece, preferred_element_type=F32)
                            for piece in (bottom, middle, top))
            c_ref[rows, :] = c
            clast_ref[pl.ds(j, 1), :] = c[t - 1:t, :]
            k_max = jnp.maximum(k_max, max_sq_norms(k_ref, rows))
            knorm_ref[pl.ds(j, 1), :] = k_max
            b = -LOG2E * c
            hi = b.astype(BF16).astype(F32)
            mid = (b - hi).astype(BF16).astype(F32)
            lo = b - hi - mid
            pieces = jnp.where(lane < 8, hi, jnp.where(lane < 16, pltpu.roll(mid, 8, 1), jnp.where(
                lane < 24, pltpu.roll(lo, 16, 1), jnp.where(lane < 32, 1.0, 0.0))))
            for n, (g, e) in enumerate(heads):
                extra = pltpu.roll(pieces, shift(g, e), 1).astype(BF16)
                kaug_ref[n, rows, :] = jnp.where(own[e], k_ref[0, rows, cols(g)], extra)
                v = v_ref[0, rows, cols(g)]
                vaug_ref[n, rows, :] = jnp.where(own[e], v, jnp.ones_like(v))
            return c[t - 1:t, :], k_max

        zero_row = jnp.zeros((1, LANES), F32)
        carried = (zero_row, zero_row)
        for j in range(seq // t):
            carried = build(j, carried)

    cq0 = LOG2E * c_ref[pl.ds(pl.multiple_of(i * t, t), SUBLANES), :][0:1, :]
    qa = []
    for g, e in heads:
        ones = (lane1 == x0(e)) | (lane1 == x0(e) + 8) | (lane1 == x0(e) + 16)
        extra = jnp.where(ones, 1.0, jnp.where(
            lane1 == x0(e) + 24, pltpu.roll(cq0, (shift(g, e) + 24) % LANES, 1), 0.0))
        qa.append(jnp.where(own[e], q_ref[0, :, cols(g)], extra.astype(BF16)))

    def tile(j, carry, masked, ids):
        m, acc = carry
        rows = pl.ds(pl.multiple_of(j * t, t), t)
        new_m, new_acc = [], []
        for pos, n in enumerate(ids):
            s = lax.dot_general(qa[n], kaug_ref[n, rows, :], (((1,), (1,)), ((), ())),
                                preferred_element_type=F32)
            if masked:
                r = lax.broadcasted_iota(jnp.int32, (t, t), 0)
                cidx = lax.broadcasted_iota(jnp.int32, (t, t), 1)
                s = jnp.where(cidx <= r, s, NEG_BIG)
            m_new = jnp.maximum(m[pos], jnp.max(s, axis=-1, keepdims=True))
            alpha = jnp.exp2(m[pos] - m_new)
            p = jnp.exp2(s - m_new).astype(BF16)
            new_m.append(m_new)
            new_acc.append(acc[pos] * alpha + jnp.dot(p, vaug_ref[n, rows, :],
                                                      preferred_element_type=F32))
        return tuple(new_m), tuple(new_acc)

    n_tiles = seq // t
    tile_id = lax.broadcasted_iota(jnp.int32, (n_tiles, LANES), 0)
    cq0_used = cq0.astype(BF16).astype(F32)
    upper = (jnp.sqrt(max_sq_norms(q_ref, slice(None)) * knorm_ref[...])
             + (cq0_used - LOG2E * clast_ref[...]))

    def needed(j, m, ids):
        row_min = jnp.zeros((1, LANES), F32)
        for pos, n in enumerate(ids):
            row_min = jnp.where(lane1 == n, jnp.min(m[pos], axis=0, keepdims=True), row_min)
        upper_j = jnp.sum(jnp.where(tile_id == j, upper, 0.0), axis=0, keepdims=True)
        hit = (upper_j >= row_min - FOX_SKIP_LOG2) & (lane1 >= ids[0]) & (lane1 <= ids[-1])
        return (j >= 0) & (jnp.max(jnp.where(hit, 1.0, 0.0)) > 0.0)

    acc = []
    for first in range(0, len(heads), FOX_GROUP):
        ids = list(range(first, first + FOX_GROUP))
        init = ((jnp.full((t, 1), NEG_BIG, F32),) * FOX_GROUP,
                (jnp.zeros((t, LANES), F32),) * FOX_GROUP)
        m, group_acc = tile(i, init, True, ids)

        def walk(state, ids=ids):
            j, _, m, group_acc = state
            m, group_acc = tile(j, (m, group_acc), False, ids)
            return j - 1, needed(j - 1, m, ids), m, group_acc

        state = lax.while_loop(lambda state: state[1], walk,
                               (i - 1, needed(i - 1, m, ids), m, group_acc))
        acc.extend(state[3])
    outs = [a / pltpu.roll(a, HEAD_DIM, 1) for a in acc]
    for g in range(n_pairs):
        o_ref[0, :, cols(g)] = jnp.where(own[0], outs[2 * g], outs[2 * g + 1]).astype(o_ref.dtype)


def _fox_attention(qkv, flog, bf_pad, layer):
    batch, seq, _ = qkv.shape
    n_heads = D_FOX // HEAD_DIM
    once = pl.Buffered(1)
    return pl.pallas_call(
        _fox_kernel,
        grid=(batch, seq // T_FOX),
        in_specs=[
            pl.BlockSpec((1, T_FOX, D_FOX), lambda b, i: (b, i, 0)),
            pl.BlockSpec((1, seq, D_FOX), lambda b, i: (b, 0, 1), pipeline_mode=once),
            pl.BlockSpec((1, seq, D_FOX), lambda b, i: (b, 0, 2), pipeline_mode=once),
            pl.BlockSpec((1, seq, LANES), lambda b, i: (b, 0, 0), pipeline_mode=once),
            pl.BlockSpec((1, 1, LANES), lambda b, i: (layer, 0, 0)),
        ],
        out_specs=pl.BlockSpec((1, T_FOX, D_FOX), lambda b, i: (b, i, 0)),
        out_shape=jax.ShapeDtypeStruct((batch, seq, D_FOX), BF16),
        scratch_shapes=[pltpu.VMEM((n_heads, seq, LANES), BF16),
                        pltpu.VMEM((n_heads, seq, LANES), BF16),
                        pltpu.VMEM((seq, LANES), F32),
                        pltpu.VMEM((seq // T_FOX, LANES), F32),
                        pltpu.VMEM((seq // T_FOX, LANES), F32)],
        compiler_params=_cparams(("parallel", "arbitrary")),
        name="fox_attention",
    )(qkv, qkv, qkv, flog, bf_pad)


def _ret_kernel(q_ref, k_ref, v_ref, cos_ref, sin_ref, lg_ref, gn_ref, o_ref,
                state_ref, dmat_ref, wq_ref, wk_ref):
    t = T_RET
    n_pairs = D_RET // LANES
    scale = 1.0 / math.sqrt(HEAD_DIM)
    lane = lax.broadcasted_iota(jnp.int32, (t, LANES), 1)
    first = lane < HEAD_DIM
    low_half = (lane % HEAD_DIM) < (HEAD_DIM // 2)
    cols = lambda p: slice(p * LANES, (p + 1) * LANES)

    @pl.when((pl.program_id(0) == 0) & (pl.program_id(1) == 0))
    def _():
        pos = lax.broadcasted_iota(jnp.int32, (t, 1), 0).astype(F32)
        r = lax.broadcasted_iota(jnp.int32, (t, t), 0)
        s = lax.broadcasted_iota(jnp.int32, (t, t), 1)
        dist = jnp.abs(r - s).astype(F32)
        visible = (s // CHUNK) <= (r // CHUNK)
        for h in range(RET_HEADS):
            lg = lg_ref[:, h * HEAD_DIM:h * HEAD_DIM + 1]
            dmat_ref[h] = jnp.where(visible, jnp.exp(lg * dist) * scale, 0.0)
        wq_ref[...] = jnp.exp(lg_ref[...] * (pos + 1.0)) * scale
        wk_ref[...] = jnp.exp(lg_ref[...] * (t - 1.0 - pos))

    @pl.when(pl.program_id(1) == 0)
    def _():
        state_ref[...] = jnp.zeros_like(state_ref)

    rb = lax.broadcasted_iota(jnp.int32, (LANES, LANES), 0)
    cb = lax.broadcasted_iota(jnp.int32, (LANES, LANES), 1)
    same_head = (rb // HEAD_DIM) == (cb // HEAD_DIM)

    def head_mean(x):
        s0 = jnp.sum(jnp.where(first, x, 0.0), axis=-1, keepdims=True)
        s1 = jnp.sum(jnp.where(first, 0.0, x), axis=-1, keepdims=True)
        return jnp.where(first, s0, s1) * (1.0 / HEAD_DIM)

    for sub in range(RET_BLK // t):
        rows = slice(sub * t, (sub + 1) * t)
        cos = cos_ref[rows, :]
        sin = sin_ref[rows, :]

        def rotary(x):
            swapped = jnp.where(low_half, pltpu.roll(x, LANES - HEAD_DIM // 2, 1),
                                pltpu.roll(x, HEAD_DIM // 2, 1))
            return x * cos + swapped * sin

        for p in range(n_pairs):
            q = rotary(q_ref[0, rows, cols(p)].astype(F32))
            k = rotary(k_ref[0, rows, cols(p)].astype(F32))
            vb = v_ref[0, rows, cols(p)]
            kb = k.astype(BF16)
            zero = jnp.zeros_like(q)
            inner = []
            for e in range(2):
                qe = jnp.where(first, q, zero) if e == 0 else jnp.where(first, zero, q)
                sc = lax.dot_general(qe.astype(BF16), kb, (((1,), (1,)), ((), ())),
                                     preferred_element_type=F32) * dmat_ref[2 * p + e]
                inner.append(jnp.dot(sc.astype(BF16), vb, preferred_element_type=F32))
            state = state_ref[p]
            o = jnp.where(first, inner[0], inner[1]) + jnp.dot(
                (q * wq_ref[:, cols(p)]).astype(BF16), state.astype(BF16),
                preferred_element_type=F32)
            upd = lax.dot_general((k * wk_ref[:, cols(p)]).astype(BF16), vb,
                                  (((0,), (0,)), ((), ())), preferred_element_type=F32)
            decay = jnp.exp(lg_ref[:, cols(p)] * float(t))
            state_ref[p] = jnp.where(same_head, state * decay + upd, 0.0)
            d = o - head_mean(o)
            var = head_mean(d * d)
            o_ref[0, rows, cols(p)] = (d * lax.rsqrt(var + EPS) * gn_ref[0, :, cols(p)]).astype(
                o_ref.dtype)


def _retention(rest, cos_t, sin_t, lg, gn_w, layer):
    batch, seq, _ = rest.shape
    n_pairs = D_RET // LANES
    tok = lambda blk: pl.BlockSpec((1, RET_BLK, D_RET), lambda b, i, blk=blk: (b, i, blk))
    fixed = pl.BlockSpec((1, D_RET), lambda b, i: (0, 0))
    table = pl.BlockSpec((RET_BLK, LANES), lambda b, i: (i, 0))
    return pl.pallas_call(
        _ret_kernel,
        grid=(batch, seq // RET_BLK),
        in_specs=[tok(4), tok(5), tok(6), table, table, fixed,
                  pl.BlockSpec((1, 1, D_RET), lambda b, i: (layer, 0, 0))],
        out_specs=pl.BlockSpec((1, RET_BLK, D_RET), lambda b, i: (b, i, 0)),
        out_shape=jax.ShapeDtypeStruct((batch, seq, D_RET), BF16),
        scratch_shapes=[
            pltpu.VMEM((n_pairs, LANES, LANES), F32),
            pltpu.VMEM((RET_HEADS, T_RET, T_RET), F32),
            pltpu.VMEM((T_RET, D_RET), F32),
            pltpu.VMEM((T_RET, D_RET), F32),
        ],
        compiler_params=_cparams(("arbitrary", "arbitrary")),
        name="retention",
    )(rest, rest, rest, cos_t, sin_t, lg, gn_w)


def _s5_kernel(u_ref, are_ref, aim_ref, ldt_ref, br_ref, bi_ref, cr_ref, ci_ref, d_ref, wg_ref,
               o_ref, wb_ref, wc_ref, lam_ref, lhs_ref, x_ref, z_ref, state_ref):
    nb = u_ref.shape[0]
    t5 = S5_T
    win = D_S5 // SUBLANES

    @pl.when(pl.program_id(1) == 0)
    def _():
        dt = jnp.exp(ldt_ref[0])
        ar = are_ref[0]
        ai = aim_ref[0]
        mag = jnp.exp(ar * dt)
        lr = mag * jnp.cos(ai * dt)
        li = mag * jnp.sin(ai * dt)
        den = ar * ar + ai * ai
        fr = ((lr - 1.0) * ar + li * ai) / den
        fi = (li * ar - (lr - 1.0) * ai) / den
        lam_ref[0] = lr
        lam_ref[1] = li
        for j in range(SUBLANES):
            rows = slice(win * j, win * (j + 1))
            f_r = fr[j:j + 1, :]
            f_i = fi[j:j + 1, :]
            b_r = br_ref[0, rows, :]
            b_i = bi_ref[0, rows, :]
            wb_ref[rows, :LANES] = (f_r * b_r - f_i * b_i).astype(BF16)
            wb_ref[rows, LANES:] = (f_r * b_i + f_i * b_r).astype(BF16)
        wc_ref[:LANES, :] = cr_ref[0].astype(BF16)
        wc_ref[LANES:, :] = (-ci_ref[0]).astype(BF16)
        state_ref[...] = jnp.zeros_like(state_ref)

    sub = lax.broadcasted_iota(jnp.int32, (2 * SUBLANES, D_S5), 0)
    keep = sub % SUBLANES == lax.broadcasted_iota(jnp.int32, (2 * SUBLANES, D_S5), 1) // win
    first = sub < SUBLANES

    for tt in range(t5 // SUBLANES):
        for b in range(nb):
            tile = u_ref[b, tt * SUBLANES:(tt + 1) * SUBLANES, :]
            for s in range(0, SUBLANES, 2):
                two = jnp.where(first,
                                jnp.broadcast_to(tile[s:s + 1, :], (2 * SUBLANES, D_S5)),
                                jnp.broadcast_to(tile[s + 1:s + 2, :], (2 * SUBLANES, D_S5)))
                row0 = (tt * SUBLANES + s) * SUBLANES
                lhs_ref[b, row0:row0 + 2 * SUBLANES, :] = jnp.where(keep, two, 0.0).astype(BF16)
    for b in range(nb):
        x_ref[b] = jnp.dot(lhs_ref[b], wb_ref[...], preferred_element_type=F32)

    lr = lam_ref[0]
    li = lam_ref[1]

    def scan(t, carry):
        rows = slice(t * SUBLANES, (t + 1) * SUBLANES)
        new = []
        for b in range(nb):
            xr, xi = carry[b]
            nr = lr * xr - li * xi + x_ref[b, rows, :LANES]
            ni = lr * xi + li * xr + x_ref[b, rows, LANES:]
            x_ref[b, rows, :LANES] = nr
            x_ref[b, rows, LANES:] = ni
            new.append((nr, ni))
        return tuple(new)

    final = tuple((state_ref[b, :, :LANES], state_ref[b, :, LANES:]) for b in range(nb))
    for t in range(t5):
        final = scan(t, final)
    for b in range(nb):
        state_ref[b, :, :LANES] = final[b][0]
        state_ref[b, :, LANES:] = final[b][1]

    window = lax.broadcasted_iota(jnp.int32, (t5, LANES), 1) // win
    for b in range(nb):
        z = jnp.dot(x_ref[b].astype(BF16), wc_ref[...], preferred_element_type=F32)
        z_ref[b, 0] = z[:, :LANES]
        z_ref[b, 1] = z[:, LANES:]
    for b in range(nb):
        halves = []
        for hh in range(2):
            per = SUBLANES // 2
            y = z_ref[b, hh, pl.ds(per * hh + per - 1, t5, stride=SUBLANES), :]
            for jj in range(per - 2, -1, -1):
                y = jnp.where(window == jj,
                              z_ref[b, hh, pl.ds(per * hh + jj, t5, stride=SUBLANES), :], y)
            halves.append(y)
        u = u_ref[b]
        y = jax.nn.gelu(jnp.concatenate(halves, axis=1) + d_ref[0] * u)
        gate = jnp.dot(y.astype(BF16), wg_ref[0], preferred_element_type=F32)
        o_ref[b] = (y * jax.nn.sigmoid(gate)).astype(o_ref.dtype)


def _s5(su, prm, layer):
    batch, seq, _ = su.shape
    fixed = lambda g, i: (layer, 0, 0)
    nb = batch // S5_BATCH_SPLIT
    lam = pl.BlockSpec((1, SUBLANES, LANES), fixed)
    rows8 = SUBLANES * S5_T
    return pl.pallas_call(
        _s5_kernel,
        grid=(S5_BATCH_SPLIT, seq // S5_T),
        in_specs=[
            pl.BlockSpec((nb, S5_T, D_S5), lambda g, i: (g, i, 0)),
            lam, lam, lam,
            pl.BlockSpec((1, D_S5, LANES), fixed), pl.BlockSpec((1, D_S5, LANES), fixed),
            pl.BlockSpec((1, LANES, D_S5), fixed), pl.BlockSpec((1, LANES, D_S5), fixed),
            pl.BlockSpec((1, 1, D_S5), fixed),
            pl.BlockSpec((1, D_S5, D_S5), fixed),
        ],
        out_specs=pl.BlockSpec((nb, S5_T, D_S5), lambda g, i: (g, i, 0)),
        out_shape=jax.ShapeDtypeStruct((batch, seq, D_S5), BF16),
        scratch_shapes=[
            pltpu.VMEM((D_S5, 2 * LANES), BF16),
            pltpu.VMEM((2 * LANES, D_S5), BF16),
            pltpu.VMEM((2, SUBLANES, LANES), F32),
            pltpu.VMEM((nb,rows8, D_S5), BF16),
            pltpu.VMEM((nb,rows8, 2 * LANES), F32),
            pltpu.VMEM((nb,2, rows8, LANES), F32),
            pltpu.VMEM((nb,SUBLANES, 2 * LANES), F32),
        ],
        compiler_params=_cparams(("arbitrary", "arbitrary")),
        name="s5",
    )(su, prm["a_re"], prm["a_im"], prm["log_dt"], prm["b_re"], prm["b_im"],
      prm["c_re"], prm["c_im"], prm["d"], prm["w_glu"])


def _rotary_tables(seq):
    half = HEAD_DIM // 2
    lane = jnp.arange(LANES)
    freqs = ROPE_BASE ** (-(lane % half).astype(F32) / half)
    ang = jnp.arange(seq, dtype=F32)[:, None] * freqs[None, :]
    sign = jnp.where(lane % HEAD_DIM < half, -1.0, 1.0).astype(F32)
    return jnp.cos(ang), jnp.sin(ang) * sign[None, :]


def _s5_params(a_re, a_im, b_re, b_im, c_re, c_im, d, log_dt, w_glu):
    depth = a_re.shape[0]
    pair = jnp.eye(2, dtype=F32)
    half = S5_GROUPS // 2
    pack_b = lambda b: jnp.einsum(
        "ljnph,kn->ljkhnp", b.reshape(depth, half, 2, S5_STATE, S5_GROUP_CH), pair
    ).reshape(depth, D_S5, LANES)
    pack_c = lambda c: jnp.einsum(
        "ljnhp,kn->lnpjkh", c.reshape(depth, half, 2, S5_GROUP_CH, S5_STATE), pair
    ).reshape(depth, LANES, D_S5)
    tile = lambda v: v.reshape(depth, SUBLANES, LANES)
    return {
        "a_re": tile(a_re), "a_im": tile(a_im),
        "log_dt": tile(jnp.repeat(log_dt, S5_STATE, axis=-1)),
        "b_re": pack_b(b_re), "b_im": pack_b(b_im), "c_re": pack_c(c_re), "c_im": pack_c(c_im),
        "d": d.reshape(depth, 1, D_S5), "w_glu": w_glu.astype(BF16),
    }


def kernel(x, norm_w, w_in, fox_b_f, s5_a_re, s5_a_im, s5_b_re, s5_b_im, s5_c_re, s5_c_im,
           s5_d, s5_log_dt, s5_w_glu, ret_gn_w, w_out, final_norm_w):
    batch, seq, _ = x.shape
    depth = w_in.shape[0]
    m = batch * seq
    cos_t, sin_t = _rotary_tables(seq)
    log_gamma = jnp.log1p(-(2.0 ** (-5.0 - jnp.arange(RET_HEADS, dtype=F32))))
    lg = jnp.repeat(log_gamma, HEAD_DIM).reshape(1, D_RET)
    norm_g = jnp.concatenate([norm_w, final_norm_w[None]], axis=0).reshape(depth + 1, 1, D_MODEL)

    order = jnp.argsort(fox_b_f, axis=1).astype(jnp.int32)
    bf_sorted = jnp.take_along_axis(fox_b_f, order, axis=1)
    bf_pad = jnp.pad(bf_sorted, ((0, 0), (0, LANES - FOX_HEADS))).reshape(depth, 1, LANES)
    gn_w = ret_gn_w.reshape(depth, 1, D_RET)
    s5_prm = _s5_params(s5_a_re, s5_a_im, s5_b_re, s5_b_im, s5_c_re, s5_c_im, s5_d, s5_log_dt,
                        s5_w_glu)

    w_in_t = jnp.swapaxes(w_in, 1, 2)
    fox_rows = jnp.take_along_axis(w_out[:, :D_FOX].reshape(depth, FOX_HEADS, HEAD_DIM, D_MODEL),
                                   order[:, :, None, None], axis=1).reshape(depth, D_FOX, D_MODEL)
    w_out_p = jnp.concatenate([fox_rows, w_out[:, D_FOX:]], axis=1).astype(BF16)
    x2 = x.reshape(m, D_MODEL)
    qkv, su, rest, flog = _projections("first", m, order[0], 0, 0, (x2, norm_g, w_in_t))
    for l in range(depth):
        rest3 = rest.reshape(batch, seq, N_REST - D_S5)
        y_fox = _fox_attention(qkv.reshape(batch, seq, N_QKV), flog.reshape(batch, seq, LANES),
                               bf_pad, l)
        y_s5 = _s5(su.reshape(batch, seq, D_S5), s5_prm, l)
        y_ret = _retention(rest3, cos_t, sin_t, lg, gn_w, l)
        ys = (y_fox.reshape(m, D_FOX), y_s5.reshape(m, D_S5), y_ret.reshape(m, D_RET), rest, x2)
        if l < depth - 1:
            x2, qkv, su, rest, flog = _projections(
                "mid", m, order[l + 1], l + 1, l + 1, ys + (w_out_p[l], norm_g, w_in_t))
    out = _projections("last", m, order[depth - 1], depth, 0, ys + (w_out_p[depth - 1], norm_g))
    return out.reshape(batch, seq, D_MODEL)
```
